```python
import math
import jax, jax.numpy as jnp
from jax import lax
import numpy as np

D_MODEL = 1024
BATCH = 2
SEQ = 8192
DEPTH = 1
DEC_BATCH = 8
DEC_SEQ = 64
PAST_LEN = 2048

CHUNK = 64
QBLOCK = 128
ROPE_THETA = 500000.0
MLA_HEADS = 8
MLA_NOPE = 64
MLA_ROPE = 32
MLA_V = 64
MLA_Q_RANK = 256
MLA_KV_RANK = 256
MLA_SCALE = (MLA_NOPE + MLA_ROPE) ** -0.5
DIFF_HEADS = 4
DIFF_D = 64
DIFF_V = 2 * DIFF_D
DIFF_ROT = DIFF_D // 4
DIFF_SCALE = DIFF_D ** -0.5
MIX_WIDTH = MLA_HEADS * MLA_V + DIFF_HEADS * DIFF_V
IN_SPLITS = (MLA_Q_RANK,
             MLA_Q_RANK + MLA_KV_RANK,
             MLA_Q_RANK + MLA_KV_RANK + MLA_ROPE,
             MLA_Q_RANK + MLA_KV_RANK + MLA_ROPE + DIFF_HEADS * 2 * DIFF_D,
             MLA_Q_RANK + MLA_KV_RANK + MLA_ROPE + 2 * DIFF_HEADS * 2 * DIFF_D)
IN_COLS = MLA_Q_RANK + MLA_KV_RANK + MLA_ROPE + 2 * DIFF_HEADS * 2 * DIFF_D + DIFF_HEADS * DIFF_V
N_GROUPS = 4
EXP_PER_GROUP = 8
N_EXPERTS = N_GROUPS * EXP_PER_GROUP
TOP_K = 2
D_EXPERT = 256
ALPHA = (2.0 * DEPTH) ** 0.25
BETA = (8.0 * DEPTH) ** -0.25
EPS_LN = 1e-5
EPS_RMS = 1e-6

kernel_name = 'hymba_mla_diffattn_hmoe_deepnorm_step'


def layer_norm(x, g, b):
    xf = x.astype(jnp.float32)
    mu = jnp.mean(xf, -1, keepdims=True)
    var = jnp.mean(jnp.square(xf - mu), -1, keepdims=True)
    return ((xf - mu) * lax.rsqrt(var + EPS_LN) * g + b).astype(x.dtype)


def rms_norm(x, g):
    xf = x.astype(jnp.float32)
    return (xf * lax.rsqrt(jnp.mean(xf * xf, -1, keepdims=True) + EPS_RMS) * g).astype(x.dtype)


def rope(x, pos):
    r = x.shape[-1]
    inv = ROPE_THETA ** (-jnp.arange(0, r, 2, dtype=jnp.float32) / r)
    ang = pos.astype(jnp.float32)[:, None] * inv
    ang = ang.reshape(ang.shape[0], *([1] * (x.ndim - 3)), r // 2)
    cos, sin = jnp.cos(ang), jnp.sin(ang)
    x1 = x[..., : r // 2].astype(jnp.float32)
    x2 = x[..., r // 2:].astype(jnp.float32)
    return jnp.concatenate([x1 * cos - x2 * sin, x1 * sin + x2 * cos], -1).astype(x.dtype)


def partial_rope(x, pos):
    return jnp.concatenate([rope(x[..., :DIFF_ROT], pos), x[..., DIFF_ROT:]], -1)


def token_projections(x, pos, w_in, q_norm, w_uq, kv_norm):
    B, S, _ = x.shape
    z = jnp.einsum('bsd,dc->bsc', x, w_in)
    c_q, c_kv, k_r, dq, dk, dv = jnp.split(z, IN_SPLITS, axis=-1)
    q = jnp.einsum('bsr,rc->bsc', rms_norm(c_q, q_norm), w_uq).reshape(B, S, MLA_HEADS, MLA_NOPE + MLA_ROPE)
    q_nope = q[..., :MLA_NOPE]
    q_pe = rope(q[..., MLA_NOPE:], pos)
    ckv = rms_norm(c_kv, kv_norm)
    kpe = rope(k_r, pos)
    dq = partial_rope(dq.reshape(B, S, DIFF_HEADS, 2, DIFF_D), pos)
    dk = partial_rope(dk.reshape(B, S, DIFF_HEADS, 2, DIFF_D), pos)
    dv = dv.reshape(B, S, DIFF_HEADS, DIFF_V)
    return q_nope, q_pe, ckv, kpe, dq, dk, dv


def mla_expand(ckv, w_ukv):
    B, K, _ = ckv.shape
    kv = jnp.einsum('bkr,rc->bkc', ckv, w_ukv).reshape(B, K, MLA_HEADS, MLA_NOPE + MLA_V)
    return kv[..., :MLA_NOPE], kv[..., MLA_NOPE:]


def mla_core(q_nope, q_pe, k_nope, k_pe, v, mask):
    s = (jnp.einsum('bqhd,bkhd->bhqk', q_nope, k_nope)
         + jnp.einsum('bqhr,bkr->bhqk', q_pe, k_pe)).astype(jnp.float32) * MLA_SCALE
    if mask is not None:
        s = jnp.where(mask, s, -jnp.inf)
    p = jax.nn.softmax(s, axis=-1).astype(v.dtype)
    return jnp.einsum('bhqk,bkhd->bqhd', p, v)


def diff_core(q, k, v, lam, mask):
    s = jnp.einsum('bqhcd,bkhcd->bchqk', q, k).astype(jnp.float32) * DIFF_SCALE
    if mask is not None:
        s = jnp.where(mask, s, -jnp.inf)
    p = jax.nn.softmax(s, axis=-1)
    a = (p[:, 0] - lam * p[:, 1]).astype(v.dtype)
    return jnp.einsum('bhqk,bkhe->bqhe', a, v)


def diff_lambda(lq1, lk1, lq2, lk2, lam_init):
    f32 = jnp.float32
    return (jnp.exp(jnp.sum(lq1.astype(f32) * lk1.astype(f32)))
            - jnp.exp(jnp.sum(lq2.astype(f32) * lk2.astype(f32))) + lam_init)


def prompt_blocks(core, q_args):
    B, S = q_args[0].shape[:2]
    nblk = S // QBLOCK
    key_chunk = jnp.arange(S) // CHUNK

    def split(a):
        return jnp.moveaxis(a.reshape(B, nblk, QBLOCK, *a.shape[2:]), 1, 0)

    def body(args):
        i, qs = args
        q_chunk = (i * QBLOCK + jnp.arange(QBLOCK)) // CHUNK
        mask = key_chunk[None, :] <= q_chunk[:, None]
        return core(*qs, mask)

    out = lax.map(body, (jnp.arange(nblk), tuple(split(a) for a in q_args)))
    return jnp.moveaxis(out, 0, 1).reshape(B, S, *out.shape[3:])


def hier_moe(h, w_rg, b_rg, w_re, b_re, w_g, w_u, w_d):
    shp = h.shape
    t = h.reshape(-1, shp[-1])
    g_logit = (t @ w_rg + b_rg).astype(jnp.float32)
    g_prob = jax.nn.softmax(g_logit, axis=-1)
    g_idx = jnp.argmax(g_logit, axis=-1)
    g_w = jnp.take_along_axis(g_prob, g_idx[:, None], axis=-1)
    e_logit = (t @ w_re + b_re).astype(jnp.float32).reshape(-1, N_GROUPS, EXP_PER_GROUP)
    e_in = jnp.take_along_axis(e_logit, g_idx[:, None, None], axis=1)[:, 0]
    e_prob = jax.nn.softmax(e_in, axis=-1)
    top_p, top_i = lax.top_k(e_prob, TOP_K)
    top_p = top_p / jnp.sum(top_p, -1, keepdims=True)
    expert_id = g_idx[:, None] * EXP_PER_GROUP + top_i
    gate = jnp.einsum('tk,tke->te', g_w * top_p,
                      jax.nn.one_hot(expert_id, N_EXPERTS, dtype=jnp.float32)).astype(h.dtype)
    hid = jax.nn.silu(jnp.einsum('td,edf->tef', t, w_g)) * jnp.einsum('td,edf->tef', t, w_u)
    y = jnp.einsum('tef,efd->td', hid * gate[..., None], w_d)
    return y.reshape(shp)


def block_output(x, mla_o, diff_o, w_out, ln1_g, ln1_b, ln2_g, ln2_b,
                 w_rg, b_rg, w_re, b_re, w_g, w_u, w_d):
    B, S = x.shape[:2]
    merged = jnp.concatenate([mla_o.reshape(B, S, -1), diff_o.reshape(B, S, -1)], -1)
    a = jnp.einsum('bsm,md->bsd', merged, w_out)
    h = layer_norm(ALPHA * x + a, ln1_g, ln1_b)
    return layer_norm(ALPHA * h + hier_moe(h, w_rg, b_rg, w_re, b_re, w_g, w_u, w_d), ln2_g, ln2_b)


def setup_inputs(seed: int = 0) -> dict:
    key = jax.random.key(seed)
    ks = jax.random.split(key, 32)
    f32 = jnp.float32

    def nrm(k, shape, scale=1.0):
        return jax.random.normal(k, shape, f32) * scale

    in_col_scale = jnp.concatenate([jnp.ones((IN_COLS - DIFF_HEADS * DIFF_V,), f32),
                                    jnp.full((DIFF_HEADS * DIFF_V,), BETA, f32)])
    ukv_col_scale = jnp.tile(jnp.concatenate([jnp.ones((MLA_NOPE,), f32), jnp.full((MLA_V,), BETA, f32)]),
                             MLA_HEADS)
    return {
        'x_prompt': nrm(ks[0], (BATCH, SEQ, D_MODEL)),
        'x_sample': nrm(ks[1], (DEC_BATCH, DEC_SEQ, D_MODEL)),
        'cache_mla_ckv': nrm(ks[2], (DEPTH, DEC_BATCH, PAST_LEN, MLA_KV_RANK)),
        'cache_mla_kpe': nrm(ks[3], (DEPTH, DEC_BATCH, PAST_LEN, MLA_ROPE)),
        'cache_diff_k': nrm(ks[4], (DEPTH, DEC_BATCH, PAST_LEN, DIFF_HEADS, 2 * DIFF_D)),
        'cache_diff_v': nrm(ks[5], (DEPTH, DEC_BATCH, PAST_LEN, DIFF_HEADS, DIFF_V), BETA),
        'w_in': nrm(ks[6], (DEPTH, D_MODEL, IN_COLS), D_MODEL ** -0.5) * in_col_scale,
        'mla_q_norm': 1.0 + nrm(ks[7], (DEPTH, MLA_Q_RANK), 0.02),
        'mla_w_uq': nrm(ks[8], (DEPTH, MLA_Q_RANK, MLA_HEADS * (MLA_NOPE + MLA_ROPE)), MLA_Q_RANK ** -0.5),
        'mla_kv_norm': 1.0 + nrm(ks[9], (DEPTH, MLA_KV_RANK), 0.02),
        'mla_w_ukv': nrm(ks[10], (DEPTH, MLA_KV_RANK, MLA_HEADS * (MLA_NOPE + MLA_V)), MLA_KV_RANK ** -0.5) * ukv_col_scale,
        'diff_lambda_q1': nrm(ks[11], (DEPTH, DIFF_D), 0.1),
        'diff_lambda_k1': nrm(ks[12], (DEPTH, DIFF_D), 0.1),
        'diff_lambda_q2': nrm(ks[13], (DEPTH, DIFF_D), 0.1),
        'diff_lambda_k2': nrm(ks[14], (DEPTH, DIFF_D), 0.1),
        'diff_subln': 1.0 + nrm(ks[15], (DEPTH, DIFF_V), 0.02),
        'w_out': nrm(ks[16], (DEPTH, MIX_WIDTH, D_MODEL), MIX_WIDTH ** -0.5 * BETA),
        'ln1_g': 1.0 + nrm(ks[17], (DEPTH, D_MODEL), 0.02),
        'ln1_b': nrm(ks[18], (DEPTH, D_MODEL), 0.02),
        'ln2_g': 1.0 + nrm(ks[19], (DEPTH, D_MODEL), 0.02),
        'ln2_b': nrm(ks[20], (DEPTH, D_MODEL), 0.02),
        'w_route_group': nrm(ks[21], (DEPTH, D_MODEL, N_GROUPS), D_MODEL ** -0.5),
        'b_route_group': nrm(ks[22], (DEPTH, N_GROUPS), 0.01),
        'w_route_expert': nrm(ks[23], (DEPTH, D_MODEL, N_EXPERTS), D_MODEL ** -0.5),
        'b_route_expert': nrm(ks[24], (DEPTH, N_EXPERTS), 0.01),
        'w_exp_gate': nrm(ks[25], (DEPTH, N_EXPERTS, D_MODEL, D_EXPERT), D_MODEL ** -0.5 * BETA),
        'w_exp_up': nrm(ks[26], (DEPTH, N_EXPERTS, D_MODEL, D_EXPERT), D_MODEL ** -0.5 * BETA),
        'w_exp_down': nrm(ks[27], (DEPTH, N_EXPERTS, D_EXPERT, D_MODEL), D_EXPERT ** -0.5 * BETA),
    }


def reference(x_prompt, x_sample, cache_mla_ckv, cache_mla_kpe, cache_diff_k, cache_diff_v,
              w_in, mla_q_norm, mla_w_uq, mla_kv_norm, mla_w_ukv,
              diff_lambda_q1, diff_lambda_k1, diff_lambda_q2, diff_lambda_k2, diff_subln,
              w_out, ln1_g, ln1_b, ln2_g, ln2_b,
              w_route_group, b_route_group, w_route_expert, b_route_expert,
              w_exp_gate, w_exp_up, w_exp_down):
    pos_p = jnp.arange(SEQ)
    pos_s = PAST_LEN + jnp.arange(x_sample.shape[1])
    hp, hs = x_prompt, x_sample
    ckv_p_l, kpe_p_l, dk_p_l, dv_p_l = [], [], [], []
    ckv_s_l, kpe_s_l, dk_s_l, dv_s_l = [], [], [], []
    for l in range(DEPTH):
        lam_init = 0.8 - 0.6 * math.exp(-0.3 * l)
        lam = diff_lambda(diff_lambda_q1[l], diff_lambda_k1[l], diff_lambda_q2[l], diff_lambda_k2[l], lam_init)
        out_args = (w_out[l], ln1_g[l], ln1_b[l], ln2_g[l], ln2_b[l],
                    w_route_group[l], b_route_group[l], w_route_expert[l], b_route_expert[l],
                    w_exp_gate[l], w_exp_up[l], w_exp_down[l])

        qn, qp, ckv, kpe, dq, dk, dv = token_projections(hp, pos_p, w_in[l], mla_q_norm[l], mla_w_uq[l], mla_kv_norm[l])
        kn, vm = mla_expand(ckv, mla_w_ukv[l])
        mla_o = prompt_blocks(lambda a, b_, m: mla_core(a, b_, kn, kpe, vm, m), (qn, qp))
        diff_o = prompt_blocks(lambda a, m: diff_core(a, dk, dv, lam, m), (dq,))
        diff_o = rms_norm(diff_o, diff_subln[l]) * (1.0 - lam_init)
        ckv_p_l.append(ckv)
        kpe_p_l.append(kpe)
        dk_p_l.append(dk.reshape(*dk.shape[:3], 2 * DIFF_D))
        dv_p_l.append(dv)
        hp = block_output(hp, mla_o, diff_o, *out_args)

        qn_s, qp_s, ckv_s, kpe_s, dq_s, dk_s, dv_s = token_projections(hs, pos_s, w_in[l], mla_q_norm[l], mla_w_uq[l], mla_kv_norm[l])
        ckv_all = jnp.concatenate([cache_mla_ckv[l], ckv_s], axis=1)
        kpe_all = jnp.concatenate([cache_mla_kpe[l], kpe_s], axis=1)
        kn_all, vm_all = mla_expand(ckv_all, mla_w_ukv[l])
        mla_o_s = mla_core(qn_s, qp_s, kn_all, kpe_all, vm_all, None)
        past_k = cache_diff_k[l]
        dk_all = jnp.concatenate([past_k.reshape(*past_k.shape[:3], 2, DIFF_D), dk_s], axis=1)
        dv_all = jnp.concatenate([cache_diff_v[l], dv_s], axis=1)
        diff_o_s = rms_norm(diff_core(dq_s, dk_all, dv_all, lam, None), diff_subln[l]) * (1.0 - lam_init)
        ckv_s_l.append(ckv_s)
        kpe_s_l.append(kpe_s)
        dk_s_l.append(dk_s.reshape(*dk_s.shape[:3], 2 * DIFF_D))
        dv_s_l.append(dv_s)
        hs = block_output(hs, mla_o_s, diff_o_s, *out_args)

    return (hp, hs,
            jnp.stack(ckv_p_l), jnp.stack(kpe_p_l), jnp.stack(dk_p_l), jnp.stack(dv_p_l),
            jnp.stack(ckv_s_l), jnp.stack(kpe_s_l), jnp.stack(dk_s_l), jnp.stack(dv_s_l))
```

```python
import functools
import math

import jax
import jax.numpy as jnp
from jax import lax
from jax.experimental import pallas as pl
from jax.experimental.pallas import tpu as pltpu

F32 = jnp.float32
BF16 = jnp.bfloat16

D_MODEL = 1024
CHUNK = 64
ROPE_THETA = 500000.0
MLA_HEADS = 8
MLA_NOPE = 64
MLA_ROPE = 32
MLA_V = 64
MLA_Q_RANK = 256
MLA_KV_RANK = 256
MLA_SCALE = (MLA_NOPE + MLA_ROPE) ** -0.5
DIFF_HEADS = 4
DIFF_D = 64
DIFF_V = 2 * DIFF_D
DIFF_ROT = DIFF_D // 4
DIFF_SCALE = DIFF_D ** -0.5
N_GROUPS = 4
EXP_PER_GROUP = 8
N_EXPERTS = N_GROUPS * EXP_PER_GROUP
D_EXPERT = 256
EPS_LN = 1e-5
EPS_RMS = 1e-6

LANES = 128
HEAD_PAD = LANES
VMEM_LIMIT = 56 * 1024 * 1024

_CQ, _CKV, _DQ, _DK, _DV, _KR, _IN_COLS_R = 0, 256, 512, 1024, 1536, 2048, 2176


def _cparams(sem):
    return pltpu.CompilerParams(dimension_semantics=sem, vmem_limit_bytes=VMEM_LIMIT)


def _rms(x, g):
    return x * lax.rsqrt(jnp.mean(x * x, axis=-1, keepdims=True) + EPS_RMS) * g


def _layer_norm(x, g, b):
    mu = jnp.mean(x, axis=-1, keepdims=True)
    xc = x - mu
    var = jnp.mean(xc * xc, axis=-1, keepdims=True)
    return xc * lax.rsqrt(var + EPS_LN) * g + b


def _rope_coeffs(cos_t, sin_t, lo, half, period):
    lane = lax.broadcasted_iota(jnp.int32, cos_t.shape, 1) & (period - 1)
    is1 = (lane >= lo) & (lane < lo + half)
    is2 = (lane >= lo + half) & (lane < lo + 2 * half)
    c = jnp.where(is1 | is2, cos_t, 1.0)
    a = jnp.where(is1, -sin_t, 0.0)
    b = jnp.where(is2, sin_t, 0.0)
    return c, a, b


def _rope_apply(x, coeffs, half):
    c, a, b = coeffs
    return x * c + pltpu.roll(x, LANES - half, 1) * a + pltpu.roll(x, half, 1) * b


def _proj_kernel(x_ref, w_in_ref, qn_ref, w_uq_ref, kvn_ref, w_ukv_ref, e_ref,
                 cm_ref, sm_ref, cd_ref, sd_ref,
                 ckv_ref, kpe_ref, dk_ref, dv_ref, qm_ref, km_ref, vm_ref, dqz_ref, dkb_ref, dvb_ref):
    xb = x_ref[...].astype(BF16)
    z = jnp.dot(xb, w_in_ref[...], preferred_element_type=F32)

    cos_m, sin_m = cm_ref[...], sm_ref[...]
    cos_d, sin_d = cd_ref[...], sd_ref[...]
    half_m = MLA_ROPE // 2
    half_d = DIFF_ROT // 2
    coef_q = _rope_coeffs(cos_m, sin_m, MLA_NOPE, half_m, LANES)
    coef_k = _rope_coeffs(cos_m, sin_m, 0, half_m, LANES)
    coef_d = _rope_coeffs(cos_d, sin_d, 0, half_d, DIFF_D)

    cq = _rms(z[:, _CQ:_CQ + MLA_Q_RANK], qn_ref[...])
    q = jnp.dot(cq.astype(BF16), w_uq_ref[...], preferred_element_type=F32)
    for h in range(MLA_HEADS):
        sl = slice(h * HEAD_PAD, (h + 1) * HEAD_PAD)
        qm_ref[:, sl] = (_rope_apply(q[:, sl], coef_q, half_m) * MLA_SCALE).astype(BF16)

    ckv = _rms(z[:, _CKV:_CKV + MLA_KV_RANK], kvn_ref[...])
    ckv_ref[...] = ckv
    kslab = _rope_apply(z[:, _KR:_KR + LANES], coef_k, half_m)
    kpe_ref[...] = kslab[:, :MLA_ROPE]
    kv = jnp.dot(ckv.astype(BF16), w_ukv_ref[...], preferred_element_type=F32)
    k_full = kv[:, :MLA_HEADS * HEAD_PAD] + jnp.dot(kslab.astype(BF16), e_ref[...], preferred_element_type=F32)
    km_ref[...] = k_full.astype(BF16)
    vm_ref[...] = kv[:, MLA_HEADS * HEAD_PAD:].astype(BF16)

    lane = lax.broadcasted_iota(jnp.int32, (z.shape[0], LANES), 1)
    first = lane < DIFF_D
    for h in range(DIFF_HEADS):
        sl = slice(h * LANES, (h + 1) * LANES)
        dq = _rope_apply(z[:, _DQ + h * LANES:_DQ + (h + 1) * LANES], coef_d, half_d) * DIFF_SCALE
        dqz_ref[:, 2 * h * LANES:(2 * h + 1) * LANES] = jnp.where(first, dq, 0.0).astype(BF16)
        dqz_ref[:, (2 * h + 1) * LANES:(2 * h + 2) * LANES] = jnp.where(first, 0.0, dq).astype(BF16)
        dk = _rope_apply(z[:, _DK + h * LANES:_DK + (h + 1) * LANES], coef_d, half_d)
        dk_ref[:, sl] = dk
        dkb_ref[:, sl] = dk.astype(BF16)
    dv = z[:, _DV:_DV + DIFF_HEADS * DIFF_V]
    dv_ref[...] = dv
    dvb_ref[...] = dv.astype(BF16)


def _proj(x, tabs, w, *, tm):
    t = x.shape[0]
    n_pos_blocks = tabs[0].shape[0] // tm
    row = lambda n: pl.BlockSpec((tm, n), lambda i: (i, 0))
    full = lambda a: pl.BlockSpec(a.shape, lambda i: (0,) * a.ndim)
    tab = pl.BlockSpec((tm, LANES), lambda i: (i % n_pos_blocks, 0))
    weights = (w['w_in'], w['q_norm'], w['w_uq'], w['kv_norm'], w['w_ukv'], w['e_place'])
    out_cols = ((MLA_KV_RANK, F32), (MLA_ROPE, F32), (512, F32), (512, F32),
                (1024, BF16), (1024, BF16), (512, BF16), (1024, BF16), (512, BF16), (512, BF16))
    return pl.pallas_call(
        _proj_kernel,
        grid=(t // tm,),
        in_specs=[row(D_MODEL)] + [full(a) for a in weights] + [tab] * 4,
        out_specs=[row(n) for n, _ in out_cols],
        out_shape=[jax.ShapeDtypeStruct((t, n), d) for n, d in out_cols],
        compiler_params=_cparams(("parallel",)),
        name="proj",
    )(x, *weights, *tabs)


def _expand_kernel(ckv_ref, kpe_ref, w_ukv_ref, e_ref, km_ref, vm_ref):
    kv = jnp.dot(ckv_ref[...].astype(BF16), w_ukv_ref[...], preferred_element_type=F32)
    k_full = kv[:, :MLA_HEADS * HEAD_PAD] + jnp.dot(kpe_ref[...].astype(BF16), e_ref[:MLA_ROPE, :],
                                                   preferred_element_type=F32)
    km_ref[...] = k_full.astype(BF16)
    vm_ref[...] = kv[:, MLA_HEADS * HEAD_PAD:].astype(BF16)


def _expand(ckv, kpe, w, *, tm):
    r = ckv.shape[0]
    row = lambda n: pl.BlockSpec((tm, n), lambda i: (i, 0))
    full = lambda a: pl.BlockSpec(a.shape, lambda i: (0,) * a.ndim)
    return pl.pallas_call(
        _expand_kernel,
        grid=(r // tm,),
        in_specs=[row(MLA_KV_RANK), row(MLA_ROPE), full(w['w_ukv']), full(w['e_place'])],
        out_specs=[row(1024), row(512)],
        out_shape=[jax.ShapeDtypeStruct((r, 1024), BF16), jax.ShapeDtypeStruct((r, 512), BF16)],
        compiler_params=_cparams(("parallel",)),
        name="expand",
    )(ckv, kpe, w['w_ukv'], w['e_place'])


def _nt_dot(a, b):
    return lax.dot_general(a, b, (((1,), (1,)), ((), ())), preferred_element_type=F32)


def _chunk_mask(t):
    shift = CHUNK.bit_length() - 1
    r = lax.broadcasted_iota(jnp.int32, (t, t), 0) >> shift
    c = lax.broadcasted_iota(jnp.int32, (t, t), 1) >> shift
    return c <= r


def _online_update(carry, s, v):
    m, l, acc = carry
    m_new = jnp.maximum(m, jnp.max(s, axis=-1, keepdims=True))
    alpha = jnp.exp(m - m_new)
    p = jnp.exp(s - m_new)
    l = alpha * l + jnp.sum(p, axis=-1, keepdims=True)
    acc = alpha * acc + jnp.dot(p.astype(BF16), v, preferred_element_type=F32)
    return m_new, l, acc


def _mla_flash_kernel(q_ref, k_ref, v_ref, o_ref, *, t, heads):
    i = pl.program_id(2)
    mask = _chunk_mask(t)
    for h in range(heads):
        q = q_ref[0, :, h * HEAD_PAD:(h + 1) * HEAD_PAD]

        def tile(j, carry, masked):
            rows = pl.ds(pl.multiple_of(j * t, t), t)
            s = _nt_dot(q, k_ref[0, rows, h * HEAD_PAD:(h + 1) * HEAD_PAD])
            if masked:
                s = jnp.where(mask, s, -jnp.inf)
            return _online_update(carry, s, v_ref[0, rows, h * MLA_V:(h + 1) * MLA_V])

        init = (jnp.full((t, 1), -jnp.inf, F32), jnp.zeros((t, 1), F32), jnp.zeros((t, MLA_V), F32))
        carry = lax.fori_loop(0, i, functools.partial(tile, masked=False), init)
        _, l, acc = tile(i, carry, True)
        o_ref[0, :, h * MLA_V:(h + 1) * MLA_V] = (acc / l).astype(BF16)


def _mla_flash(qm, km, vm, *, t, heads):
    b, s, _ = qm.shape
    return pl.pallas_call(
        functools.partial(_mla_flash_kernel, t=t, heads=heads),
        grid=(b, MLA_HEADS // heads, s // t),
        in_specs=[pl.BlockSpec((1, t, heads * HEAD_PAD), lambda bi, hi, i: (bi, i, hi)),
                  pl.BlockSpec((1, s, heads * HEAD_PAD), lambda bi, hi, i: (bi, 0, hi)),
                  pl.BlockSpec((1, s, heads * MLA_V), lambda bi, hi, i: (bi, 0, hi))],
        out_specs=pl.BlockSpec((1, t, heads * MLA_V), lambda bi, hi, i: (bi, i, hi)),
        out_shape=jax.ShapeDtypeStruct((b, s, MLA_HEADS * MLA_V), BF16),
        compiler_params=_cparams(("parallel", "parallel", "arbitrary")),
        name="mla_flash",
    )(qm, km, vm)


def _diff_lambda(lam_ref, lam_init):
    lq1, lk1, lq2, lk2 = (lam_ref[r:r + 1, :] for r in range(4))
    return (jnp.exp(jnp.sum(lq1 * lk1, axis=-1, keepdims=True))
            - jnp.exp(jnp.sum(lq2 * lk2, axis=-1, keepdims=True)) + lam_init)


def _diff_finish(o1, o2, lam, subln, lam_init):
    o = o1 - lam * o2
    return _rms(o, subln) * (1.0 - lam_init)


def _diff_flash_kernel(lam_ref, subln_ref, q_ref, k_ref, v_ref, o_ref, *, t, lam_init):
    i = pl.program_id(2)
    mask = _chunk_mask(t)
    q1 = q_ref[0, :, :LANES]
    q2 = q_ref[0, :, LANES:]

    def tile(j, carry, masked):
        c1, c2 = carry
        rows = pl.ds(pl.multiple_of(j * t, t), t)
        k = k_ref[0, rows, :]
        v = v_ref[0, rows, :]
        s1 = _nt_dot(q1, k)
        s2 = _nt_dot(q2, k)
        if masked:
            s1 = jnp.where(mask, s1, -jnp.inf)
            s2 = jnp.where(mask, s2, -jnp.inf)
        return _online_update(c1, s1, v), _online_update(c2, s2, v)

    init1 = (jnp.full((t, 1), -jnp.inf, F32), jnp.zeros((t, 1), F32), jnp.zeros((t, DIFF_V), F32))
    carry = lax.fori_loop(0, i, functools.partial(tile, masked=False), (init1, init1))
    (_, l1, a1), (_, l2, a2) = tile(i, carry, True)
    lam = _diff_lambda(lam_ref, lam_init)
    o_ref[0] = _diff_finish(a1 / l1, a2 / l2, lam, subln_ref[...], lam_init).astype(BF16)


def _diff_flash(lam_rows, subln, dqz, dkb, dvb, *, t, lam_init):
    b, s, _ = dqz.shape
    full = lambda a: pl.BlockSpec(a.shape, lambda bi, hi, i: (0,) * a.ndim)
    return pl.pallas_call(
        functools.partial(_diff_flash_kernel, t=t, lam_init=lam_init),
        grid=(b, DIFF_HEADS, s // t),
        in_specs=[full(lam_rows), full(subln),
                  pl.BlockSpec((1, t, 2 * LANES), lambda bi, hi, i: (bi, i, hi)),
                  pl.BlockSpec((1, s, LANES), lambda bi, hi, i: (bi, 0, hi)),
                  pl.BlockSpec((1, s, LANES), lambda bi, hi, i: (bi, 0, hi))],
        out_specs=pl.BlockSpec((1, t, LANES), lambda bi, hi, i: (bi, i, hi)),
        out_shape=jax.ShapeDtypeStruct((b, s, DIFF_HEADS * DIFF_V), BF16),
        compiler_params=_cparams(("parallel", "parallel", "arbitrary")),
        name="diff_flash",
    )(lam_rows, subln, dqz, dkb, dvb)


def _two_part_softmax_pv(s_past, s_new, v_past, v_new):
    m = jnp.maximum(jnp.max(s_past, axis=-1, keepdims=True), jnp.max(s_new, axis=-1, keepdims=True))
    p_past = jnp.exp(s_past - m)
    p_new = jnp.exp(s_new - m)
    l = jnp.sum(p_past, axis=-1, keepdims=True) + jnp.sum(p_new, axis=-1, keepdims=True)
    acc = (jnp.dot(p_past.astype(BF16), v_past, preferred_element_type=F32)
           + jnp.dot(p_new.astype(BF16), v_new, preferred_element_type=F32))
    return acc / l


def _mla_step_kernel(q_ref, kp_ref, vp_ref, kn_ref, vn_ref, o_ref):
    for h in range(MLA_HEADS):
        ks = slice(h * HEAD_PAD, (h + 1) * HEAD_PAD)
        vs = slice(h * MLA_V, (h + 1) * MLA_V)
        q = q_ref[0, :, ks]
        o = _two_part_softmax_pv(_nt_dot(q, kp_ref[0, :, ks]), _nt_dot(q, kn_ref[0, :, ks]),
                                 vp_ref[0, :, vs], vn_ref[0, :, vs])
        o_ref[0, :, vs] = o.astype(BF16)


def _mla_step(qm, km_past, vm_past, km_new, vm_new):
    b, n, _ = qm.shape
    blk = lambda a: pl.BlockSpec((1,) + a.shape[1:], lambda bi: (bi, 0, 0))
    args = (qm, km_past, vm_past, km_new, vm_new)
    return pl.pallas_call(
        _mla_step_kernel,
        grid=(b,),
        in_specs=[blk(a) for a in args],
        out_specs=pl.BlockSpec((1, n, MLA_HEADS * MLA_V), lambda bi: (bi, 0, 0)),
        out_shape=jax.ShapeDtypeStruct((b, n, MLA_HEADS * MLA_V), BF16),
        compiler_params=_cparams(("parallel",)),
        name="mla_step",
    )(*args)


def _diff_step_kernel(lam_ref, subln_ref, q_ref, kp_ref, vp_ref, kn_ref, vn_ref, o_ref, *, lam_init):
    lam = _diff_lambda(lam_ref, lam_init)
    q1 = q_ref[0, :, :LANES]
    q2 = q_ref[0, :, LANES:]
    kp = kp_ref[0].astype(BF16)
    vp = vp_ref[0].astype(BF16)
    kn = kn_ref[0]
    vn = vn_ref[0]
    o1 = _two_part_softmax_pv(_nt_dot(q1, kp), _nt_dot(q1, kn), vp, vn)
    o2 = _two_part_softmax_pv(_nt_dot(q2, kp), _nt_dot(q2, kn), vp, vn)
    o_ref[0] = _diff_finish(o1, o2, lam, subln_ref[...], lam_init).astype(BF16)


def _diff_step(lam_rows, subln, dqz, k_past, v_past, dkb, dvb, *, lam_init):
    b, n, _ = dqz.shape
    p = k_past.shape[1]
    full = lambda a: pl.BlockSpec(a.shape, lambda bi, hi: (0,) * a.ndim)
    head = lambda rows, width: pl.BlockSpec((1, rows, width), lambda bi, hi: (bi, 0, hi))
    return pl.pallas_call(
        functools.partial(_diff_step_kernel, lam_init=lam_init),
        grid=(b, DIFF_HEADS),
        in_specs=[full(lam_rows), full(subln), head(n, 2 * LANES), head(p, LANES), head(p, LANES),
                  head(n, LANES), head(n, LANES)],
        out_specs=head(n, LANES),
        out_shape=jax.ShapeDtypeStruct((b, n, DIFF_HEADS * DIFF_V), BF16),
        compiler_params=_cparams(("parallel", "parallel")),
        name="diff_step",
    )(lam_rows, subln, dqz, k_past, v_past, dkb, dvb)


def _first_index_of_max(vals, lane):
    m = jnp.max(vals, axis=-1, keepdims=True)
    idx = jnp.min(jnp.where(vals == m, lane, float(LANES)), axis=-1, keepdims=True)
    return m, idx


def _post_attn_kernel(x_ref, mla_ref, diff_ref, wo_a_ref, wo_b_ref, g_ref, b_ref, wr_ref, br_ref,
                      h_ref, gate_ref, *, alpha):
    a = (jnp.dot(mla_ref[...], wo_a_ref[...], preferred_element_type=F32)
         + jnp.dot(diff_ref[...], wo_b_ref[...], preferred_element_type=F32))
    h = _layer_norm(alpha * x_ref[...] + a, g_ref[...], b_ref[...])
    h_ref[...] = h

    logit = jnp.dot(h.astype(BF16), wr_ref[...], preferred_element_type=F32) + br_ref[...]
    lane_i = lax.broadcasted_iota(jnp.int32, logit.shape, 1)
    lane = lane_i.astype(F32)
    group_of_lane = (lane_i >> (EXP_PER_GROUP.bit_length() - 1)).astype(F32)
    neg = -jnp.inf
    g_logit = jnp.where((lane_i >= N_EXPERTS) & (lane_i < N_EXPERTS + N_GROUPS), logit, neg)
    g_max, g_lane = _first_index_of_max(g_logit, lane)
    g_w = 1.0 / jnp.sum(jnp.exp(g_logit - g_max), axis=-1, keepdims=True)
    g_idx = g_lane - float(N_EXPERTS)
    e_logit = jnp.where(group_of_lane == g_idx, logit, neg)
    m1, i1 = _first_index_of_max(e_logit, lane)
    m2, i2 = _first_index_of_max(jnp.where(lane == i1, neg, e_logit), lane)
    r = jnp.exp(m2 - m1)
    p1 = 1.0 / (1.0 + r)
    p2 = r / (1.0 + r)
    gate_ref[...] = g_w * (jnp.where(lane == i1, p1, 0.0) + jnp.where(lane == i2, p2, 0.0))


def _post_attn(x, mla_o, diff_o, w, *, tm, alpha):
    t = x.shape[0]
    row = lambda n: pl.BlockSpec((tm, n), lambda i: (i, 0))
    full = lambda a: pl.BlockSpec(a.shape, lambda i: (0,) * a.ndim)
    weights = (w['w_out_a'], w['w_out_b'], w['ln1_g'], w['ln1_b'], w['w_route'], w['b_route'])
    return pl.pallas_call(
        functools.partial(_post_attn_kernel, alpha=alpha),
        grid=(t // tm,),
        in_specs=[row(D_MODEL), row(512), row(512)] + [full(a) for a in weights],
        out_specs=[row(D_MODEL), row(LANES)],
        out_shape=[jax.ShapeDtypeStruct((t, D_MODEL), F32), jax.ShapeDtypeStruct((t, LANES), F32)],
        compiler_params=_cparams(("parallel",)),
        name="post_attn",
    )(x, mla_o, diff_o, *weights)


def _moe_kernel(h_ref, gate_ref, wg_ref, wu_ref, wd_ref, g_ref, b_ref, y_ref, hb_ref, acc_ref, *, alpha):
    e = pl.program_id(1)

    @pl.when(e == 0)
    def _():
        hb_ref[...] = h_ref[...].astype(BF16)
        acc_ref[...] = jnp.zeros_like(acc_ref)

    hb = hb_ref[...]
    lane = lax.broadcasted_iota(jnp.int32, gate_ref.shape, 1)
    gate_e = jnp.sum(jnp.where(lane == e, gate_ref[...], 0.0), axis=-1, keepdims=True)
    hid = (jax.nn.silu(jnp.dot(hb, wg_ref[0], preferred_element_type=F32))
           * jnp.dot(hb, wu_ref[0], preferred_element_type=F32))
    acc_ref[...] += jnp.dot((hid * gate_e).astype(BF16), wd_ref[0], preferred_element_type=F32)

    @pl.when(e == N_EXPERTS - 1)
    def _():
        y_ref[...] = _layer_norm(alpha * h_ref[...] + acc_ref[...], g_ref[...], b_ref[...])


def _moe(h, gate, w, *, tm, alpha):
    t = h.shape[0]
    row = lambda n: pl.BlockSpec((tm, n), lambda i, e: (i, 0))
    full = lambda a: pl.BlockSpec(a.shape, lambda i, e: (0,) * a.ndim)
    exp = lambda a: pl.BlockSpec((1,) + a.shape[1:], lambda i, e: (e, 0, 0))
    return pl.pallas_call(
        functools.partial(_moe_kernel, alpha=alpha),
        grid=(t // tm, N_EXPERTS),
        in_specs=[row(D_MODEL), row(LANES), exp(w['w_gate']), exp(w['w_up']), exp(w['w_down']),
                  full(w['ln2_g']), full(w['ln2_b'])],
        out_specs=row(D_MODEL),
        out_shape=jax.ShapeDtypeStruct((t, D_MODEL), F32),
        scratch_shapes=[pltpu.VMEM((tm, D_MODEL), BF16), pltpu.VMEM((tm, D_MODEL), F32)],
        compiler_params=_cparams(("parallel", "arbitrary")),
        name="moe",
    )(h, gate, w['w_gate'], w['w_up'], w['w_down'], w['ln2_g'], w['ln2_b'])


def _rope_tables(pos):
    pos = pos.astype(F32)[:, None]
    lane = jnp.arange(LANES)
    inv_m = ROPE_THETA ** (-jnp.arange(0, MLA_ROPE, 2, dtype=F32) / MLA_ROPE)
    ang_m = (pos * inv_m)[:, lane % (MLA_ROPE // 2)]
    inv_d = ROPE_THETA ** (-jnp.arange(0, DIFF_ROT, 2, dtype=F32) / DIFF_ROT)
    ang_d = (pos * inv_d)[:, lane % (DIFF_ROT // 2)]
    return jnp.cos(ang_m), jnp.sin(ang_m), jnp.cos(ang_d), jnp.sin(ang_d)


def _prep_weights(l, w_in, mla_q_norm, mla_w_uq, mla_kv_norm, mla_w_ukv, diff_subln, w_out, ln1_g, ln1_b,
                  ln2_g, ln2_b, w_route_group, b_route_group, w_route_expert, b_route_expert,
                  w_exp_gate, w_exp_up, w_exp_down, lam_rows):
    wi = w_in[l]
    s0, s1, s2, s3, s4 = 256, 512, 544, 1056, 1568
    w_in_r = jnp.concatenate([wi[:, :s0], wi[:, s0:s1], wi[:, s2:s3], wi[:, s3:s4], wi[:, s4:],
                              wi[:, s1:s2], jnp.zeros((D_MODEL, LANES - MLA_ROPE), F32)], axis=1)
    w_uq = jnp.pad(mla_w_uq[l].reshape(MLA_Q_RANK, MLA_HEADS, MLA_NOPE + MLA_ROPE),
                   ((0, 0), (0, 0), (0, HEAD_PAD - MLA_NOPE - MLA_ROPE))).reshape(MLA_Q_RANK, -1)
    ukv = mla_w_ukv[l].reshape(MLA_KV_RANK, MLA_HEADS, MLA_NOPE + MLA_V)
    w_uk = jnp.pad(ukv[..., :MLA_NOPE], ((0, 0), (0, 0), (0, HEAD_PAD - MLA_NOPE))).reshape(MLA_KV_RANK, -1)
    w_uv = ukv[..., MLA_NOPE:].reshape(MLA_KV_RANK, -1)
    rows = jnp.arange(LANES)[:, None]
    cols = jnp.arange(MLA_HEADS * HEAD_PAD)[None, :]
    e_place = ((rows < MLA_ROPE) & (cols % HEAD_PAD == rows + MLA_NOPE)).astype(BF16)
    w_route = jnp.concatenate([w_route_expert[l], w_route_group[l],
                               jnp.zeros((D_MODEL, LANES - N_EXPERTS - N_GROUPS), F32)], axis=1)
    b_route = jnp.concatenate([b_route_expert[l], b_route_group[l],
                               jnp.zeros((LANES - N_EXPERTS - N_GROUPS,), F32)])[None, :]
    return dict(
        w_in=w_in_r.astype(BF16), q_norm=mla_q_norm[l][None, :], w_uq=w_uq.astype(BF16),
        kv_norm=mla_kv_norm[l][None, :], w_ukv=jnp.concatenate([w_uk, w_uv], axis=1).astype(BF16),
        e_place=e_place, subln=diff_subln[l][None, :], lam_rows=lam_rows,
        w_out_a=w_out[l][:MLA_HEADS * MLA_V].astype(BF16), w_out_b=w_out[l][MLA_HEADS * MLA_V:].astype(BF16),
        ln1_g=ln1_g[l][None, :], ln1_b=ln1_b[l][None, :], ln2_g=ln2_g[l][None, :], ln2_b=ln2_b[l][None, :],
        w_route=w_route.astype(BF16), b_route=b_route,
        w_gate=w_exp_gate[l].astype(BF16), w_up=w_exp_up[l].astype(BF16), w_down=w_exp_down[l].astype(BF16))


def _block_output(x, mla_o, diff_o, w, alpha):
    t = x.shape[0]
    h, gate = _post_attn(x, mla_o.reshape(t, -1), diff_o.reshape(t, -1), w, tm=min(512, t), alpha=alpha)
    return _moe(h, gate, w, tm=min(1024, t), alpha=alpha)


def kernel(x_prompt, x_sample, cache_mla_ckv, cache_mla_kpe, cache_diff_k, cache_diff_v, w_in, mla_q_norm, mla_w_uq, mla_kv_norm, mla_w_ukv, diff_lambda_q1, diff_lambda_k1, diff_lambda_q2, diff_lambda_k2, diff_subln, w_out, ln1_g, ln1_b, ln2_g, ln2_b, w_route_group, b_route_group, w_route_expert, b_route_expert, w_exp_gate, w_exp_up, w_exp_down):
    depth = w_in.shape[0]
    bp, sp, _ = x_prompt.shape
    bs, ss, _ = x_sample.shape
    past = cache_mla_ckv.shape[2]
    alpha = (2.0 * depth) ** 0.25
    tabs_p = _rope_tables(jnp.arange(sp))
    tabs_s = tuple(jnp.tile(t, (bs, 1)) for t in _rope_tables(past + jnp.arange(ss)))
    hp = x_prompt.reshape(bp * sp, D_MODEL)
    hs = x_sample.reshape(bs * ss, D_MODEL)
    outs = [[] for _ in range(8)]
    for l in range(depth):
        lam_init = 0.8 - 0.6 * math.exp(-0.3 * l)
        lam_rows = jnp.stack([diff_lambda_q1[l], diff_lambda_k1[l], diff_lambda_q2[l], diff_lambda_k2[l]])
        w = _prep_weights(l, w_in, mla_q_norm, mla_w_uq, mla_kv_norm, mla_w_ukv, diff_subln, w_out,
                          ln1_g, ln1_b, ln2_g, ln2_b, w_route_group, b_route_group, w_route_expert,
                          b_route_expert, w_exp_gate, w_exp_up, w_exp_down, lam_rows)

        ckv, kpe, dk, dv, qm, km, vm, dqz, dkb, dvb = _proj(hp, tabs_p, w, tm=256)
        b3 = lambda a: a.reshape(bp, sp, -1)
        mla_o = _mla_flash(b3(qm), b3(km), b3(vm), t=256, heads=2)
        diff_o = _diff_flash(w['lam_rows'], w['subln'], b3(dqz), b3(dkb), b3(dvb), t=256, lam_init=lam_init)
        outs[0].append(ckv.reshape(bp, sp, MLA_KV_RANK))
        outs[1].append(kpe.reshape(bp, sp, MLA_ROPE))
        outs[2].append(dk.reshape(bp, sp, DIFF_HEADS, 2 * DIFF_D))
        outs[3].append(dv.reshape(bp, sp, DIFF_HEADS, DIFF_V))
        hp = _block_output(hp, mla_o, diff_o, w, alpha)

        ckv_s, kpe_s, dk_s, dv_s, qm_s, km_s, vm_s, dqz_s, dkb_s, dvb_s = _proj(hs, tabs_s, w, tm=256)
        km_past, vm_past = _expand(cache_mla_ckv[l].reshape(bs * past, MLA_KV_RANK),
                                   cache_mla_kpe[l].reshape(bs * past, MLA_ROPE), w, tm=512)
        s3 = lambda a: a.reshape(bs, ss, -1)
        p3 = lambda a: a.reshape(bs, past, -1)
        mla_o_s = _mla_step(s3(qm_s), p3(km_past), p3(vm_past), s3(km_s), s3(vm_s))
        diff_o_s = _diff_step(w['lam_rows'], w['subln'], s3(dqz_s), p3(cache_diff_k[l]), p3(cache_diff_v[l]),
                              s3(dkb_s), s3(dvb_s), lam_init=lam_init)
        outs[4].append(ckv_s.reshape(bs, ss, MLA_KV_RANK))
        outs[5].append(kpe_s.reshape(bs, ss, MLA_ROPE))
        outs[6].append(dk_s.reshape(bs, ss, DIFF_HEADS, 2 * DIFF_D))
        outs[7].append(dv_s.reshape(bs, ss, DIFF_HEADS, DIFF_V))
        hs = _block_output(hs, mla_o_s, diff_o_s, w, alpha)

    return (hp.reshape(bp, sp, D_MODEL), hs.reshape(bs, ss, D_MODEL)) + tuple(jnp.stack(o) for o in outs)
```

```python
import functools
import math

import jax
import jax.numpy as jnp
from jax import lax
from jax.experimental import pallas as pl
from jax.experimental.pallas import tpu as pltpu

F32 = jnp.float32
BF16 = jnp.bfloat16

D_MODEL = 1024
CHUNK = 64
ROPE_THETA = 500000.0
MLA_HEADS = 8
MLA_NOPE = 64
MLA_ROPE = 32
MLA_V = 64
MLA_Q_RANK = 256
MLA_KV_RANK = 256
MLA_SCALE = (MLA_NOPE + MLA_ROPE) ** -0.5
DIFF_HEADS = 4
DIFF_D = 64
DIFF_V = 2 * DIFF_D
DIFF_ROT = DIFF_D // 4
DIFF_SCALE = DIFF_D ** -0.5
N_GROUPS = 4
EXP_PER_GROUP = 8
N_EXPERTS = N_GROUPS * EXP_PER_GROUP
D_EXPERT = 256
EPS_LN = 1e-5
EPS_RMS = 1e-6
LOG2E = math.log2(math.e)

LANES = 128
HEAD_PAD = LANES
ATTN_TILE = 256
VMEM_LIMIT = 56 * 1024 * 1024

_CQ, _CKV, _DK, _DV, _KR, _IN_COLS_R = 0, 256, 512, 1024, 1536, 1664


def _cparams(sem):
    return pltpu.CompilerParams(dimension_semantics=sem, vmem_limit_bytes=VMEM_LIMIT)


def _rms(x, g):
    return x * lax.rsqrt(jnp.mean(x * x, axis=-1, keepdims=True) + EPS_RMS) * g


def _layer_norm(x, g, b):
    mu = jnp.mean(x, axis=-1, keepdims=True)
    xc = x - mu
    var = jnp.mean(xc * xc, axis=-1, keepdims=True)
    return xc * lax.rsqrt(var + EPS_LN) * g + b


def _nt_dot(a, b):
    return lax.dot_general(a, b, (((1,), (1,)), ((), ())), preferred_element_type=F32)


def _rope_coeffs(cos_t, sin_t, lo, half, period):
    lane = lax.broadcasted_iota(jnp.int32, cos_t.shape, 1) & (period - 1)
    is1 = (lane >= lo) & (lane < lo + half)
    is2 = (lane >= lo + half) & (lane < lo + 2 * half)
    c = jnp.where(is1 | is2, cos_t, 1.0)
    a = jnp.where(is1, -sin_t, 0.0)
    b = jnp.where(is2, sin_t, 0.0)
    return c, a, b


def _rope_apply(x, coeffs, half):
    c, a, b = coeffs
    return x * c + pltpu.roll(x, LANES - half, 1) * a + pltpu.roll(x, half, 1) * b


def _rope_rows(x, cos_t, sin_t):
    half = cos_t.shape[0]
    x1, x2 = x[:half], x[half:]
    return x1 * cos_t - x2 * sin_t, x1 * sin_t + x2 * cos_t


def _proj_kernel(x_ref, w_in_ref, w_dqt_ref, qn_ref, w_uqt_ref, kvn_ref, w_ukv_ref, w_uvt_ref, e_ref,
                 cm_ref, sm_ref, cd_ref, sd_ref, cmt_ref, smt_ref, cdt_ref, sdt_ref,
                 ckv_ref, kpe_ref, dk_ref, dv_ref, qt_ref, km_ref, vm_ref, vt_ref,
                 dqt_ref, dkb_ref, dvb_ref, dvt_ref):
    xb = x_ref[...].astype(BF16)
    tm = xb.shape[0]
    z = jnp.dot(xb, w_in_ref[...], preferred_element_type=F32)
    half_m = MLA_ROPE // 2
    half_d = DIFF_ROT // 2
    coef_k = _rope_coeffs(cm_ref[...], sm_ref[...], 0, half_m, LANES)
    coef_d = _rope_coeffs(cd_ref[...], sd_ref[...], 0, half_d, DIFF_D)

    cq = _rms(z[:, _CQ:_CQ + MLA_Q_RANK], qn_ref[...])
    qt = _nt_dot(w_uqt_ref[...], cq.astype(BF16)) * (MLA_SCALE * LOG2E)
    cos_mt, sin_mt = cmt_ref[...], smt_ref[...]
    for h in range(MLA_HEADS):
        r0 = h * HEAD_PAD
        qt_ref[0, r0:r0 + MLA_NOPE, :] = qt[r0:r0 + MLA_NOPE].astype(BF16)
        o1, o2 = _rope_rows(qt[r0 + MLA_NOPE:r0 + MLA_NOPE + MLA_ROPE], cos_mt, sin_mt)
        qt_ref[0, r0 + MLA_NOPE:r0 + MLA_NOPE + half_m, :] = o1.astype(BF16)
        qt_ref[0, r0 + MLA_NOPE + half_m:r0 + MLA_NOPE + MLA_ROPE, :] = o2.astype(BF16)
        qt_ref[0, r0 + MLA_NOPE + MLA_ROPE:r0 + HEAD_PAD, :] = jnp.zeros((HEAD_PAD - MLA_NOPE - MLA_ROPE, tm), BF16)

    ckv = _rms(z[:, _CKV:_CKV + MLA_KV_RANK], kvn_ref[...])
    ckv_ref[...] = ckv
    ckv_b = ckv.astype(BF16)
    kslab = _rope_apply(z[:, _KR:_KR + LANES], coef_k, half_m)
    kpe_ref[...] = kslab[:, :MLA_ROPE]
    kv = jnp.dot(ckv_b, w_ukv_ref[...], preferred_element_type=F32)
    k_full = kv[:, :MLA_HEADS * HEAD_PAD] + jnp.dot(kslab.astype(BF16), e_ref[...], preferred_element_type=F32)
    km_ref[...] = k_full.astype(BF16)
    vm_ref[...] = kv[:, MLA_HEADS * HEAD_PAD:].astype(BF16)
    vt_ref[0] = _nt_dot(w_uvt_ref[...], ckv_b).astype(BF16)

    dqt = _nt_dot(w_dqt_ref[...], xb) * (DIFF_SCALE * LOG2E)
    cos_dt, sin_dt = cdt_ref[...], sdt_ref[...]
    zeros = jnp.zeros((DIFF_D, tm), BF16)
    for n in range(2 * DIFF_HEADS):
        r0 = n * DIFF_D
        o1, o2 = _rope_rows(dqt[r0:r0 + DIFF_ROT], cos_dt, sin_dt)
        qn = jnp.concatenate([o1, o2, dqt[r0 + DIFF_ROT:r0 + DIFF_D]], axis=0).astype(BF16)
        lo, hi = (qn, zeros) if n % 2 == 0 else (zeros, qn)
        dqt_ref[0, n * LANES:n * LANES + DIFF_D, :] = lo
        dqt_ref[0, n * LANES + DIFF_D:(n + 1) * LANES, :] = hi

    for h in range(DIFF_HEADS):
        sl = slice(h * LANES, (h + 1) * LANES)
        dk = _rope_apply(z[:, _DK + h * LANES:_DK + (h + 1) * LANES], coef_d, half_d)
        dk_ref[:, sl] = dk
        dkb_ref[:, sl] = dk.astype(BF16)
    dv = z[:, _DV:_DV + DIFF_HEADS * DIFF_V]
    dv_ref[...] = dv
    dvb_ref[...] = dv.astype(BF16)
    dvt_ref[0] = dv.T.astype(BF16)


def _proj(x, tabs, tabs_t, w, *, tm):
    t = x.shape[0]
    n_pos_blocks = tabs[0].shape[0] // tm
    row = lambda n: pl.BlockSpec((tm, n), lambda i: (i, 0))
    full = lambda a: pl.BlockSpec(a.shape, lambda i: (0,) * a.ndim)
    tab = pl.BlockSpec((tm, LANES), lambda i: (i % n_pos_blocks, 0))
    tab_t = lambda a: pl.BlockSpec((a.shape[0], tm), lambda i: (0, i % n_pos_blocks))
    tr = lambda n: pl.BlockSpec((1, n, tm), lambda i: (i, 0, 0))
    weights = (w['w_in'], w['w_dqt'], w['q_norm'], w['w_uqt'], w['kv_norm'], w['w_ukv'], w['w_uvt'], w['e_place'])
    outs = ((row, MLA_KV_RANK, F32), (row, MLA_ROPE, F32), (row, 512, F32), (row, 512, F32),
            (tr, 1024, BF16), (row, 1024, BF16), (row, 512, BF16), (tr, 512, BF16),
            (tr, 1024, BF16), (row, 512, BF16), (row, 512, BF16), (tr, 512, BF16))
    shape = lambda kind, n: (t // tm, n, tm) if kind is tr else (t, n)
    return pl.pallas_call(
        _proj_kernel,
        grid=(t // tm,),
        in_specs=[row(D_MODEL)] + [full(a) for a in weights] + [tab] * 4 + [tab_t(a) for a in tabs_t],
        out_specs=[kind(n) for kind, n, _ in outs],
        out_shape=[jax.ShapeDtypeStruct(shape(kind, n), d) for kind, n, d in outs],
        compiler_params=_cparams(("parallel",)),
        name="proj",
    )(x, *weights, *tabs, *tabs_t)


def _expand_kernel(ckv_ref, kpe_ref, w_ukv_ref, e_ref, km_ref, vm_ref):
    kv = jnp.dot(ckv_ref[...].astype(BF16), w_ukv_ref[...], preferred_element_type=F32)
    k_full = kv[:, :MLA_HEADS * HEAD_PAD] + jnp.dot(kpe_ref[...].astype(BF16), e_ref[:MLA_ROPE, :],
                                                   preferred_element_type=F32)
    km_ref[...] = k_full.astype(BF16)
    vm_ref[...] = kv[:, MLA_HEADS * HEAD_PAD:].astype(BF16)


def _expand(ckv, kpe, w, *, tm):
    r = ckv.shape[0]
    row = lambda n: pl.BlockSpec((tm, n), lambda i: (i, 0))
    full = lambda a: pl.BlockSpec(a.shape, lambda i: (0,) * a.ndim)
    return pl.pallas_call(
        _expand_kernel,
        grid=(r // tm,),
        in_specs=[row(MLA_KV_RANK), row(MLA_ROPE), full(w['w_ukv']), full(w['e_place'])],
        out_specs=[row(1024), row(512)],
        out_shape=[jax.ShapeDtypeStruct((r, 1024), BF16), jax.ShapeDtypeStruct((r, 512), BF16)],
        compiler_params=_cparams(("parallel",)),
        name="expand",
    )(ckv, kpe, w['w_ukv'], w['e_place'])


def _chunk_mask_t(t):
    shift = CHUNK.bit_length() - 1
    kc = lax.broadcasted_iota(jnp.int32, (t, t), 0) >> shift
    qc = lax.broadcasted_iota(jnp.int32, (t, t), 1) >> shift
    return kc <= qc


def _online_update_t(carry, s, vt):
    m, l, acc = carry
    m_new = jnp.maximum(m, jnp.max(s, axis=0, keepdims=True))
    alpha = jnp.exp2(m - m_new)
    p = jnp.exp2(s - m_new)
    l = alpha * l + jnp.sum(p, axis=0, keepdims=True)
    acc = alpha * acc + jnp.dot(vt, p.astype(BF16), preferred_element_type=F32)
    return m_new, l, acc


def _flash_init(t, dv, n):
    return tuple((jnp.full((1, t), -jnp.inf, F32), jnp.zeros((1, t), F32), jnp.zeros((dv, t), F32))
                 for _ in range(n))


def _flash_loop(i, scores, values, t, dv, n):
    mask = _chunk_mask_t(t)

    def update(j, s, state, masked):
        if masked:
            s = tuple(jnp.where(mask, sn, -jnp.inf) for sn in s)
        return tuple(_online_update_t(state[k], s[k], values(j, k)) for k in range(n))

    def body(j, carry):
        s, state = carry
        s_next = scores(j + 1)
        return s_next, update(j, s, state, False)

    s, state = lax.fori_loop(0, i, body, (scores(0), _flash_init(t, dv, n)))
    return update(i, s, state, True)


def _mla_flash_kernel(qt_ref, k_ref, vt_ref, o_ref, *, t, heads):
    def scores(j):
        rows = pl.ds(pl.multiple_of(j * t, t), t)
        return tuple(jnp.dot(k_ref[0, rows, h * HEAD_PAD:(h + 1) * HEAD_PAD],
                             qt_ref[0, h * HEAD_PAD:(h + 1) * HEAD_PAD, :], preferred_element_type=F32)
                     for h in range(heads))

    values = lambda j, h: vt_ref[j, h * MLA_V:(h + 1) * MLA_V, :]
    state = _flash_loop(pl.program_id(2), scores, values, t, MLA_V, heads)
    for h in range(0, heads, 2):
        pair = jnp.concatenate([state[h][2] / state[h][1], state[h + 1][2] / state[h + 1][1]], axis=0)
        o_ref[0, :, h * MLA_V:(h + 2) * MLA_V] = pair.T.astype(BF16)


def _mla_flash(qt, km, vt, *, t, heads):
    b, s, _ = km.shape
    nk = s // t
    return pl.pallas_call(
        functools.partial(_mla_flash_kernel, t=t, heads=heads),
        grid=(b, MLA_HEADS // heads, nk),
        in_specs=[pl.BlockSpec((1, heads * HEAD_PAD, t), lambda bi, hi, i: (bi * nk + i, hi, 0)),
                  pl.BlockSpec((1, s, heads * HEAD_PAD), lambda bi, hi, i: (bi, 0, hi)),
                  pl.BlockSpec((nk, heads * MLA_V, t), lambda bi, hi, i: (bi, hi, 0))],
        out_specs=pl.BlockSpec((1, t, heads * MLA_V), lambda bi, hi, i: (bi, i, hi)),
        out_shape=jax.ShapeDtypeStruct((b, s, MLA_HEADS * MLA_V), BF16),
        compiler_params=_cparams(("parallel", "parallel", "arbitrary")),
        name="mla_flash",
    )(qt, km, vt)


def _diff_lambda(lam_ref, lam_init):
    lq1, lk1, lq2, lk2 = (lam_ref[r:r + 1, :] for r in range(4))
    return (jnp.exp(jnp.sum(lq1 * lk1, axis=-1, keepdims=True))
            - jnp.exp(jnp.sum(lq2 * lk2, axis=-1, keepdims=True)) + lam_init)


def _diff_finish(o1, o2, lam, subln, lam_init):
    o = o1 - lam * o2
    return _rms(o, subln) * (1.0 - lam_init)


def _diff_flash_kernel(lam_ref, subln_ref, qt_ref, k_ref, vt_ref, o_ref, *, t, heads, lam_init):
    def scores(j):
        rows = pl.ds(pl.multiple_of(j * t, t), t)
        return tuple(jnp.dot(k_ref[0, rows, (n // 2) * LANES:(n // 2 + 1) * LANES],
                             qt_ref[0, n * LANES:(n + 1) * LANES, :], preferred_element_type=F32)
                     for n in range(2 * heads))

    values = lambda j, n: vt_ref[j, (n // 2) * DIFF_V:(n // 2 + 1) * DIFF_V, :]
    state = _flash_loop(pl.program_id(2), scores, values, t, DIFF_V, 2 * heads)
    lam = _diff_lambda(lam_ref, lam_init)
    for h in range(heads):
        (_, l1, a1), (_, l2, a2) = state[2 * h], state[2 * h + 1]
        o = _diff_finish((a1 / l1).T, (a2 / l2).T, lam, subln_ref[...], lam_init)
        o_ref[0, :, h * DIFF_V:(h + 1) * DIFF_V] = o.astype(BF16)


def _diff_flash(lam_rows, subln, dqt, dkb, dvt, *, t, heads, lam_init):
    b, s, _ = dkb.shape
    nk = s // t
    full = lambda a: pl.BlockSpec(a.shape, lambda bi, hi, i: (0,) * a.ndim)
    return pl.pallas_call(
        functools.partial(_diff_flash_kernel, t=t, heads=heads, lam_init=lam_init),
        grid=(b, DIFF_HEADS // heads, nk),
        in_specs=[full(lam_rows), full(subln),
                  pl.BlockSpec((1, heads * 2 * LANES, t), lambda bi, hi, i: (bi * nk + i, hi, 0)),
                  pl.BlockSpec((1, s, heads * LANES), lambda bi, hi, i: (bi, 0, hi)),
                  pl.BlockSpec((nk, heads * DIFF_V, t), lambda bi, hi, i: (bi, hi, 0))],
        out_specs=pl.BlockSpec((1, t, heads * DIFF_V), lambda bi, hi, i: (bi, i, hi)),
        out_shape=jax.ShapeDtypeStruct((b, s, DIFF_HEADS * DIFF_V), BF16),
        compiler_params=_cparams(("parallel", "parallel", "arbitrary")),
        name="diff_flash",
    )(lam_rows, subln, dqt, dkb, dvt)


def _two_part_softmax_pv(s_past, s_new, v_past, v_new):
    m = jnp.maximum(jnp.max(s_past, axis=-1, keepdims=True), jnp.max(s_new, axis=-1, keepdims=True))
    p_past = jnp.exp2(s_past - m)
    p_new = jnp.exp2(s_new - m)
    l = jnp.sum(p_past, axis=-1, keepdims=True) + jnp.sum(p_new, axis=-1, keepdims=True)
    acc = (jnp.dot(p_past.astype(BF16), v_past, preferred_element_type=F32)
           + jnp.dot(p_new.astype(BF16), v_new, preferred_element_type=F32))
    return acc / l


def _token_major(qt_ref):
    return qt_ref[0].astype(F32).T.astype(BF16)


def _mla_step_kernel(qt_ref, kp_ref, vp_ref, kn_ref, vn_ref, o_ref):
    q_all = _token_major(qt_ref)
    for h in range(MLA_HEADS):
        ks = slice(h * HEAD_PAD, (h + 1) * HEAD_PAD)
        vs = slice(h * MLA_V, (h + 1) * MLA_V)
        q = q_all[:, ks]
        o = _two_part_softmax_pv(_nt_dot(q, kp_ref[0, :, ks]), _nt_dot(q, kn_ref[0, :, ks]),
                                 vp_ref[0, :, vs], vn_ref[0, :, vs])
        o_ref[0, :, vs] = o.astype(BF16)


def _mla_step(qt, km_past, vm_past, km_new, vm_new):
    b, _, n = qt.shape
    blk = lambda a: pl.BlockSpec((1,) + a.shape[1:], lambda bi: (bi, 0, 0))
    args = (qt, km_past, vm_past, km_new, vm_new)
    return pl.pallas_call(
        _mla_step_kernel,
        grid=(b,),
        in_specs=[blk(a) for a in args],
        out_specs=pl.BlockSpec((1, n, MLA_HEADS * MLA_V), lambda bi: (bi, 0, 0)),
        out_shape=jax.ShapeDtypeStruct((b, n, MLA_HEADS * MLA_V), BF16),
        compiler_params=_cparams(("parallel",)),
        name="mla_step",
    )(*args)


def _diff_step_kernel(lam_ref, subln_ref, qt_ref, kp_ref, vp_ref, kn_ref, vn_ref, o_ref, *, lam_init):
    lam = _diff_lambda(lam_ref, lam_init)
    q = _token_major(qt_ref)
    q1 = q[:, :LANES]
    q2 = q[:, LANES:]
    kp = kp_ref[0].astype(BF16)
    vp = vp_ref[0].astype(BF16)
    kn = kn_ref[0]
    vn = vn_ref[0]
    o1 = _two_part_softmax_pv(_nt_dot(q1, kp), _nt_dot(q1, kn), vp, vn)
    o2 = _two_part_softmax_pv(_nt_dot(q2, kp), _nt_dot(q2, kn), vp, vn)
    o_ref[0] = _diff_finish(o1, o2, lam, subln_ref[...], lam_init).astype(BF16)


def _diff_step(lam_rows, subln, dqt, k_past, v_past, dkb, dvb, *, lam_init):
    b, _, n = dqt.shape
    p = k_past.shape[1]
    full = lambda a: pl.BlockSpec(a.shape, lambda bi, hi: (0,) * a.ndim)
    head = lambda rows, width: pl.BlockSpec((1, rows, width), lambda bi, hi: (bi, 0, hi))
    return pl.pallas_call(
        functools.partial(_diff_step_kernel, lam_init=lam_init),
        grid=(b, DIFF_HEADS),
        in_specs=[full(lam_rows), full(subln),
                  pl.BlockSpec((1, 2 * LANES, n), lambda bi, hi: (bi, hi, 0)),
                  head(p, LANES), head(p, LANES), head(n, LANES), head(n, LANES)],
        out_specs=head(n, LANES),
        out_shape=jax.ShapeDtypeStruct((b, n, DIFF_HEADS * DIFF_V), BF16),
        compiler_params=_cparams(("parallel", "parallel")),
        name="diff_step",
    )(lam_rows, subln, dqt, k_past, v_past, dkb, dvb)


def _first_index_of_max(vals, lane):
    m = jnp.max(vals, axis=-1, keepdims=True)
    idx = jnp.min(jnp.where(vals == m, lane, float(LANES)), axis=-1, keepdims=True)
    return m, idx


def _post_attn_kernel(x_ref, mla_ref, diff_ref, wo_a_ref, wo_b_ref, g_ref, b_ref, wr_ref, br_ref,
                      h_ref, gate_ref, *, alpha):
    a = (jnp.dot(mla_ref[...], wo_a_ref[...], preferred_element_type=F32)
         + jnp.dot(diff_ref[...], wo_b_ref[...], preferred_element_type=F32))
    h = _layer_norm(alpha * x_ref[...] + a, g_ref[...], b_ref[...])
    h_ref[...] = h

    logit = jnp.dot(h.astype(BF16), wr_ref[...], preferred_element_type=F32) + br_ref[...]
    lane_i = lax.broadcasted_iota(jnp.int32, logit.shape, 1)
    lane = lane_i.astype(F32)
    group_of_lane = (lane_i >> (EXP_PER_GROUP.bit_length() - 1)).astype(F32)
    neg = -jnp.inf
    g_logit = jnp.where((lane_i >= N_EXPERTS) & (lane_i < N_EXPERTS + N_GROUPS), logit, neg)
    g_max, g_lane = _first_index_of_max(g_logit, lane)
    g_w = 1.0 / jnp.sum(jnp.exp(g_logit - g_max), axis=-1, keepdims=True)
    g_idx = g_lane - float(N_EXPERTS)
    e_logit = jnp.where(group_of_lane == g_idx, logit, neg)
    m1, i1 = _first_index_of_max(e_logit, lane)
    m2, i2 = _first_index_of_max(jnp.where(lane == i1, neg, e_logit), lane)
    r = jnp.exp(m2 - m1)
    p1 = 1.0 / (1.0 + r)
    p2 = r / (1.0 + r)
    gate_ref[...] = g_w * (jnp.where(lane == i1, p1, 0.0) + jnp.where(lane == i2, p2, 0.0))


def _post_attn(x, mla_o, diff_o, w, *, tm, alpha):
    t = x.shape[0]
    row = lambda n: pl.BlockSpec((tm, n), lambda i: (i, 0))
    full = lambda a: pl.BlockSpec(a.shape, lambda i: (0,) * a.ndim)
    weights = (w['w_out_a'], w['w_out_b'], w['ln1_g'], w['ln1_b'], w['w_route'], w['b_route'])
    return pl.pallas_call(
        functools.partial(_post_attn_kernel, alpha=alpha),
        grid=(t // tm,),
        in_specs=[row(D_MODEL), row(512), row(512)] + [full(a) for a in weights],
        out_specs=[row(D_MODEL), row(LANES)],
        out_shape=[jax.ShapeDtypeStruct((t, D_MODEL), F32), jax.ShapeDtypeStruct((t, LANES), F32)],
        compiler_params=_cparams(("parallel",)),
        name="post_attn",
    )(x, mla_o, diff_o, *weights)


def _moe_kernel(h_ref, gate_ref, wg_ref, wu_ref, wd_ref, g_ref, b_ref, y_ref, hb_ref, acc_ref, *, alpha):
    e = pl.program_id(1)

    @pl.when(e == 0)
    def _():
        hb_ref[...] = h_ref[...].astype(BF16)
        acc_ref[...] = jnp.zeros_like(acc_ref)

    hb = hb_ref[...]
    lane = lax.broadcasted_iota(jnp.int32, gate_ref.shape, 1)
    gate_e = jnp.sum(jnp.where(lane == e, gate_ref[...], 0.0), axis=-1, keepdims=True)
    hid = (jax.nn.silu(jnp.dot(hb, wg_ref[0], preferred_element_type=F32))
           * jnp.dot(hb, wu_ref[0], preferred_element_type=F32))
    acc_ref[...] += jnp.dot((hid * gate_e).astype(BF16), wd_ref[0], preferred_element_type=F32)

    @pl.when(e == N_EXPERTS - 1)
    def _():
        y_ref[...] = _layer_norm(alpha * h_ref[...] + acc_ref[...], g_ref[...], b_ref[...])


def _moe(h, gate, w, *, tm, alpha):
    t = h.shape[0]
    row = lambda n: pl.BlockSpec((tm, n), lambda i, e: (i, 0))
    full = lambda a: pl.BlockSpec(a.shape, lambda i, e: (0,) * a.ndim)
    exp = lambda a: pl.BlockSpec((1,) + a.shape[1:], lambda i, e: (e, 0, 0))
    return pl.pallas_call(
        functools.partial(_moe_kernel, alpha=alpha),
        grid=(t // tm, N_EXPERTS),
        in_specs=[row(D_MODEL), row(LANES), exp(w['w_gate']), exp(w['w_up']), exp(w['w_down']),
                  full(w['ln2_g']), full(w['ln2_b'])],
        out_specs=row(D_MODEL),
        out_shape=jax.ShapeDtypeStruct((t, D_MODEL), F32),
        scratch_shapes=[pltpu.VMEM((tm, D_MODEL), BF16), pltpu.VMEM((tm, D_MODEL), F32)],
        compiler_params=_cparams(("parallel", "arbitrary")),
        name="moe",
    )(h, gate, w['w_gate'], w['w_up'], w['w_down'], w['ln2_g'], w['ln2_b'])


def _rope_tables(pos):
    pos = pos.astype(F32)[:, None]
    lane = jnp.arange(LANES)
    inv_m = ROPE_THETA ** (-jnp.arange(0, MLA_ROPE, 2, dtype=F32) / MLA_ROPE)
    ang_m = pos * inv_m
    inv_d = ROPE_THETA ** (-jnp.arange(0, DIFF_ROT, 2, dtype=F32) / DIFF_ROT)
    ang_d = pos * inv_d
    wide_m = ang_m[:, lane % (MLA_ROPE // 2)]
    wide_d = ang_d[:, lane % (DIFF_ROT // 2)]
    tabs = (jnp.cos(wide_m), jnp.sin(wide_m), jnp.cos(wide_d), jnp.sin(wide_d))
    tabs_t = (jnp.cos(ang_m).T, jnp.sin(ang_m).T, jnp.cos(ang_d).T, jnp.sin(ang_d).T)
    return tabs, tabs_t


def _prep_weights(l, w_in, mla_q_norm, mla_w_uq, mla_kv_norm, mla_w_ukv, diff_subln, w_out, ln1_g, ln1_b,
                  ln2_g, ln2_b, w_route_group, b_route_group, w_route_expert, b_route_expert,
                  w_exp_gate, w_exp_up, w_exp_down, lam_rows):
    wi = w_in[l]
    s0, s1, s2, s3, s4 = 256, 512, 544, 1056, 1568
    w_in_r = jnp.concatenate([wi[:, :s0], wi[:, s0:s1], wi[:, s3:s4], wi[:, s4:],
                              wi[:, s1:s2], jnp.zeros((D_MODEL, LANES - MLA_ROPE), F32)], axis=1)
    w_uq = jnp.pad(mla_w_uq[l].reshape(MLA_Q_RANK, MLA_HEADS, MLA_NOPE + MLA_ROPE),
                   ((0, 0), (0, 0), (0, HEAD_PAD - MLA_NOPE - MLA_ROPE))).reshape(MLA_Q_RANK, -1)
    ukv = mla_w_ukv[l].reshape(MLA_KV_RANK, MLA_HEADS, MLA_NOPE + MLA_V)
    w_uk = jnp.pad(ukv[..., :MLA_NOPE], ((0, 0), (0, 0), (0, HEAD_PAD - MLA_NOPE))).reshape(MLA_KV_RANK, -1)
    w_uv = ukv[..., MLA_NOPE:].reshape(MLA_KV_RANK, -1)
    rows = jnp.arange(LANES)[:, None]
    cols = jnp.arange(MLA_HEADS * HEAD_PAD)[None, :]
    e_place = ((rows < MLA_ROPE) & (cols % HEAD_PAD == rows + MLA_NOPE)).astype(BF16)
    w_route = jnp.concatenate([w_route_expert[l], w_route_group[l],
                               jnp.zeros((D_MODEL, LANES - N_EXPERTS - N_GROUPS), F32)], axis=1)
    b_route = jnp.concatenate([b_route_expert[l], b_route_group[l],
                               jnp.zeros((LANES - N_EXPERTS - N_GROUPS,), F32)])[None, :]
    return dict(
        w_in=w_in_r.astype(BF16), w_dqt=wi[:, s2:s3].T.astype(BF16),
        q_norm=mla_q_norm[l][None, :], w_uqt=w_uq.T.astype(BF16),
        kv_norm=mla_kv_norm[l][None, :], w_ukv=jnp.concatenate([w_uk, w_uv], axis=1).astype(BF16),
        w_uvt=w_uv.T.astype(BF16),
        e_place=e_place, subln=diff_subln[l][None, :], lam_rows=lam_rows,
        w_out_a=w_out[l][:MLA_HEADS * MLA_V].astype(BF16), w_out_b=w_out[l][MLA_HEADS * MLA_V:].astype(BF16),
        ln1_g=ln1_g[l][None, :], ln1_b=ln1_b[l][None, :], ln2_g=ln2_g[l][None, :], ln2_b=ln2_b[l][None, :],
        w_route=w_route.astype(BF16), b_route=b_route,
        w_gate=w_exp_gate[l].astype(BF16), w_up=w_exp_up[l].astype(BF16), w_down=w_exp_down[l].astype(BF16))


def _block_output(x, mla_o, diff_o, w, alpha):
    t = x.shape[0]
    h, gate = _post_attn(x, mla_o.reshape(t, -1), diff_o.reshape(t, -1), w, tm=min(512, t), alpha=alpha)
    return _moe(h, gate, w, tm=min(1024, t), alpha=alpha)


def kernel(x_prompt, x_sample, cache_mla_ckv, cache_mla_kpe, cache_diff_k, cache_diff_v, w_in, mla_q_norm, mla_w_uq, mla_kv_norm, mla_w_ukv, diff_lambda_q1, diff_lambda_k1, diff_lambda_q2, diff_lambda_k2, diff_subln, w_out, ln1_g, ln1_b, ln2_g, ln2_b, w_route_group, b_route_group, w_route_expert, b_route_expert, w_exp_gate, w_exp_up, w_exp_down):
    depth = w_in.shape[0]
    bp, sp, _ = x_prompt.shape
    bs, ss, _ = x_sample.shape
    past = cache_mla_ckv.shape[2]
    alpha = (2.0 * depth) ** 0.25
    tabs_p, tabs_pt = _rope_tables(jnp.arange(sp))
    tabs_s, tabs_st = _rope_tables(past + jnp.arange(ss))
    hp = x_prompt.reshape(bp * sp, D_MODEL)
    hs = x_sample.reshape(bs * ss, D_MODEL)
    outs = [[] for _ in range(8)]
    for l in range(depth):
        lam_init = 0.8 - 0.6 * math.exp(-0.3 * l)
        lam_rows = jnp.stack([diff_lambda_q1[l], diff_lambda_k1[l], diff_lambda_q2[l], diff_lambda_k2[l]])
        w = _prep_weights(l, w_in, mla_q_norm, mla_w_uq, mla_kv_norm, mla_w_ukv, diff_subln, w_out,
                          ln1_g, ln1_b, ln2_g, ln2_b, w_route_group, b_route_group, w_route_expert,
                          b_route_expert, w_exp_gate, w_exp_up, w_exp_down, lam_rows)

        ckv, kpe, dk, dv, qt, km, _, vt, dqt, dkb, _, dvt = _proj(hp, tabs_p, tabs_pt, w, tm=ATTN_TILE)
        b3 = lambda a: a.reshape(bp, sp, -1)
        mla_o = _mla_flash(qt, b3(km), vt, t=ATTN_TILE, heads=4)
        diff_o = _diff_flash(w['lam_rows'], w['subln'], dqt, b3(dkb), dvt, t=ATTN_TILE, heads=2,
                             lam_init=lam_init)
        outs[0].append(ckv.reshape(bp, sp, MLA_KV_RANK))
        outs[1].append(kpe.reshape(bp, sp, MLA_ROPE))
        outs[2].append(dk.reshape(bp, sp, DIFF_HEADS, 2 * DIFF_D))
        outs[3].append(dv.reshape(bp, sp, DIFF_HEADS, DIFF_V))
        hp = _block_output(hp, mla_o, diff_o, w, alpha)

        ckv_s, kpe_s, dk_s, dv_s, qt_s, km_s, vm_s, _, dqt_s, dkb_s, dvb_s, _ = _proj(hs, tabs_s, tabs_st, w, tm=ss)
        km_past, vm_past = _expand(cache_mla_ckv[l].reshape(bs * past, MLA_KV_RANK),
                                   cache_mla_kpe[l].reshape(bs * past, MLA_ROPE), w, tm=512)
        s3 = lambda a: a.reshape(bs, ss, -1)
        p3 = lambda a: a.reshape(bs, past, -1)
        mla_o_s = _mla_step(qt_s, p3(km_past), p3(vm_past), s3(km_s), s3(vm_s))
        diff_o_s = _diff_step(w['lam_rows'], w['subln'], dqt_s, p3(cache_diff_k[l]), p3(cache_diff_v[l]),
                              s3(dkb_s), s3(dvb_s), lam_init=lam_init)
        outs[4].append(ckv_s.reshape(bs, ss, MLA_KV_RANK))
        outs[5].append(kpe_s.reshape(bs, ss, MLA_ROPE))
        outs[6].append(dk_s.reshape(bs, ss, DIFF_HEADS, 2 * DIFF_D))
        outs[7].append(dv_s.reshape(bs, ss, DIFF_HEADS, DIFF_V))
        hs = _block_output(hs, mla_o_s, diff_o_s, w, alpha)

    return (hp.reshape(bp, sp, D_MODEL), hs.reshape(bs, ss, D_MODEL)) + tuple(jnp.stack(o) for o in outs)
```

```python
import functools
import math

import jax
import jax.numpy as jnp
from jax import lax
from jax.experimental import pallas as pl
from jax.experimental.pallas import tpu as pltpu

F32 = jnp.float32
BF16 = jnp.bfloat16

D_MODEL = 1024
CHUNK = 64
ROPE_THETA = 500000.0
MLA_HEADS = 8
MLA_NOPE = 64
MLA_ROPE = 32
MLA_V = 64
MLA_Q_RANK = 256
MLA_KV_RANK = 256
MLA_SCALE = (MLA_NOPE + MLA_ROPE) ** -0.5
DIFF_HEADS = 4
DIFF_D = 64
DIFF_V = 2 * DIFF_D
DIFF_ROT = DIFF_D // 4
DIFF_SCALE = DIFF_D ** -0.5
N_GROUPS = 4
EXP_PER_GROUP = 8
N_EXPERTS = N_GROUPS * EXP_PER_GROUP
D_EXPERT = 256
EPS_LN = 1e-5
EPS_RMS = 1e-6
LOG2E = math.log2(math.e)

LANES = 128
HEAD_PAD = LANES
ATTN_TILE = 256
VMEM_LIMIT = 56 * 1024 * 1024

_CQ, _CKV, _DK, _DV, _KR, _IN_COLS_R = 0, 256, 512, 1024, 1536, 1664


def _cparams(sem):
    return pltpu.CompilerParams(dimension_semantics=sem, vmem_limit_bytes=VMEM_LIMIT)


def _rms(x, g):
    return x * lax.rsqrt(jnp.mean(x * x, axis=-1, keepdims=True) + EPS_RMS) * g


def _layer_norm(x, g, b):
    mu = jnp.mean(x, axis=-1, keepdims=True)
    xc = x - mu
    var = jnp.mean(xc * xc, axis=-1, keepdims=True)
    return xc * lax.rsqrt(var + EPS_LN) * g + b


def _nt_dot(a, b):
    return lax.dot_general(a, b, (((1,), (1,)), ((), ())), preferred_element_type=F32)


def _rope_coeffs(cos_t, sin_t, lo, half, period):
    lane = lax.broadcasted_iota(jnp.int32, cos_t.shape, 1) & (period - 1)
    is1 = (lane >= lo) & (lane < lo + half)
    is2 = (lane >= lo + half) & (lane < lo + 2 * half)
    c = jnp.where(is1 | is2, cos_t, 1.0)
    a = jnp.where(is1, -sin_t, 0.0)
    b = jnp.where(is2, sin_t, 0.0)
    return c, a, b


def _rope_apply(x, coeffs, half):
    c, a, b = coeffs
    return x * c + pltpu.roll(x, LANES - half, 1) * a + pltpu.roll(x, half, 1) * b


def _rope_rows(x, cos_t, sin_t):
    half = cos_t.shape[0]
    x1, x2 = x[:half], x[half:]
    return x1 * cos_t - x2 * sin_t, x1 * sin_t + x2 * cos_t


def _proj_kernel(x_ref, w_in_ref, w_dqt_ref, qn_ref, w_uqt_ref, kvn_ref, w_ukv_ref, w_uvt_ref, e_ref,
                 cm_ref, sm_ref, cd_ref, sd_ref, cmt_ref, smt_ref, cdt_ref, sdt_ref,
                 ckv_ref, kpe_ref, dk_ref, dv_ref, qt_ref, km_ref, vm_ref, vt_ref,
                 dqt_ref, dkb_ref, dvb_ref, dvt_ref):
    xb = x_ref[...].astype(BF16)
    tm = xb.shape[0]
    z = jnp.dot(xb, w_in_ref[...], preferred_element_type=F32)
    half_m = MLA_ROPE // 2
    half_d = DIFF_ROT // 2
    coef_k = _rope_coeffs(cm_ref[...], sm_ref[...], 0, half_m, LANES)
    coef_d = _rope_coeffs(cd_ref[...], sd_ref[...], 0, half_d, DIFF_D)

    cq = _rms(z[:, _CQ:_CQ + MLA_Q_RANK], qn_ref[...])
    qt = _nt_dot(w_uqt_ref[...], cq.astype(BF16)) * (MLA_SCALE * LOG2E)
    cos_mt, sin_mt = cmt_ref[...], smt_ref[...]
    for h in range(MLA_HEADS):
        r0 = h * HEAD_PAD
        qt_ref[0, r0:r0 + MLA_NOPE, :] = qt[r0:r0 + MLA_NOPE].astype(BF16)
        o1, o2 = _rope_rows(qt[r0 + MLA_NOPE:r0 + MLA_NOPE + MLA_ROPE], cos_mt, sin_mt)
        qt_ref[0, r0 + MLA_NOPE:r0 + MLA_NOPE + half_m, :] = o1.astype(BF16)
        qt_ref[0, r0 + MLA_NOPE + half_m:r0 + MLA_NOPE + MLA_ROPE, :] = o2.astype(BF16)
        qt_ref[0, r0 + MLA_NOPE + MLA_ROPE:r0 + HEAD_PAD, :] = jnp.zeros((HEAD_PAD - MLA_NOPE - MLA_ROPE, tm), BF16)

    ckv = _rms(z[:, _CKV:_CKV + MLA_KV_RANK], kvn_ref[...])
    ckv_ref[...] = ckv
    ckv_b = ckv.astype(BF16)
    kslab = _rope_apply(z[:, _KR:_KR + LANES], coef_k, half_m)
    kpe_ref[...] = kslab[:, :MLA_ROPE]
    kv = jnp.dot(ckv_b, w_ukv_ref[...], preferred_element_type=F32)
    k_full = kv[:, :MLA_HEADS * HEAD_PAD] + jnp.dot(kslab.astype(BF16), e_ref[...], preferred_element_type=F32)
    km_ref[...] = k_full.astype(BF16)
    vm_ref[...] = kv[:, MLA_HEADS * HEAD_PAD:].astype(BF16)
    vt_ref[0] = _nt_dot(w_uvt_ref[...], ckv_b).astype(BF16)

    dqt = _nt_dot(w_dqt_ref[...], xb) * (DIFF_SCALE * LOG2E)
    cos_dt, sin_dt = cdt_ref[...], sdt_ref[...]
    zeros = jnp.zeros((DIFF_D, tm), BF16)
    for n in range(2 * DIFF_HEADS):
        r0 = n * DIFF_D
        o1, o2 = _rope_rows(dqt[r0:r0 + DIFF_ROT], cos_dt, sin_dt)
        qn = jnp.concatenate([o1, o2, dqt[r0 + DIFF_ROT:r0 + DIFF_D]], axis=0).astype(BF16)
        lo, hi = (qn, zeros) if n % 2 == 0 else (zeros, qn)
        dqt_ref[0, n * LANES:n * LANES + DIFF_D, :] = lo
        dqt_ref[0, n * LANES + DIFF_D:(n + 1) * LANES, :] = hi

    for h in range(DIFF_HEADS):
        sl = slice(h * LANES, (h + 1) * LANES)
        dk = _rope_apply(z[:, _DK + h * LANES:_DK + (h + 1) * LANES], coef_d, half_d)
        dk_ref[:, sl] = dk
        dkb_ref[:, sl] = dk.astype(BF16)
    dv = z[:, _DV:_DV + DIFF_HEADS * DIFF_V]
    dv_ref[...] = dv
    dvb_ref[...] = dv.astype(BF16)
    dvt_ref[0] = dv.T.astype(BF16)


def _proj(x, tabs, tabs_t, w, *, tm):
    t = x.shape[0]
    n_pos_blocks = tabs[0].shape[0] // tm
    row = lambda n: pl.BlockSpec((tm, n), lambda i: (i, 0))
    full = lambda a: pl.BlockSpec(a.shape, lambda i: (0,) * a.ndim)
    tab = pl.BlockSpec((tm, LANES), lambda i: (i % n_pos_blocks, 0))
    tab_t = lambda a: pl.BlockSpec((a.shape[0], tm), lambda i: (0, i % n_pos_blocks))
    tr = lambda n: pl.BlockSpec((1, n, tm), lambda i: (i, 0, 0))
    weights = (w['w_in'], w['w_dqt'], w['q_norm'], w['w_uqt'], w['kv_norm'], w['w_ukv'], w['w_uvt'], w['e_place'])
    outs = ((row, MLA_KV_RANK, F32), (row, MLA_ROPE, F32), (row, 512, F32), (row, 512, F32),
            (tr, 1024, BF16), (row, 1024, BF16), (row, 512, BF16), (tr, 512, BF16),
            (tr, 1024, BF16), (row, 512, BF16), (row, 512, BF16), (tr, 512, BF16))
    shape = lambda kind, n: (t // tm, n, tm) if kind is tr else (t, n)
    return pl.pallas_call(
        _proj_kernel,
        grid=(t // tm,),
        in_specs=[row(D_MODEL)] + [full(a) for a in weights] + [tab] * 4 + [tab_t(a) for a in tabs_t],
        out_specs=[kind(n) for kind, n, _ in outs],
        out_shape=[jax.ShapeDtypeStruct(shape(kind, n), d) for kind, n, d in outs],
        compiler_params=_cparams(("parallel",)),
        name="proj",
    )(x, *weights, *tabs, *tabs_t)


def _expand_kernel(ckv_ref, kpe_ref, w_ukv_ref, e_ref, km_ref, vm_ref):
    kv = jnp.dot(ckv_ref[...].astype(BF16), w_ukv_ref[...], preferred_element_type=F32)
    k_full = kv[:, :MLA_HEADS * HEAD_PAD] + jnp.dot(kpe_ref[...].astype(BF16), e_ref[:MLA_ROPE, :],
                                                   preferred_element_type=F32)
    km_ref[...] = k_full.astype(BF16)
    vm_ref[...] = kv[:, MLA_HEADS * HEAD_PAD:].astype(BF16)


def _expand(ckv, kpe, w, *, tm):
    r = ckv.shape[0]
    row = lambda n: pl.BlockSpec((tm, n), lambda i: (i, 0))
    full = lambda a: pl.BlockSpec(a.shape, lambda i: (0,) * a.ndim)
    return pl.pallas_call(
        _expand_kernel,
        grid=(r // tm,),
        in_specs=[row(MLA_KV_RANK), row(MLA_ROPE), full(w['w_ukv']), full(w['e_place'])],
        out_specs=[row(1024), row(512)],
        out_shape=[jax.ShapeDtypeStruct((r, 1024), BF16), jax.ShapeDtypeStruct((r, 512), BF16)],
        compiler_params=_cparams(("parallel",)),
        name="expand",
    )(ckv, kpe, w['w_ukv'], w['e_place'])


def _chunk_mask_t(t):
    shift = CHUNK.bit_length() - 1
    kc = lax.broadcasted_iota(jnp.int32, (t, t), 0) >> shift
    qc = lax.broadcasted_iota(jnp.int32, (t, t), 1) >> shift
    return kc <= qc


def _flash_scratch(t, dv, n):
    return [pltpu.VMEM((n, t, t), F32), pltpu.VMEM((n, t, t), F32),
            pltpu.VMEM((n, 1, t), F32), pltpu.VMEM((n, 1, t), F32), pltpu.VMEM((n, dv, t), F32)]


def _flash_loop(i, score, value, n, s_a, s_b, m_ref, l_ref, acc_ref):
    t = s_a.shape[-1]
    mask = _chunk_mask_t(t)
    m_ref[...] = jnp.full(m_ref.shape, -jnp.inf, F32)
    l_ref[...] = jnp.zeros(l_ref.shape, F32)
    acc_ref[...] = jnp.zeros(acc_ref.shape, F32)

    def step(j, s_cur, s_nxt, masked):
        if s_nxt is not None:
            for k in range(n):
                s_nxt[k] = score(j + 1, k)
        for k in range(n):
            s = s_cur[k]
            if masked:
                s = jnp.where(mask, s, -jnp.inf)
            m = m_ref[k]
            m_new = jnp.maximum(m, jnp.max(s, axis=0, keepdims=True))
            alpha = jnp.exp2(m - m_new)
            p = jnp.exp2(s - m_new)
            m_ref[k] = m_new
            l_ref[k] = alpha * l_ref[k] + jnp.sum(p, axis=0, keepdims=True)
            acc_ref[k] = alpha * acc_ref[k] + jnp.dot(value(j, k), p.astype(BF16), preferred_element_type=F32)

    for k in range(n):
        s_a[k] = score(0, k)

    def pair(jj, carry):
        step(2 * jj, s_a, s_b, False)
        step(2 * jj + 1, s_b, s_a, False)
        return carry

    lax.fori_loop(0, i // 2, pair, 0)

    @pl.when(i % 2 == 0)
    def _():
        step(i, s_a, None, True)

    @pl.when(i % 2 == 1)
    def _():
        step(i - 1, s_a, s_b, False)
        step(i, s_b, None, True)


def _mla_flash_kernel(qt_ref, k_ref, vt_ref, o_ref, s_a, s_b, m_ref, l_ref, acc_ref, *, t, heads):
    def score(j, h):
        rows = pl.ds(pl.multiple_of(j * t, t), t)
        return jnp.dot(k_ref[0, rows, h * HEAD_PAD:(h + 1) * HEAD_PAD],
                       qt_ref[0, h * HEAD_PAD:(h + 1) * HEAD_PAD, :], preferred_element_type=F32)

    value = lambda j, h: vt_ref[j, h * MLA_V:(h + 1) * MLA_V, :]
    _flash_loop(pl.program_id(2), score, value, heads, s_a, s_b, m_ref, l_ref, acc_ref)
    for h in range(0, heads, 2):
        pair = jnp.concatenate([acc_ref[h] / l_ref[h], acc_ref[h + 1] / l_ref[h + 1]], axis=0)
        o_ref[0, :, h * MLA_V:(h + 2) * MLA_V] = pair.T.astype(BF16)


def _mla_flash(qt, km, vt, *, t, heads):
    b, s, _ = km.shape
    nk = s // t
    return pl.pallas_call(
        functools.partial(_mla_flash_kernel, t=t, heads=heads),
        grid=(b, MLA_HEADS // heads, nk),
        in_specs=[pl.BlockSpec((1, heads * HEAD_PAD, t), lambda bi, hi, i: (bi * nk + i, hi, 0)),
                  pl.BlockSpec((1, s, heads * HEAD_PAD), lambda bi, hi, i: (bi, 0, hi)),
                  pl.BlockSpec((nk, heads * MLA_V, t), lambda bi, hi, i: (bi, hi, 0))],
        out_specs=pl.BlockSpec((1, t, heads * MLA_V), lambda bi, hi, i: (bi, i, hi)),
        out_shape=jax.ShapeDtypeStruct((b, s, MLA_HEADS * MLA_V), BF16),
        scratch_shapes=_flash_scratch(t, MLA_V, heads),
        compiler_params=_cparams(("parallel", "parallel", "arbitrary")),
        name="mla_flash",
    )(qt, km, vt)


def _diff_lambda(lam_ref, lam_init):
    lq1, lk1, lq2, lk2 = (lam_ref[r:r + 1, :] for r in range(4))
    return (jnp.exp(jnp.sum(lq1 * lk1, axis=-1, keepdims=True))
            - jnp.exp(jnp.sum(lq2 * lk2, axis=-1, keepdims=True)) + lam_init)


def _diff_finish(o1, o2, lam, subln, lam_init):
    o = o1 - lam * o2
    return _rms(o, subln) * (1.0 - lam_init)


def _diff_flash_kernel(lam_ref, subln_ref, qt_ref, k_ref, vt_ref, o_ref, s_a, s_b, m_ref, l_ref, acc_ref,
                       *, t, heads, lam_init):
    def score(j, n):
        rows = pl.ds(pl.multiple_of(j * t, t), t)
        return jnp.dot(k_ref[0, rows, (n // 2) * LANES:(n // 2 + 1) * LANES],
                       qt_ref[0, n * LANES:(n + 1) * LANES, :], preferred_element_type=F32)

    value = lambda j, n: vt_ref[j, (n // 2) * DIFF_V:(n // 2 + 1) * DIFF_V, :]
    _flash_loop(pl.program_id(2), score, value, 2 * heads, s_a, s_b, m_ref, l_ref, acc_ref)
    lam = _diff_lambda(lam_ref, lam_init)
    for h in range(heads):
        o1 = (acc_ref[2 * h] / l_ref[2 * h]).T
        o2 = (acc_ref[2 * h + 1] / l_ref[2 * h + 1]).T
        o = _diff_finish(o1, o2, lam, subln_ref[...], lam_init)
        o_ref[0, :, h * DIFF_V:(h + 1) * DIFF_V] = o.astype(BF16)


def _diff_flash(lam_rows, subln, dqt, dkb, dvt, *, t, heads, lam_init):
    b, s, _ = dkb.shape
    nk = s // t
    full = lambda a: pl.BlockSpec(a.shape, lambda bi, hi, i: (0,) * a.ndim)
    return pl.pallas_call(
        functools.partial(_diff_flash_kernel, t=t, heads=heads, lam_init=lam_init),
        grid=(b, DIFF_HEADS // heads, nk),
        in_specs=[full(lam_rows), full(subln),
                  pl.BlockSpec((1, heads * 2 * LANES, t), lambda bi, hi, i: (bi * nk + i, hi, 0)),
                  pl.BlockSpec((1, s, heads * LANES), lambda bi, hi, i: (bi, 0, hi)),
                  pl.BlockSpec((nk, heads * DIFF_V, t), lambda bi, hi, i: (bi, hi, 0))],
        out_specs=pl.BlockSpec((1, t, heads * DIFF_V), lambda bi, hi, i: (bi, i, hi)),
        out_shape=jax.ShapeDtypeStruct((b, s, DIFF_HEADS * DIFF_V), BF16),
        scratch_shapes=_flash_scratch(t, DIFF_V, 2 * heads),
        compiler_params=_cparams(("parallel", "parallel", "arbitrary")),
        name="diff_flash",
    )(lam_rows, subln, dqt, dkb, dvt)


def _two_part_softmax_pv(s_past, s_new, v_past, v_new):
    m = jnp.maximum(jnp.max(s_past, axis=-1, keepdims=True), jnp.max(s_new, axis=-1, keepdims=True))
    p_past = jnp.exp2(s_past - m)
    p_new = jnp.exp2(s_new - m)
    l = jnp.sum(p_past, axis=-1, keepdims=True) + jnp.sum(p_new, axis=-1, keepdims=True)
    acc = (jnp.dot(p_past.astype(BF16), v_past, preferred_element_type=F32)
           + jnp.dot(p_new.astype(BF16), v_new, preferred_element_type=F32))
    return acc / l


def _token_major(qt_ref):
    return qt_ref[0].astype(F32).T.astype(BF16)


def _mla_step_kernel(qt_ref, kp_ref, vp_ref, kn_ref, vn_ref, o_ref):
    q_all = _token_major(qt_ref)
    for h in range(MLA_HEADS):
        ks = slice(h * HEAD_PAD, (h + 1) * HEAD_PAD)
        vs = slice(h * MLA_V, (h + 1) * MLA_V)
        q = q_all[:, ks]
        o = _two_part_softmax_pv(_nt_dot(q, kp_ref[0, :, ks]), _nt_dot(q, kn_ref[0, :, ks]),
                                 vp_ref[0, :, vs], vn_ref[0, :, vs])
        o_ref[0, :, vs] = o.astype(BF16)


def _mla_step(qt, km_past, vm_past, km_new, vm_new):
    b, _, n = qt.shape
    blk = lambda a: pl.BlockSpec((1,) + a.shape[1:], lambda bi: (bi, 0, 0))
    args = (qt, km_past, vm_past, km_new, vm_new)
    return pl.pallas_call(
        _mla_step_kernel,
        grid=(b,),
        in_specs=[blk(a) for a in args],
        out_specs=pl.BlockSpec((1, n, MLA_HEADS * MLA_V), lambda bi: (bi, 0, 0)),
        out_shape=jax.ShapeDtypeStruct((b, n, MLA_HEADS * MLA_V), BF16),
        compiler_params=_cparams(("parallel",)),
        name="mla_step",
    )(*args)


def _diff_step_kernel(lam_ref, subln_ref, qt_ref, kp_ref, vp_ref, kn_ref, vn_ref, o_ref, *, lam_init):
    lam = _diff_lambda(lam_ref, lam_init)
    q = _token_major(qt_ref)
    q1 = q[:, :LANES]
    q2 = q[:, LANES:]
    kp = kp_ref[0].astype(BF16)
    vp = vp_ref[0].astype(BF16)
    kn = kn_ref[0]
    vn = vn_ref[0]
    o1 = _two_part_softmax_pv(_nt_dot(q1, kp), _nt_dot(q1, kn), vp, vn)
    o2 = _two_part_softmax_pv(_nt_dot(q2, kp), _nt_dot(q2, kn), vp, vn)
    o_ref[0] = _diff_finish(o1, o2, lam, subln_ref[...], lam_init).astype(BF16)


def _diff_step(lam_rows, subln, dqt, k_past, v_past, dkb, dvb, *, lam_init):
    b, _, n = dqt.shape
    p = k_past.shape[1]
    full = lambda a: pl.BlockSpec(a.shape, lambda bi, hi: (0,) * a.ndim)
    head = lambda rows, width: pl.BlockSpec((1, rows, width), lambda bi, hi: (bi, 0, hi))
    return pl.pallas_call(
        functools.partial(_diff_step_kernel, lam_init=lam_init),
        grid=(b, DIFF_HEADS),
        in_specs=[full(lam_rows), full(subln),
                  pl.BlockSpec((1, 2 * LANES, n), lambda bi, hi: (bi, hi, 0)),
                  head(p, LANES), head(p, LANES), head(n, LANES), head(n, LANES)],
        out_specs=head(n, LANES),
        out_shape=jax.ShapeDtypeStruct((b, n, DIFF_HEADS * DIFF_V), BF16),
        compiler_params=_cparams(("parallel", "parallel")),
        name="diff_step",
    )(lam_rows, subln, dqt, k_past, v_past, dkb, dvb)


def _first_index_of_max(vals, lane):
    m = jnp.max(vals, axis=-1, keepdims=True)
    idx = jnp.min(jnp.where(vals == m, lane, float(LANES)), axis=-1, keepdims=True)
    return m, idx


def _post_attn_kernel(x_ref, mla_ref, diff_ref, wo_a_ref, wo_b_ref, g_ref, b_ref, wr_ref, br_ref,
                      h_ref, gate_ref, *, alpha):
    a = (jnp.dot(mla_ref[...], wo_a_ref[...], preferred_element_type=F32)
         + jnp.dot(diff_ref[...], wo_b_ref[...], preferred_element_type=F32))
    h = _layer_norm(alpha * x_ref[...] + a, g_ref[...], b_ref[...])
    h_ref[...] = h

    logit = jnp.dot(h.astype(BF16), wr_ref[...], preferred_element_type=F32) + br_ref[...]
    lane_i = lax.broadcasted_iota(jnp.int32, logit.shape, 1)
    lane = lane_i.astype(F32)
    group_of_lane = (lane_i >> (EXP_PER_GROUP.bit_length() - 1)).astype(F32)
    neg = -jnp.inf
    g_logit = jnp.where((lane_i >= N_EXPERTS) & (lane_i < N_EXPERTS + N_GROUPS), logit, neg)
    g_max, g_lane = _first_index_of_max(g_logit, lane)
    g_w = 1.0 / jnp.sum(jnp.exp(g_logit - g_max), axis=-1, keepdims=True)
    g_idx = g_lane - float(N_EXPERTS)
    e_logit = jnp.where(group_of_lane == g_idx, logit, neg)
    m1, i1 = _first_index_of_max(e_logit, lane)
    m2, i2 = _first_index_of_max(jnp.where(lane == i1, neg, e_logit), lane)
    r = jnp.exp(m2 - m1)
    p1 = 1.0 / (1.0 + r)
    p2 = r / (1.0 + r)
    gate_ref[...] = g_w * (jnp.where(lane == i1, p1, 0.0) + jnp.where(lane == i2, p2, 0.0))


def _post_attn(x, mla_o, diff_o, w, *, tm, alpha):
    t = x.shape[0]
    row = lambda n: pl.BlockSpec((tm, n), lambda i: (i, 0))
    full = lambda a: pl.BlockSpec(a.shape, lambda i: (0,) * a.ndim)
    weights = (w['w_out_a'], w['w_out_b'], w['ln1_g'], w['ln1_b'], w['w_route'], w['b_route'])
    return pl.pallas_call(
        functools.partial(_post_attn_kernel, alpha=alpha),
        grid=(t // tm,),
        in_specs=[row(D_MODEL), row(512), row(512)] + [full(a) for a in weights],
        out_specs=[row(D_MODEL), row(LANES)],
        out_shape=[jax.ShapeDtypeStruct((t, D_MODEL), F32), jax.ShapeDtypeStruct((t, LANES), F32)],
        compiler_params=_cparams(("parallel",)),
        name="post_attn",
    )(x, mla_o, diff_o, *weights)


def _moe_kernel(h_ref, gate_ref, wg_ref, wu_ref, wd_ref, g_ref, b_ref, y_ref, hb_ref, acc_ref, *, alpha):
    e = pl.program_id(1)

    @pl.when(e == 0)
    def _():
        hb_ref[...] = h_ref[...].astype(BF16)
        acc_ref[...] = jnp.zeros_like(acc_ref)

    hb = hb_ref[...]
    lane = lax.broadcasted_iota(jnp.int32, gate_ref.shape, 1)
    gate_e = jnp.sum(jnp.where(lane == e, gate_ref[...], 0.0), axis=-1, keepdims=True)
    hid = (jax.nn.silu(jnp.dot(hb, wg_ref[0], preferred_element_type=F32))
           * jnp.dot(hb, wu_ref[0], preferred_element_type=F32))
    acc_ref[...] += jnp.dot((hid * gate_e).astype(BF16), wd_ref[0], preferred_element_type=F32)

    @pl.when(e == N_EXPERTS - 1)
    def _():
        y_ref[...] = _layer_norm(alpha * h_ref[...] + acc_ref[...], g_ref[...], b_ref[...])


def _moe(h, gate, w, *, tm, alpha):
    t = h.shape[0]
    row = lambda n: pl.BlockSpec((tm, n), lambda i, e: (i, 0))
    full = lambda a: pl.BlockSpec(a.shape, lambda i, e: (0,) * a.ndim)
    exp = lambda a: pl.BlockSpec((1,) + a.shape[1:], lambda i, e: (e, 0, 0))
    return pl.pallas_call(
        functools.partial(_moe_kernel, alpha=alpha),
        grid=(t // tm, N_EXPERTS),
        in_specs=[row(D_MODEL), row(LANES), exp(w['w_gate']), exp(w['w_up']), exp(w['w_down']),
                  full(w['ln2_g']), full(w['ln2_b'])],
        out_specs=row(D_MODEL),
        out_shape=jax.ShapeDtypeStruct((t, D_MODEL), F32),
        scratch_shapes=[pltpu.VMEM((tm, D_MODEL), BF16), pltpu.VMEM((tm, D_MODEL), F32)],
        compiler_params=_cparams(("parallel", "arbitrary")),
        name="moe",
    )(h, gate, w['w_gate'], w['w_up'], w['w_down'], w['ln2_g'], w['ln2_b'])


def _rope_tables(pos):
    pos = pos.astype(F32)[:, None]
    lane = jnp.arange(LANES)
    inv_m = ROPE_THETA ** (-jnp.arange(0, MLA_ROPE, 2, dtype=F32) / MLA_ROPE)
    ang_m = pos * inv_m
    inv_d = ROPE_THETA ** (-jnp.arange(0, DIFF_ROT, 2, dtype=F32) / DIFF_ROT)
    ang_d = pos * inv_d
    wide_m = ang_m[:, lane % (MLA_ROPE // 2)]
    wide_d = ang_d[:, lane % (DIFF_ROT // 2)]
    tabs = (jnp.cos(wide_m), jnp.sin(wide_m), jnp.cos(wide_d), jnp.sin(wide_d))
    tabs_t = (jnp.cos(ang_m).T, jnp.sin(ang_m).T, jnp.cos(ang_d).T, jnp.sin(ang_d).T)
    return tabs, tabs_t


def _prep_weights(l, w_in, mla_q_norm, mla_w_uq, mla_kv_norm, mla_w_ukv, diff_subln, w_out, ln1_g, ln1_b,
                  ln2_g, ln2_b, w_route_group, b_route_group, w_route_expert, b_route_expert,
                  w_exp_gate, w_exp_up, w_exp_down, lam_rows):
    wi = w_in[l]
    s0, s1, s2, s3, s4 = 256, 512, 544, 1056, 1568
    w_in_r = jnp.concatenate([wi[:, :s0], wi[:, s0:s1], wi[:, s3:s4], wi[:, s4:],
                              wi[:, s1:s2], jnp.zeros((D_MODEL, LANES - MLA_ROPE), F32)], axis=1)
    w_uq = jnp.pad(mla_w_uq[l].reshape(MLA_Q_RANK, MLA_HEADS, MLA_NOPE + MLA_ROPE),
                   ((0, 0), (0, 0), (0, HEAD_PAD - MLA_NOPE - MLA_ROPE))).reshape(MLA_Q_RANK, -1)
    ukv = mla_w_ukv[l].reshape(MLA_KV_RANK, MLA_HEADS, MLA_NOPE + MLA_V)
    w_uk = jnp.pad(ukv[..., :MLA_NOPE], ((0, 0), (0, 0), (0, HEAD_PAD - MLA_NOPE))).reshape(MLA_KV_RANK, -1)
    w_uv = ukv[..., MLA_NOPE:].reshape(MLA_KV_RANK, -1)
    rows = jnp.arange(LANES)[:, None]
    cols = jnp.arange(MLA_HEADS * HEAD_PAD)[None, :]
    e_place = ((rows < MLA_ROPE) & (cols % HEAD_PAD == rows + MLA_NOPE)).astype(BF16)
    w_route = jnp.concatenate([w_route_expert[l], w_route_group[l],
                               jnp.zeros((D_MODEL, LANES - N_EXPERTS - N_GROUPS), F32)], axis=1)
    b_route = jnp.concatenate([b_route_expert[l], b_route_group[l],
                               jnp.zeros((LANES - N_EXPERTS - N_GROUPS,), F32)])[None, :]
    return dict(
        w_in=w_in_r.astype(BF16), w_dqt=wi[:, s2:s3].T.astype(BF16),
        q_norm=mla_q_norm[l][None, :], w_uqt=w_uq.T.astype(BF16),
        kv_norm=mla_kv_norm[l][None, :], w_ukv=jnp.concatenate([w_uk, w_uv], axis=1).astype(BF16),
        w_uvt=w_uv.T.astype(BF16),
        e_place=e_place, subln=diff_subln[l][None, :], lam_rows=lam_rows,
        w_out_a=w_out[l][:MLA_HEADS * MLA_V].astype(BF16), w_out_b=w_out[l][MLA_HEADS * MLA_V:].astype(BF16),
        ln1_g=ln1_g[l][None, :], ln1_b=ln1_b[l][None, :], ln2_g=ln2_g[l][None, :], ln2_b=ln2_b[l][None, :],
        w_route=w_route.astype(BF16), b_route=b_route,
        w_gate=w_exp_gate[l].astype(BF16), w_up=w_exp_up[l].astype(BF16), w_down=w_exp_down[l].astype(BF16))


def _block_output(x, mla_o, diff_o, w, alpha):
    t = x.shape[0]
    h, gate = _post_attn(x, mla_o.reshape(t, -1), diff_o.reshape(t, -1), w, tm=min(512, t), alpha=alpha)
    return _moe(h, gate, w, tm=min(1024, t), alpha=alpha)


def kernel(x_prompt, x_sample, cache_mla_ckv, cache_mla_kpe, cache_diff_k, cache_diff_v, w_in, mla_q_norm, mla_w_uq, mla_kv_norm, mla_w_ukv, diff_lambda_q1, diff_lambda_k1, diff_lambda_q2, diff_lambda_k2, diff_subln, w_out, ln1_g, ln1_b, ln2_g, ln2_b, w_route_group, b_route_group, w_route_expert, b_route_expert, w_exp_gate, w_exp_up, w_exp_down):
    depth = w_in.shape[0]
    bp, sp, _ = x_prompt.shape
    bs, ss, _ = x_sample.shape
    past = cache_mla_ckv.shape[2]
    alpha = (2.0 * depth) ** 0.25
    tabs_p, tabs_pt = _rope_tables(jnp.arange(sp))
    tabs_s, tabs_st = _rope_tables(past + jnp.arange(ss))
    hp = x_prompt.reshape(bp * sp, D_MODEL)
    hs = x_sample.reshape(bs * ss, D_MODEL)
    outs = [[] for _ in range(8)]
    for l in range(depth):
        lam_init = 0.8 - 0.6 * math.exp(-0.3 * l)
        lam_rows = jnp.stack([diff_lambda_q1[l], diff_lambda_k1[l], diff_lambda_q2[l], diff_lambda_k2[l]])
        w = _prep_weights(l, w_in, mla_q_norm, mla_w_uq, mla_kv_norm, mla_w_ukv, diff_subln, w_out,
                          ln1_g, ln1_b, ln2_g, ln2_b, w_route_group, b_route_group, w_route_expert,
                          b_route_expert, w_exp_gate, w_exp_up, w_exp_down, lam_rows)

        ckv, kpe, dk, dv, qt, km, _, vt, dqt, dkb, _, dvt = _proj(hp, tabs_p, tabs_pt, w, tm=ATTN_TILE)
        b3 = lambda a: a.reshape(bp, sp, -1)
        mla_o = _mla_flash(qt, b3(km), vt, t=ATTN_TILE, heads=4)
        diff_o = _diff_flash(w['lam_rows'], w['subln'], dqt, b3(dkb), dvt, t=ATTN_TILE, heads=2,
                             lam_init=lam_init)
        outs[0].append(ckv.reshape(bp, sp, MLA_KV_RANK))
        outs[1].append(kpe.reshape(bp, sp, MLA_ROPE))
        outs[2].append(dk.reshape(bp, sp, DIFF_HEADS, 2 * DIFF_D))
        outs[3].append(dv.reshape(bp, sp, DIFF_HEADS, DIFF_V))
        hp = _block_output(hp, mla_o, diff_o, w, alpha)

        ckv_s, kpe_s, dk_s, dv_s, qt_s, km_s, vm_s, _, dqt_s, dkb_s, dvb_s, _ = _proj(hs, tabs_s, tabs_st, w, tm=ss)
        km_past, vm_past = _expand(cache_mla_ckv[l].reshape(bs * past, MLA_KV_RANK),
                                   cache_mla_kpe[l].reshape(bs * past, MLA_ROPE), w, tm=512)
        s3 = lambda a: a.reshape(bs, ss, -1)
        p3 = lambda a: a.reshape(bs, past, -1)
        mla_o_s = _mla_step(qt_s, p3(km_past), p3(vm_past), s3(km_s), s3(vm_s))
        diff_o_s = _diff_step(w['lam_rows'], w['subln'], dqt_s, p3(cache_diff_k[l]), p3(cache_diff_v[l]),
                              s3(dkb_s), s3(dvb_s), lam_init=lam_init)
        outs[4].append(ckv_s.reshape(bs, ss, MLA_KV_RANK))
        outs[5].append(kpe_s.reshape(bs, ss, MLA_ROPE))
        outs[6].append(dk_s.reshape(bs, ss, DIFF_HEADS, 2 * DIFF_D))
        outs[7].append(dv_s.reshape(bs, ss, DIFF_HEADS, DIFF_V))
        hs = _block_output(hs, mla_o_s, diff_o_s, w, alpha)

    return (hp.reshape(bp, sp, D_MODEL), hs.reshape(bs, ss, D_MODEL)) + tuple(jnp.stack(o) for o in outs)
```

```python
import functools
import math

import jax
import jax.numpy as jnp
from jax import lax
from jax.experimental import pallas as pl
from jax.experimental.pallas import tpu as pltpu

F32 = jnp.float32
BF16 = jnp.bfloat16

D_MODEL = 1024
CHUNK = 64
ROPE_THETA = 500000.0
MLA_HEADS = 8
MLA_NOPE = 64
MLA_ROPE = 32
MLA_V = 64
MLA_Q_RANK = 256
MLA_KV_RANK = 256
MLA_SCALE = (MLA_NOPE + MLA_ROPE) ** -0.5
DIFF_HEADS = 4
DIFF_D = 64
DIFF_V = 2 * DIFF_D
DIFF_ROT = DIFF_D // 4
DIFF_SCALE = DIFF_D ** -0.5
N_GROUPS = 4
EXP_PER_GROUP = 8
N_EXPERTS = N_GROUPS * EXP_PER_GROUP
D_EXPERT = 256
EPS_LN = 1e-5
EPS_RMS = 1e-6
LOG2E = math.log2(math.e)

LANES = 128
HEAD_PAD = LANES
ATTN_TILE = 256
BF16_ROWS = 16
MLA_VA = MLA_V + BF16_ROWS
DIFF_VA = DIFF_V + BF16_ROWS
VMEM_LIMIT = 56 * 1024 * 1024

_CQ, _CKV, _DK, _DV, _KR, _IN_COLS_R = 0, 256, 512, 1024, 1536, 1664


def _cparams(sem):
    return pltpu.CompilerParams(dimension_semantics=sem, vmem_limit_bytes=VMEM_LIMIT)


def _rms(x, g):
    return x * lax.rsqrt(jnp.mean(x * x, axis=-1, keepdims=True) + EPS_RMS) * g


def _layer_norm(x, g, b):
    mu = jnp.mean(x, axis=-1, keepdims=True)
    xc = x - mu
    var = jnp.mean(xc * xc, axis=-1, keepdims=True)
    return xc * lax.rsqrt(var + EPS_LN) * g + b


def _nt_dot(a, b):
    return lax.dot_general(a, b, (((1,), (1,)), ((), ())), preferred_element_type=F32)


def _rope_coeffs(cos_t, sin_t, lo, half, period):
    lane = lax.broadcasted_iota(jnp.int32, cos_t.shape, 1) & (period - 1)
    is1 = (lane >= lo) & (lane < lo + half)
    is2 = (lane >= lo + half) & (lane < lo + 2 * half)
    c = jnp.where(is1 | is2, cos_t, 1.0)
    a = jnp.where(is1, -sin_t, 0.0)
    b = jnp.where(is2, sin_t, 0.0)
    return c, a, b


def _rope_apply(x, coeffs, half):
    c, a, b = coeffs
    return x * c + pltpu.roll(x, LANES - half, 1) * a + pltpu.roll(x, half, 1) * b


def _rope_rows(x, cos_t, sin_t):
    half = cos_t.shape[0]
    x1, x2 = x[:half], x[half:]
    return x1 * cos_t - x2 * sin_t, x1 * sin_t + x2 * cos_t


def _proj_kernel(x_ref, w_in_ref, w_dqt_ref, qn_ref, w_uqt_ref, kvn_ref, w_ukv_ref, w_uvt_ref, e_ref,
                 cm_ref, sm_ref, cd_ref, sd_ref, cmt_ref, smt_ref, cdt_ref, sdt_ref,
                 ckv_ref, kpe_ref, dk_ref, dv_ref, qt_ref, km_ref, vm_ref, vt_ref,
                 dqt_ref, dkb_ref, dvb_ref, dvt_ref):
    xb = x_ref[...].astype(BF16)
    tm = xb.shape[0]
    z = jnp.dot(xb, w_in_ref[...], preferred_element_type=F32)
    half_m = MLA_ROPE // 2
    half_d = DIFF_ROT // 2
    coef_k = _rope_coeffs(cm_ref[...], sm_ref[...], 0, half_m, LANES)
    coef_d = _rope_coeffs(cd_ref[...], sd_ref[...], 0, half_d, DIFF_D)

    cq = _rms(z[:, _CQ:_CQ + MLA_Q_RANK], qn_ref[...])
    qt = _nt_dot(w_uqt_ref[...], cq.astype(BF16)) * (MLA_SCALE * LOG2E)
    cos_mt, sin_mt = cmt_ref[...], smt_ref[...]
    for h in range(MLA_HEADS):
        r0 = h * HEAD_PAD
        qt_ref[0, r0:r0 + MLA_NOPE, :] = qt[r0:r0 + MLA_NOPE].astype(BF16)
        o1, o2 = _rope_rows(qt[r0 + MLA_NOPE:r0 + MLA_NOPE + MLA_ROPE], cos_mt, sin_mt)
        qt_ref[0, r0 + MLA_NOPE:r0 + MLA_NOPE + half_m, :] = o1.astype(BF16)
        qt_ref[0, r0 + MLA_NOPE + half_m:r0 + MLA_NOPE + MLA_ROPE, :] = o2.astype(BF16)
        qt_ref[0, r0 + MLA_NOPE + MLA_ROPE:r0 + HEAD_PAD, :] = jnp.zeros((HEAD_PAD - MLA_NOPE - MLA_ROPE, tm), BF16)

    ckv = _rms(z[:, _CKV:_CKV + MLA_KV_RANK], kvn_ref[...])
    ckv_ref[...] = ckv
    ckv_b = ckv.astype(BF16)
    kslab = _rope_apply(z[:, _KR:_KR + LANES], coef_k, half_m)
    kpe_ref[...] = kslab[:, :MLA_ROPE]
    kv = jnp.dot(ckv_b, w_ukv_ref[...], preferred_element_type=F32)
    k_full = kv[:, :MLA_HEADS * HEAD_PAD] + jnp.dot(kslab.astype(BF16), e_ref[...], preferred_element_type=F32)
    km_ref[...] = k_full.astype(BF16)
    vm_ref[...] = kv[:, MLA_HEADS * HEAD_PAD:].astype(BF16)
    ones_rows = jnp.where(lax.broadcasted_iota(jnp.int32, (BF16_ROWS, tm), 0) == 0, 1.0, 0.0).astype(BF16)
    vt = _nt_dot(w_uvt_ref[...], ckv_b)
    for h in range(MLA_HEADS):
        vt_ref[0, h * MLA_VA:h * MLA_VA + MLA_V, :] = vt[h * MLA_V:(h + 1) * MLA_V].astype(BF16)
        vt_ref[0, h * MLA_VA + MLA_V:(h + 1) * MLA_VA, :] = ones_rows

    dqt = _nt_dot(w_dqt_ref[...], xb) * (DIFF_SCALE * LOG2E)
    cos_dt, sin_dt = cdt_ref[...], sdt_ref[...]
    zeros = jnp.zeros((DIFF_D, tm), BF16)
    for n in range(2 * DIFF_HEADS):
        r0 = n * DIFF_D
        o1, o2 = _rope_rows(dqt[r0:r0 + DIFF_ROT], cos_dt, sin_dt)
        qn = jnp.concatenate([o1, o2, dqt[r0 + DIFF_ROT:r0 + DIFF_D]], axis=0).astype(BF16)
        lo, hi = (qn, zeros) if n % 2 == 0 else (zeros, qn)
        dqt_ref[0, n * LANES:n * LANES + DIFF_D, :] = lo
        dqt_ref[0, n * LANES + DIFF_D:(n + 1) * LANES, :] = hi

    dv = z[:, _DV:_DV + DIFF_HEADS * DIFF_V]
    dvb_ref[...] = dv.astype(BF16)
    dvt = dv.T
    for h in range(DIFF_HEADS):
        sl = slice(h * LANES, (h + 1) * LANES)
        dk = _rope_apply(z[:, _DK + h * LANES:_DK + (h + 1) * LANES], coef_d, half_d)
        dk_ref[:, h, :] = dk
        dkb_ref[:, sl] = dk.astype(BF16)
        dv_ref[:, h, :] = dv[:, sl]
        dvt_ref[0, h * DIFF_VA:h * DIFF_VA + DIFF_V, :] = dvt[sl].astype(BF16)
        dvt_ref[0, h * DIFF_VA + DIFF_V:(h + 1) * DIFF_VA, :] = ones_rows


def _proj(x, tabs, tabs_t, w, *, tm):
    t = x.shape[0]
    n_pos_blocks = tabs[0].shape[0] // tm
    row = lambda n: pl.BlockSpec((tm, n), lambda i: (i, 0))
    full = lambda a: pl.BlockSpec(a.shape, lambda i: (0,) * a.ndim)
    tab = pl.BlockSpec((tm, LANES), lambda i: (i % n_pos_blocks, 0))
    tab_t = lambda a: pl.BlockSpec((a.shape[0], tm), lambda i: (0, i % n_pos_blocks))
    tr = lambda n: pl.BlockSpec((1, n, tm), lambda i: (i, 0, 0))
    hd = lambda n: pl.BlockSpec((tm, DIFF_HEADS, n), lambda i: (i, 0, 0))
    weights = (w['w_in'], w['w_dqt'], w['q_norm'], w['w_uqt'], w['kv_norm'], w['w_ukv'], w['w_uvt'], w['e_place'])
    outs = ((row, MLA_KV_RANK, F32), (row, MLA_ROPE, F32), (hd, 2 * DIFF_D, F32), (hd, DIFF_V, F32),
            (tr, 1024, BF16), (row, 1024, BF16), (row, 512, BF16), (tr, MLA_HEADS * MLA_VA, BF16),
            (tr, 1024, BF16), (row, 512, BF16), (row, 512, BF16), (tr, DIFF_HEADS * DIFF_VA, BF16))
    shape = lambda kind, n: {tr: (t // tm, n, tm), hd: (t, DIFF_HEADS, n), row: (t, n)}[kind]
    return pl.pallas_call(
        _proj_kernel,
        grid=(t // tm,),
        in_specs=[row(D_MODEL)] + [full(a) for a in weights] + [tab] * 4 + [tab_t(a) for a in tabs_t],
        out_specs=[kind(n) for kind, n, _ in outs],
        out_shape=[jax.ShapeDtypeStruct(shape(kind, n), d) for kind, n, d in outs],
        compiler_params=_cparams(("parallel",)),
        name="proj",
    )(x, *weights, *tabs, *tabs_t)


def _expand_kernel(ckv_ref, kpe_ref, w_ukv_ref, e_ref, km_ref, vm_ref):
    kv = jnp.dot(ckv_ref[...].astype(BF16), w_ukv_ref[...], preferred_element_type=F32)
    k_full = kv[:, :MLA_HEADS * HEAD_PAD] + jnp.dot(kpe_ref[...].astype(BF16), e_ref[:MLA_ROPE, :],
                                                   preferred_element_type=F32)
    km_ref[...] = k_full.astype(BF16)
    vm_ref[...] = kv[:, MLA_HEADS * HEAD_PAD:].astype(BF16)


def _expand(ckv, kpe, w, *, tm):
    r = ckv.shape[0]
    row = lambda n: pl.BlockSpec((tm, n), lambda i: (i, 0))
    full = lambda a: pl.BlockSpec(a.shape, lambda i: (0,) * a.ndim)
    return pl.pallas_call(
        _expand_kernel,
        grid=(r // tm,),
        in_specs=[row(MLA_KV_RANK), row(MLA_ROPE), full(w['w_ukv']), full(w['e_place'])],
        out_specs=[row(1024), row(512)],
        out_shape=[jax.ShapeDtypeStruct((r, 1024), BF16), jax.ShapeDtypeStruct((r, 512), BF16)],
        compiler_params=_cparams(("parallel",)),
        name="expand",
    )(ckv, kpe, w['w_ukv'], w['e_place'])


def _chunk_mask_t(t):
    shift = CHUNK.bit_length() - 1
    kc = lax.broadcasted_iota(jnp.int32, (t, t), 0) >> shift
    qc = lax.broadcasted_iota(jnp.int32, (t, t), 1) >> shift
    return kc <= qc


def _flash_scratch(t, dv, n):
    return [pltpu.VMEM((n, t, t), F32), pltpu.VMEM((n, t, t), F32),
            pltpu.VMEM((n, 1, t), F32), pltpu.VMEM((n, dv, t), F32)]


def _flash_loop(i, score, value, n, s_a, s_b, m_ref, acc_ref):
    t = s_a.shape[-1]
    mask = _chunk_mask_t(t)
    m_ref[...] = jnp.full(m_ref.shape, -jnp.inf, F32)
    acc_ref[...] = jnp.zeros(acc_ref.shape, F32)

    def step(j, s_cur, s_nxt, masked):
        if s_nxt is not None:
            for k in range(n):
                s_nxt[k] = score(j + 1, k)
        for k in range(n):
            s = s_cur[k]
            if masked:
                s = jnp.where(mask, s, -jnp.inf)
            m = m_ref[k]
            m_new = jnp.maximum(m, jnp.max(s, axis=0, keepdims=True))
            alpha = jnp.exp2(m - m_new)
            p = jnp.exp2(s - m_new)
            m_ref[k] = m_new
            acc_ref[k] = alpha * acc_ref[k] + jnp.dot(value(j, k), p.astype(BF16), preferred_element_type=F32)

    for k in range(n):
        s_a[k] = score(0, k)

    def pair(jj, carry):
        step(2 * jj, s_a, s_b, False)
        step(2 * jj + 1, s_b, s_a, False)
        return carry

    lax.fori_loop(0, i // 2, pair, 0)

    @pl.when(i % 2 == 0)
    def _():
        step(i, s_a, None, True)

    @pl.when(i % 2 == 1)
    def _():
        step(i - 1, s_a, s_b, False)
        step(i, s_b, None, True)


def _normalised(acc_ref, k, dv):
    return acc_ref[k, :dv, :] / acc_ref[k, dv:dv + 1, :]


def _mla_flash_kernel(qt_ref, k_ref, vt_ref, o_ref, s_a, s_b, m_ref, acc_ref, *, t, heads):
    def score(j, h):
        rows = pl.ds(pl.multiple_of(j * t, t), t)
        return jnp.dot(k_ref[0, rows, h * HEAD_PAD:(h + 1) * HEAD_PAD],
                       qt_ref[0, h * HEAD_PAD:(h + 1) * HEAD_PAD, :], preferred_element_type=F32)

    value = lambda j, h: vt_ref[j, h * MLA_VA:(h + 1) * MLA_VA, :]
    _flash_loop(pl.program_id(2), score, value, heads, s_a, s_b, m_ref, acc_ref)
    for h in range(0, heads, 2):
        pair = jnp.concatenate([_normalised(acc_ref, h, MLA_V), _normalised(acc_ref, h + 1, MLA_V)], axis=0)
        o_ref[0, :, h * MLA_V:(h + 2) * MLA_V] = pair.T.astype(BF16)


def _mla_flash(qt, km, vt, *, t, heads):
    b, s, _ = km.shape
    nk = s // t
    return pl.pallas_call(
        functools.partial(_mla_flash_kernel, t=t, heads=heads),
        grid=(b, MLA_HEADS // heads, nk),
        in_specs=[pl.BlockSpec((1, heads * HEAD_PAD, t), lambda bi, hi, i: (bi * nk + i, hi, 0)),
                  pl.BlockSpec((1, s, heads * HEAD_PAD), lambda bi, hi, i: (bi, 0, hi)),
                  pl.BlockSpec((nk, heads * MLA_VA, t), lambda bi, hi, i: (bi, hi, 0))],
        out_specs=pl.BlockSpec((1, t, heads * MLA_V), lambda bi, hi, i: (bi, i, hi)),
        out_shape=jax.ShapeDtypeStruct((b, s, MLA_HEADS * MLA_V), BF16),
        scratch_shapes=_flash_scratch(t, MLA_VA, heads),
        compiler_params=_cparams(("parallel", "parallel", "arbitrary")),
        name="mla_flash",
    )(qt, km, vt)


def _diff_lambda(lam_ref, lam_init):
    lq1, lk1, lq2, lk2 = (lam_ref[r:r + 1, :] for r in range(4))
    return (jnp.exp(jnp.sum(lq1 * lk1, axis=-1, keepdims=True))
            - jnp.exp(jnp.sum(lq2 * lk2, axis=-1, keepdims=True)) + lam_init)


def _diff_finish(o1, o2, lam, subln, lam_init):
    o = o1 - lam * o2
    return _rms(o, subln) * (1.0 - lam_init)


def _diff_flash_kernel(lam_ref, subln_ref, qt_ref, k_ref, vt_ref, o_ref, s_a, s_b, m_ref, acc_ref,
                       *, t, heads, lam_init):
    def score(j, n):
        rows = pl.ds(pl.multiple_of(j * t, t), t)
        return jnp.dot(k_ref[0, rows, (n // 2) * LANES:(n // 2 + 1) * LANES],
                       qt_ref[0, n * LANES:(n + 1) * LANES, :], preferred_element_type=F32)

    value = lambda j, n: vt_ref[j, (n // 2) * DIFF_VA:(n // 2 + 1) * DIFF_VA, :]
    _flash_loop(pl.program_id(2), score, value, 2 * heads, s_a, s_b, m_ref, acc_ref)
    lam = _diff_lambda(lam_ref, lam_init)
    for h in range(heads):
        o1 = _normalised(acc_ref, 2 * h, DIFF_V).T
        o2 = _normalised(acc_ref, 2 * h + 1, DIFF_V).T
        o = _diff_finish(o1, o2, lam, subln_ref[...], lam_init)
        o_ref[0, :, h * DIFF_V:(h + 1) * DIFF_V] = o.astype(BF16)


def _diff_flash(lam_rows, subln, dqt, dkb, dvt, *, t, heads, lam_init):
    b, s, _ = dkb.shape
    nk = s // t
    full = lambda a: pl.BlockSpec(a.shape, lambda bi, hi, i: (0,) * a.ndim)
    return pl.pallas_call(
        functools.partial(_diff_flash_kernel, t=t, heads=heads, lam_init=lam_init),
        grid=(b, DIFF_HEADS // heads, nk),
        in_specs=[full(lam_rows), full(subln),
                  pl.BlockSpec((1, heads * 2 * LANES, t), lambda bi, hi, i: (bi * nk + i, hi, 0)),
                  pl.BlockSpec((1, s, heads * LANES), lambda bi, hi, i: (bi, 0, hi)),
                  pl.BlockSpec((nk, heads * DIFF_VA, t), lambda bi, hi, i: (bi, hi, 0))],
        out_specs=pl.BlockSpec((1, t, heads * DIFF_V), lambda bi, hi, i: (bi, i, hi)),
        out_shape=jax.ShapeDtypeStruct((b, s, DIFF_HEADS * DIFF_V), BF16),
        scratch_shapes=_flash_scratch(t, DIFF_VA, 2 * heads),
        compiler_params=_cparams(("parallel", "parallel", "arbitrary")),
        name="diff_flash",
    )(lam_rows, subln, dqt, dkb, dvt)


def _two_part_softmax_pv(s_past, s_new, v_past, v_new):
    m = jnp.maximum(jnp.max(s_past, axis=-1, keepdims=True), jnp.max(s_new, axis=-1, keepdims=True))
    p_past = jnp.exp2(s_past - m)
    p_new = jnp.exp2(s_new - m)
    l = jnp.sum(p_past, axis=-1, keepdims=True) + jnp.sum(p_new, axis=-1, keepdims=True)
    acc = (jnp.dot(p_past.astype(BF16), v_past, preferred_element_type=F32)
           + jnp.dot(p_new.astype(BF16), v_new, preferred_element_type=F32))
    return acc / l


def _token_major(qt_ref):
    return qt_ref[0].astype(F32).T.astype(BF16)


def _mla_step_kernel(qt_ref, kp_ref, vp_ref, kn_ref, vn_ref, o_ref):
    q_all = _token_major(qt_ref)
    for h in range(MLA_HEADS):
        ks = slice(h * HEAD_PAD, (h + 1) * HEAD_PAD)
        vs = slice(h * MLA_V, (h + 1) * MLA_V)
        q = q_all[:, ks]
        o = _two_part_softmax_pv(_nt_dot(q, kp_ref[0, :, ks]), _nt_dot(q, kn_ref[0, :, ks]),
                                 vp_ref[0, :, vs], vn_ref[0, :, vs])
        o_ref[0, :, vs] = o.astype(BF16)


def _mla_step(qt, km_past, vm_past, km_new, vm_new):
    b, _, n = qt.shape
    blk = lambda a: pl.BlockSpec((1,) + a.shape[1:], lambda bi: (bi, 0, 0))
    args = (qt, km_past, vm_past, km_new, vm_new)
    return pl.pallas_call(
        _mla_step_kernel,
        grid=(b,),
        in_specs=[blk(a) for a in args],
        out_specs=pl.BlockSpec((1, n, MLA_HEADS * MLA_V), lambda bi: (bi, 0, 0)),
        out_shape=jax.ShapeDtypeStruct((b, n, MLA_HEADS * MLA_V), BF16),
        compiler_params=_cparams(("parallel",)),
        name="mla_step",
    )(*args)


def _diff_step_kernel(lam_ref, subln_ref, qt_ref, kp_ref, vp_ref, kn_ref, vn_ref, o_ref, *, lam_init):
    lam = _diff_lambda(lam_ref, lam_init)
    q = _token_major(qt_ref)
    for h in range(DIFF_HEADS):
        sl = slice(h * LANES, (h + 1) * LANES)
        q1 = q[:, 2 * h * LANES:(2 * h + 1) * LANES]
        q2 = q[:, (2 * h + 1) * LANES:(2 * h + 2) * LANES]
        kp = kp_ref[0, :, h, :].astype(BF16)
        vp = vp_ref[0, :, h, :].astype(BF16)
        kn = kn_ref[0, :, sl]
        vn = vn_ref[0, :, sl]
        o1 = _two_part_softmax_pv(_nt_dot(q1, kp), _nt_dot(q1, kn), vp, vn)
        o2 = _two_part_softmax_pv(_nt_dot(q2, kp), _nt_dot(q2, kn), vp, vn)
        o_ref[0, :, sl] = _diff_finish(o1, o2, lam, subln_ref[...], lam_init).astype(BF16)


def _diff_step(lam_rows, subln, dqt, k_past, v_past, dkb, dvb, *, lam_init):
    b, _, n = dqt.shape
    full = lambda a: pl.BlockSpec(a.shape, lambda bi: (0,) * a.ndim)
    blk = lambda a: pl.BlockSpec((1,) + a.shape[1:], lambda bi: (bi,) + (0,) * (a.ndim - 1))
    args = (dqt, k_past, v_past, dkb, dvb)
    return pl.pallas_call(
        functools.partial(_diff_step_kernel, lam_init=lam_init),
        grid=(b,),
        in_specs=[full(lam_rows), full(subln)] + [blk(a) for a in args],
        out_specs=pl.BlockSpec((1, n, DIFF_HEADS * DIFF_V), lambda bi: (bi, 0, 0)),
        out_shape=jax.ShapeDtypeStruct((b, n, DIFF_HEADS * DIFF_V), BF16),
        compiler_params=_cparams(("parallel",)),
        name="diff_step",
    )(lam_rows, subln, *args)


def _first_index_of_max(vals, lane):
    m = jnp.max(vals, axis=-1, keepdims=True)
    idx = jnp.min(jnp.where(vals == m, lane, float(LANES)), axis=-1, keepdims=True)
    return m, idx


def _post_attn_kernel(x_ref, mla_ref, diff_ref, wo_a_ref, wo_b_ref, g_ref, b_ref, wr_ref, br_ref,
                      h_ref, gate_ref, *, alpha):
    a = (jnp.dot(mla_ref[...], wo_a_ref[...], preferred_element_type=F32)
         + jnp.dot(diff_ref[...], wo_b_ref[...], preferred_element_type=F32))
    h = _layer_norm(alpha * x_ref[...] + a, g_ref[...], b_ref[...])
    h_ref[...] = h

    logit = jnp.dot(h.astype(BF16), wr_ref[...], preferred_element_type=F32) + br_ref[...]
    lane_i = lax.broadcasted_iota(jnp.int32, logit.shape, 1)
    lane = lane_i.astype(F32)
    group_of_lane = (lane_i >> (EXP_PER_GROUP.bit_length() - 1)).astype(F32)
    neg = -jnp.inf
    g_logit = jnp.where((lane_i >= N_EXPERTS) & (lane_i < N_EXPERTS + N_GROUPS), logit, neg)
    g_max, g_lane = _first_index_of_max(g_logit, lane)
    g_w = 1.0 / jnp.sum(jnp.exp(g_logit - g_max), axis=-1, keepdims=True)
    g_idx = g_lane - float(N_EXPERTS)
    e_logit = jnp.where(group_of_lane == g_idx, logit, neg)
    m1, i1 = _first_index_of_max(e_logit, lane)
    m2, i2 = _first_index_of_max(jnp.where(lane == i1, neg, e_logit), lane)
    r = jnp.exp(m2 - m1)
    p1 = 1.0 / (1.0 + r)
    p2 = r / (1.0 + r)
    gate_ref[...] = g_w * (jnp.where(lane == i1, p1, 0.0) + jnp.where(lane == i2, p2, 0.0))


def _post_attn(x, mla_o, diff_o, w, *, tm, alpha):
    t = x.shape[0]
    row = lambda n: pl.BlockSpec((tm, n), lambda i: (i, 0))
    full = lambda a: pl.BlockSpec(a.shape, lambda i: (0,) * a.ndim)
    weights = (w['w_out_a'], w['w_out_b'], w['ln1_g'], w['ln1_b'], w['w_route'], w['b_route'])
    return pl.pallas_call(
        functools.partial(_post_attn_kernel, alpha=alpha),
        grid=(t // tm,),
        in_specs=[row(D_MODEL), row(512), row(512)] + [full(a) for a in weights],
        out_specs=[row(D_MODEL), row(LANES)],
        out_shape=[jax.ShapeDtypeStruct((t, D_MODEL), F32), jax.ShapeDtypeStruct((t, LANES), F32)],
        compiler_params=_cparams(("parallel",)),
        name="post_attn",
    )(x, mla_o, diff_o, *weights)


def _moe_kernel(h_ref, gate_ref, wg_ref, wu_ref, wd_ref, g_ref, b_ref, y_ref, hb_ref, acc_ref, *, alpha):
    e = pl.program_id(1)

    @pl.when(e == 0)
    def _():
        hb_ref[...] = h_ref[...].astype(BF16)
        acc_ref[...] = jnp.zeros_like(acc_ref)

    hb = hb_ref[...]
    lane = lax.broadcasted_iota(jnp.int32, gate_ref.shape, 1)
    gate_e = jnp.sum(jnp.where(lane == e, gate_ref[...], 0.0), axis=-1, keepdims=True)
    hid = (jax.nn.silu(jnp.dot(hb, wg_ref[0], preferred_element_type=F32))
           * jnp.dot(hb, wu_ref[0], preferred_element_type=F32))
    acc_ref[...] += jnp.dot((hid * gate_e).astype(BF16), wd_ref[0], preferred_element_type=F32)

    @pl.when(e == N_EXPERTS - 1)
    def _():
        y_ref[...] = _layer_norm(alpha * h_ref[...] + acc_ref[...], g_ref[...], b_ref[...])


def _moe(h, gate, w, *, tm, alpha):
    t = h.shape[0]
    row = lambda n: pl.BlockSpec((tm, n), lambda i, e: (i, 0))
    full = lambda a: pl.BlockSpec(a.shape, lambda i, e: (0,) * a.ndim)
    exp = lambda a: pl.BlockSpec((1,) + a.shape[1:], lambda i, e: (e, 0, 0))
    return pl.pallas_call(
        functools.partial(_moe_kernel, alpha=alpha),
        grid=(t // tm, N_EXPERTS),
        in_specs=[row(D_MODEL), row(LANES), exp(w['w_gate']), exp(w['w_up']), exp(w['w_down']),
                  full(w['ln2_g']), full(w['ln2_b'])],
        out_specs=row(D_MODEL),
        out_shape=jax.ShapeDtypeStruct((t, D_MODEL), F32),
        scratch_shapes=[pltpu.VMEM((tm, D_MODEL), BF16), pltpu.VMEM((tm, D_MODEL), F32)],
        compiler_params=_cparams(("parallel", "arbitrary")),
        name="moe",
    )(h, gate, w['w_gate'], w['w_up'], w['w_down'], w['ln2_g'], w['ln2_b'])


def _rope_tables(pos):
    pos = pos.astype(F32)[:, None]
    lane = jnp.arange(LANES)
    inv_m = ROPE_THETA ** (-jnp.arange(0, MLA_ROPE, 2, dtype=F32) / MLA_ROPE)
    ang_m = pos * inv_m
    inv_d = ROPE_THETA ** (-jnp.arange(0, DIFF_ROT, 2, dtype=F32) / DIFF_ROT)
    ang_d = pos * inv_d
    narrow = lax.optimization_barrier((jnp.cos(ang_m), jnp.sin(ang_m), jnp.cos(ang_d), jnp.sin(ang_d)))
    tabs = tuple(a[:, lane % a.shape[1]] for a in narrow)
    tabs_t = tuple(a.T for a in narrow)
    return tabs, tabs_t


def _prep_weights(l, w_in, mla_q_norm, mla_w_uq, mla_kv_norm, mla_w_ukv, diff_subln, w_out, ln1_g, ln1_b,
                  ln2_g, ln2_b, w_route_group, b_route_group, w_route_expert, b_route_expert,
                  w_exp_gate, w_exp_up, w_exp_down, lam_rows):
    wi = w_in[l]
    s0, s1, s2, s3, s4 = 256, 512, 544, 1056, 1568
    w_in_r = jnp.concatenate([wi[:, :s0], wi[:, s0:s1], wi[:, s3:s4], wi[:, s4:],
                              wi[:, s1:s2], jnp.zeros((D_MODEL, LANES - MLA_ROPE), F32)], axis=1)
    w_uq = jnp.pad(mla_w_uq[l].reshape(MLA_Q_RANK, MLA_HEADS, MLA_NOPE + MLA_ROPE),
                   ((0, 0), (0, 0), (0, HEAD_PAD - MLA_NOPE - MLA_ROPE))).reshape(MLA_Q_RANK, -1)
    ukv = mla_w_ukv[l].reshape(MLA_KV_RANK, MLA_HEADS, MLA_NOPE + MLA_V)
    w_uk = jnp.pad(ukv[..., :MLA_NOPE], ((0, 0), (0, 0), (0, HEAD_PAD - MLA_NOPE))).reshape(MLA_KV_RANK, -1)
    w_uv = ukv[..., MLA_NOPE:].reshape(MLA_KV_RANK, -1)
    rows = jnp.arange(LANES)[:, None]
    cols = jnp.arange(MLA_HEADS * HEAD_PAD)[None, :]
    e_place = ((rows < MLA_ROPE) & (cols % HEAD_PAD == rows + MLA_NOPE)).astype(BF16)
    w_route = jnp.concatenate([w_route_expert[l], w_route_group[l],
                               jnp.zeros((D_MODEL, LANES - N_EXPERTS - N_GROUPS), F32)], axis=1)
    b_route = jnp.concatenate([b_route_expert[l], b_route_group[l],
                               jnp.zeros((LANES - N_EXPERTS - N_GROUPS,), F32)])[None, :]
    return dict(
        w_in=w_in_r.astype(BF16), w_dqt=wi[:, s2:s3].T.astype(BF16),
        q_norm=mla_q_norm[l][None, :], w_uqt=w_uq.T.astype(BF16),
        kv_norm=mla_kv_norm[l][None, :], w_ukv=jnp.concatenate([w_uk, w_uv], axis=1).astype(BF16),
        w_uvt=w_uv.T.astype(BF16),
        e_place=e_place, subln=diff_subln[l][None, :], lam_rows=lam_rows,
        w_out_a=w_out[l][:MLA_HEADS * MLA_V].astype(BF16), w_out_b=w_out[l][MLA_HEADS * MLA_V:].astype(BF16),
        ln1_g=ln1_g[l][None, :], ln1_b=ln1_b[l][None, :], ln2_g=ln2_g[l][None, :], ln2_b=ln2_b[l][None, :],
        w_route=w_route.astype(BF16), b_route=b_route,
        w_gate=w_exp_gate[l].astype(BF16), w_up=w_exp_up[l].astype(BF16), w_down=w_exp_down[l].astype(BF16))


def _block_output(x, mla_o, diff_o, w, alpha):
    t = x.shape[0]
    h, gate = _post_attn(x, mla_o.reshape(t, -1), diff_o.reshape(t, -1), w, tm=min(512, t), alpha=alpha)
    return _moe(h, gate, w, tm=min(1024, t), alpha=alpha)


def kernel(x_prompt, x_sample, cache_mla_ckv, cache_mla_kpe, cache_diff_k, cache_diff_v, w_in, mla_q_norm, mla_w_uq, mla_kv_norm, mla_w_ukv, diff_lambda_q1, diff_lambda_k1, diff_lambda_q2, diff_lambda_k2, diff_subln, w_out, ln1_g, ln1_b, ln2_g, ln2_b, w_route_group, b_route_group, w_route_expert, b_route_expert, w_exp_gate, w_exp_up, w_exp_down):
    depth = w_in.shape[0]
    bp, sp, _ = x_prompt.shape
    bs, ss, _ = x_sample.shape
    past = cache_mla_ckv.shape[2]
    alpha = (2.0 * depth) ** 0.25
    tabs_p, tabs_pt = _rope_tables(jnp.arange(sp))
    tabs_s, tabs_st = _rope_tables(past + jnp.arange(ss))
    hp = x_prompt.reshape(bp * sp, D_MODEL)
    hs = x_sample.reshape(bs * ss, D_MODEL)
    outs = [[] for _ in range(8)]
    for l in range(depth):
        lam_init = 0.8 - 0.6 * math.exp(-0.3 * l)
        lam_rows = jnp.stack([diff_lambda_q1[l], diff_lambda_k1[l], diff_lambda_q2[l], diff_lambda_k2[l]])
        w = _prep_weights(l, w_in, mla_q_norm, mla_w_uq, mla_kv_norm, mla_w_ukv, diff_subln, w_out,
                          ln1_g, ln1_b, ln2_g, ln2_b, w_route_group, b_route_group, w_route_expert,
                          b_route_expert, w_exp_gate, w_exp_up, w_exp_down, lam_rows)

        ckv, kpe, dk, dv, qt, km, _, vt, dqt, dkb, _, dvt = _proj(hp, tabs_p, tabs_pt, w, tm=ATTN_TILE)
        b3 = lambda a: a.reshape(bp, sp, -1)
        mla_o = _mla_flash(qt, b3(km), vt, t=ATTN_TILE, heads=4)
        diff_o = _diff_flash(w['lam_rows'], w['subln'], dqt, b3(dkb), dvt, t=ATTN_TILE, heads=2,
                             lam_init=lam_init)
        outs[0].append(ckv.reshape(bp, sp, MLA_KV_RANK))
        outs[1].append(kpe.reshape(bp, sp, MLA_ROPE))
        outs[2].append(dk.reshape(bp, sp, DIFF_HEADS, 2 * DIFF_D))
        outs[3].append(dv.reshape(bp, sp, DIFF_HEADS, DIFF_V))
        hp = _block_output(hp, mla_o, diff_o, w, alpha)

        ckv_s, kpe_s, dk_s, dv_s, qt_s, km_s, vm_s, _, dqt_s, dkb_s, dvb_s, _ = _proj(hs, tabs_s, tabs_st, w, tm=ss)
        km_past, vm_past = _expand(cache_mla_ckv[l].reshape(bs * past, MLA_KV_RANK),
                                   cache_mla_kpe[l].reshape(bs * past, MLA_ROPE), w, tm=512)
        s3 = lambda a: a.reshape(bs, ss, -1)
        p3 = lambda a: a.reshape(bs, past, -1)
        mla_o_s = _mla_step(qt_s, p3(km_past), p3(vm_past), s3(km_s), s3(vm_s))
        diff_o_s = _diff_step(w['lam_rows'], w['subln'], dqt_s, cache_diff_k[l], cache_diff_v[l],
                              s3(dkb_s), s3(dvb_s), lam_init=lam_init)
        outs[4].append(ckv_s.reshape(bs, ss, MLA_KV_RANK))
        outs[5].append(kpe_s.reshape(bs, ss, MLA_ROPE))
        outs[6].append(dk_s.reshape(bs, ss, DIFF_HEADS, 2 * DIFF_D))
        outs[7].append(dv_s.reshape(bs, ss, DIFF_HEADS, DIFF_V))
        hs = _block_output(hs, mla_o_s, diff_o_s, w, alpha)

    return (hp.reshape(bp, sp, D_MODEL), hs.reshape(bs, ss, D_MODEL)) + tuple(jnp.stack(o) for o in outs)
```

```python
import functools
import math

import jax
import jax.numpy as jnp
from jax import lax
from jax.experimental import pallas as pl
from jax.experimental.pallas import tpu as pltpu

F32 = jnp.float32
BF16 = jnp.bfloat16

D_MODEL = 1024
CHUNK = 64
ROPE_THETA = 500000.0
MLA_HEADS = 8
MLA_NOPE = 64
MLA_ROPE = 32
MLA_V = 64
MLA_Q_RANK = 256
MLA_KV_RANK = 256
MLA_SCALE = (MLA_NOPE + MLA_ROPE) ** -0.5
DIFF_HEADS = 4
DIFF_D = 64
DIFF_V = 2 * DIFF_D
DIFF_ROT = DIFF_D // 4
DIFF_SCALE = DIFF_D ** -0.5
N_GROUPS = 4
EXP_PER_GROUP = 8
N_EXPERTS = N_GROUPS * EXP_PER_GROUP
D_EXPERT = 256
EPS_LN = 1e-5
EPS_RMS = 1e-6
LOG2E = math.log2(math.e)

LANES = 128
HEAD_PAD = LANES
ATTN_TILE = 256
BF16_ROWS = 16
MLA_VA = MLA_V + BF16_ROWS
DIFF_VA = DIFF_V + BF16_ROWS
VMEM_LIMIT = 56 * 1024 * 1024

_CQ, _CKV, _DK, _DV, _KR, _IN_COLS_R = 0, 256, 512, 1024, 1536, 1664


def _cparams(sem):
    return pltpu.CompilerParams(dimension_semantics=sem, vmem_limit_bytes=VMEM_LIMIT)


def _rms(x, g):
    return x * lax.rsqrt(jnp.mean(x * x, axis=-1, keepdims=True) + EPS_RMS) * g


def _layer_norm(x, g, b):
    mu = jnp.mean(x, axis=-1, keepdims=True)
    xc = x - mu
    var = jnp.mean(xc * xc, axis=-1, keepdims=True)
    return xc * lax.rsqrt(var + EPS_LN) * g + b


def _nt_dot(a, b):
    return lax.dot_general(a, b, (((1,), (1,)), ((), ())), preferred_element_type=F32)


def _rope_coeffs(cos_t, sin_t, lo, half, period):
    lane = lax.broadcasted_iota(jnp.int32, cos_t.shape, 1) & (period - 1)
    is1 = (lane >= lo) & (lane < lo + half)
    is2 = (lane >= lo + half) & (lane < lo + 2 * half)
    c = jnp.where(is1 | is2, cos_t, 1.0)
    a = jnp.where(is1, -sin_t, 0.0)
    b = jnp.where(is2, sin_t, 0.0)
    return c, a, b


def _rope_apply(x, coeffs, half):
    c, a, b = coeffs
    return x * c + pltpu.roll(x, LANES - half, 1) * a + pltpu.roll(x, half, 1) * b


def _rope_rows(x, cos_t, sin_t):
    half = cos_t.shape[0]
    x1, x2 = x[:half], x[half:]
    return x1 * cos_t - x2 * sin_t, x1 * sin_t + x2 * cos_t


def _proj_kernel(x_ref, w_in_ref, w_dqt_ref, qn_ref, w_uqt_ref, kvn_ref, w_ukv_ref, w_uvt_ref, e_ref,
                 cm_ref, sm_ref, cd_ref, sd_ref, cmt_ref, smt_ref, cdt_ref, sdt_ref,
                 ckv_ref, kpe_ref, dk_ref, dv_ref, qt_ref, km_ref, vm_ref, vt_ref,
                 dqt_ref, dkb_ref, dvb_ref, dvt_ref):
    xb = x_ref[...].astype(BF16)
    tm = xb.shape[0]
    z = jnp.dot(xb, w_in_ref[...], preferred_element_type=F32)
    half_m = MLA_ROPE // 2
    half_d = DIFF_ROT // 2
    coef_k = _rope_coeffs(cm_ref[...], sm_ref[...], 0, half_m, LANES)
    coef_d = _rope_coeffs(cd_ref[...], sd_ref[...], 0, half_d, DIFF_D)

    cq = _rms(z[:, _CQ:_CQ + MLA_Q_RANK], qn_ref[...])
    qt = _nt_dot(w_uqt_ref[...], cq.astype(BF16)) * (MLA_SCALE * LOG2E)
    cos_mt, sin_mt = cmt_ref[...], smt_ref[...]
    for h in range(MLA_HEADS):
        r0 = h * HEAD_PAD
        qt_ref[0, r0:r0 + MLA_NOPE, :] = qt[r0:r0 + MLA_NOPE].astype(BF16)
        o1, o2 = _rope_rows(qt[r0 + MLA_NOPE:r0 + MLA_NOPE + MLA_ROPE], cos_mt, sin_mt)
        qt_ref[0, r0 + MLA_NOPE:r0 + MLA_NOPE + half_m, :] = o1.astype(BF16)
        qt_ref[0, r0 + MLA_NOPE + half_m:r0 + MLA_NOPE + MLA_ROPE, :] = o2.astype(BF16)
        qt_ref[0, r0 + MLA_NOPE + MLA_ROPE:r0 + HEAD_PAD, :] = jnp.zeros((HEAD_PAD - MLA_NOPE - MLA_ROPE, tm), BF16)

    ckv = _rms(z[:, _CKV:_CKV + MLA_KV_RANK], kvn_ref[...])
    ckv_ref[...] = ckv
    ckv_b = ckv.astype(BF16)
    kslab = _rope_apply(z[:, _KR:_KR + LANES], coef_k, half_m)
    kpe_ref[...] = kslab[:, :MLA_ROPE]
    kv = jnp.dot(ckv_b, w_ukv_ref[...], preferred_element_type=F32)
    k_full = kv[:, :MLA_HEADS * HEAD_PAD] + jnp.dot(kslab.astype(BF16), e_ref[...], preferred_element_type=F32)
    km_ref[...] = k_full.astype(BF16)
    vm_ref[...] = kv[:, MLA_HEADS * HEAD_PAD:].astype(BF16)
    ones_rows = jnp.where(lax.broadcasted_iota(jnp.int32, (BF16_ROWS, tm), 0) == 0, 1.0, 0.0).astype(BF16)
    vt = _nt_dot(w_uvt_ref[...], ckv_b)
    for h in range(MLA_HEADS):
        vt_ref[0, h * MLA_VA:h * MLA_VA + MLA_V, :] = vt[h * MLA_V:(h + 1) * MLA_V].astype(BF16)
        vt_ref[0, h * MLA_VA + MLA_V:(h + 1) * MLA_VA, :] = ones_rows

    dqt = _nt_dot(w_dqt_ref[...], xb) * (DIFF_SCALE * LOG2E)
    cos_dt, sin_dt = cdt_ref[...], sdt_ref[...]
    zeros = jnp.zeros((DIFF_D, tm), BF16)
    for n in range(2 * DIFF_HEADS):
        r0 = n * DIFF_D
        o1, o2 = _rope_rows(dqt[r0:r0 + DIFF_ROT], cos_dt, sin_dt)
        qn = jnp.concatenate([o1, o2, dqt[r0 + DIFF_ROT:r0 + DIFF_D]], axis=0).astype(BF16)
        lo, hi = (qn, zeros) if n % 2 == 0 else (zeros, qn)
        dqt_ref[0, n * LANES:n * LANES + DIFF_D, :] = lo
        dqt_ref[0, n * LANES + DIFF_D:(n + 1) * LANES, :] = hi

    dv = z[:, _DV:_DV + DIFF_HEADS * DIFF_V]
    dvb_ref[...] = dv.astype(BF16)
    dvt = dv.T
    for h in range(DIFF_HEADS):
        sl = slice(h * LANES, (h + 1) * LANES)
        dk = _rope_apply(z[:, _DK + h * LANES:_DK + (h + 1) * LANES], coef_d, half_d)
        dk_ref[:, h, :] = dk
        dkb_ref[:, sl] = dk.astype(BF16)
        dv_ref[:, h, :] = dv[:, sl]
        dvt_ref[0, h * DIFF_VA:h * DIFF_VA + DIFF_V, :] = dvt[sl].astype(BF16)
        dvt_ref[0, h * DIFF_VA + DIFF_V:(h + 1) * DIFF_VA, :] = ones_rows


def _proj(x, tabs, tabs_t, w, *, tm):
    t = x.shape[0]
    n_pos_blocks = tabs[0].shape[0] // tm
    row = lambda n: pl.BlockSpec((tm, n), lambda i: (i, 0))
    full = lambda a: pl.BlockSpec(a.shape, lambda i: (0,) * a.ndim)
    tab = pl.BlockSpec((tm, LANES), lambda i: (i % n_pos_blocks, 0))
    tab_t = lambda a: pl.BlockSpec((a.shape[0], tm), lambda i: (0, i % n_pos_blocks))
    tr = lambda n: pl.BlockSpec((1, n, tm), lambda i: (i, 0, 0))
    hd = lambda n: pl.BlockSpec((tm, DIFF_HEADS, n), lambda i: (i, 0, 0))
    weights = (w['w_in'], w['w_dqt'], w['q_norm'], w['w_uqt'], w['kv_norm'], w['w_ukv'], w['w_uvt'], w['e_place'])
    outs = ((row, MLA_KV_RANK, F32), (row, MLA_ROPE, F32), (hd, 2 * DIFF_D, F32), (hd, DIFF_V, F32),
            (tr, 1024, BF16), (row, 1024, BF16), (row, 512, BF16), (tr, MLA_HEADS * MLA_VA, BF16),
            (tr, 1024, BF16), (row, 512, BF16), (row, 512, BF16), (tr, DIFF_HEADS * DIFF_VA, BF16))
    shape = lambda kind, n: {tr: (t // tm, n, tm), hd: (t, DIFF_HEADS, n), row: (t, n)}[kind]
    return pl.pallas_call(
        _proj_kernel,
        grid=(t // tm,),
        in_specs=[row(D_MODEL)] + [full(a) for a in weights] + [tab] * 4 + [tab_t(a) for a in tabs_t],
        out_specs=[kind(n) for kind, n, _ in outs],
        out_shape=[jax.ShapeDtypeStruct(shape(kind, n), d) for kind, n, d in outs],
        compiler_params=_cparams(("parallel",)),
        name="proj",
    )(x, *weights, *tabs, *tabs_t)


def _expand_kernel(ckv_ref, kpe_ref, w_ukv_ref, e_ref, km_ref, vm_ref):
    kv = jnp.dot(ckv_ref[...].astype(BF16), w_ukv_ref[...], preferred_element_type=F32)
    k_full = kv[:, :MLA_HEADS * HEAD_PAD] + jnp.dot(kpe_ref[...].astype(BF16), e_ref[:MLA_ROPE, :],
                                                   preferred_element_type=F32)
    km_ref[...] = k_full.astype(BF16)
    vm_ref[...] = kv[:, MLA_HEADS * HEAD_PAD:].astype(BF16)


def _expand(ckv, kpe, w, *, tm):
    r = ckv.shape[0]
    row = lambda n: pl.BlockSpec((tm, n), lambda i: (i, 0))
    full = lambda a: pl.BlockSpec(a.shape, lambda i: (0,) * a.ndim)
    return pl.pallas_call(
        _expand_kernel,
        grid=(r // tm,),
        in_specs=[row(MLA_KV_RANK), row(MLA_ROPE), full(w['w_ukv']), full(w['e_place'])],
        out_specs=[row(1024), row(512)],
        out_shape=[jax.ShapeDtypeStruct((r, 1024), BF16), jax.ShapeDtypeStruct((r, 512), BF16)],
        compiler_params=_cparams(("parallel",)),
        name="expand",
    )(ckv, kpe, w['w_ukv'], w['e_place'])


def _chunk_mask_t(t):
    shift = CHUNK.bit_length() - 1
    kc = lax.broadcasted_iota(jnp.int32, (t, t), 0) >> shift
    qc = lax.broadcasted_iota(jnp.int32, (t, t), 1) >> shift
    return kc <= qc


def _flash_scratch(t, dv, n):
    return [pltpu.VMEM((n, t, t), F32), pltpu.VMEM((n, t, t), F32),
            pltpu.VMEM((n, 1, t), F32), pltpu.VMEM((n, dv, t), F32)]


def _flash_loop(i, score, value, n, s_a, s_b, m_ref, acc_ref):
    t = s_a.shape[-1]
    mask = _chunk_mask_t(t)
    m_ref[...] = jnp.full(m_ref.shape, -jnp.inf, F32)
    acc_ref[...] = jnp.zeros(acc_ref.shape, F32)

    def step(j, s_cur, s_nxt, masked):
        if s_nxt is not None:
            for k in range(n):
                s_nxt[k] = score(j + 1, k)
        for k in range(n):
            s = s_cur[k]
            if masked:
                s = jnp.where(mask, s, -jnp.inf)
            m = m_ref[k]
            m_new = jnp.maximum(m, jnp.max(s, axis=0, keepdims=True))
            alpha = jnp.exp2(m - m_new)
            p = jnp.exp2(s - m_new)
            m_ref[k] = m_new
            acc_ref[k] = alpha * acc_ref[k] + jnp.dot(value(j, k), p.astype(BF16), preferred_element_type=F32)

    for k in range(n):
        s_a[k] = score(0, k)

    def pair(jj, carry):
        step(2 * jj, s_a, s_b, False)
        step(2 * jj + 1, s_b, s_a, False)
        return carry

    lax.fori_loop(0, i // 2, pair, 0)

    @pl.when(i % 2 == 0)
    def _():
        step(i, s_a, None, True)

    @pl.when(i % 2 == 1)
    def _():
        step(i - 1, s_a, s_b, False)
        step(i, s_b, None, True)


def _normalised(acc_ref, k, dv):
    return acc_ref[k, :dv, :] / acc_ref[k, dv:dv + 1, :]


def _mla_flash_kernel(qt_ref, k_ref, vt_ref, o_ref, s_a, s_b, m_ref, acc_ref, *, t, heads):
    def score(j, h):
        rows = pl.ds(pl.multiple_of(j * t, t), t)
        return jnp.dot(k_ref[0, rows, h * HEAD_PAD:(h + 1) * HEAD_PAD],
                       qt_ref[0, h * HEAD_PAD:(h + 1) * HEAD_PAD, :], preferred_element_type=F32)

    value = lambda j, h: vt_ref[j, h * MLA_VA:(h + 1) * MLA_VA, :]
    _flash_loop(pl.program_id(2), score, value, heads, s_a, s_b, m_ref, acc_ref)
    for h in range(0, heads, 2):
        pair = jnp.concatenate([_normalised(acc_ref, h, MLA_V), _normalised(acc_ref, h + 1, MLA_V)], axis=0)
        o_ref[0, :, h * MLA_V:(h + 2) * MLA_V] = pair.T.astype(BF16)


def _mla_flash(qt, km, vt, *, t, heads):
    b, s, _ = km.shape
    nk = s // t
    return pl.pallas_call(
        functools.partial(_mla_flash_kernel, t=t, heads=heads),
        grid=(b, MLA_HEADS // heads, nk),
        in_specs=[pl.BlockSpec((1, heads * HEAD_PAD, t), lambda bi, hi, i: (bi * nk + i, hi, 0)),
                  pl.BlockSpec((1, s, heads * HEAD_PAD), lambda bi, hi, i: (bi, 0, hi)),
                  pl.BlockSpec((nk, heads * MLA_VA, t), lambda bi, hi, i: (bi, hi, 0))],
        out_specs=pl.BlockSpec((1, t, heads * MLA_V), lambda bi, hi, i: (bi, i, hi)),
        out_shape=jax.ShapeDtypeStruct((b, s, MLA_HEADS * MLA_V), BF16),
        scratch_shapes=_flash_scratch(t, MLA_VA, heads),
        compiler_params=_cparams(("parallel", "parallel", "arbitrary")),
        name="mla_flash",
    )(qt, km, vt)


def _diff_lambda(lam_ref, lam_init):
    lq1, lk1, lq2, lk2 = (lam_ref[r:r + 1, :] for r in range(4))
    return (jnp.exp(jnp.sum(lq1 * lk1, axis=-1, keepdims=True))
            - jnp.exp(jnp.sum(lq2 * lk2, axis=-1, keepdims=True)) + lam_init)


def _diff_finish(o1, o2, lam, subln, lam_init):
    o = o1 - lam * o2
    return _rms(o, subln) * (1.0 - lam_init)


def _diff_flash_kernel(lam_ref, subln_ref, qt_ref, k_ref, vt_ref, o_ref, s_a, s_b, m_ref, acc_ref,
                       *, t, heads, lam_init):
    def score(j, n):
        rows = pl.ds(pl.multiple_of(j * t, t), t)
        return jnp.dot(k_ref[0, rows, (n // 2) * LANES:(n // 2 + 1) * LANES],
                       qt_ref[0, n * LANES:(n + 1) * LANES, :], preferred_element_type=F32)

    value = lambda j, n: vt_ref[j, (n // 2) * DIFF_VA:(n // 2 + 1) * DIFF_VA, :]
    _flash_loop(pl.program_id(2), score, value, 2 * heads, s_a, s_b, m_ref, acc_ref)
    lam = _diff_lambda(lam_ref, lam_init)
    for h in range(heads):
        o1 = _normalised(acc_ref, 2 * h, DIFF_V).T
        o2 = _normalised(acc_ref, 2 * h + 1, DIFF_V).T
        o = _diff_finish(o1, o2, lam, subln_ref[...], lam_init)
        o_ref[0, :, h * DIFF_V:(h + 1) * DIFF_V] = o.astype(BF16)


def _diff_flash(lam_rows, subln, dqt, dkb, dvt, *, t, heads, lam_init):
    b, s, _ = dkb.shape
    nk = s // t
    full = lambda a: pl.BlockSpec(a.shape, lambda bi, hi, i: (0,) * a.ndim)
    return pl.pallas_call(
        functools.partial(_diff_flash_kernel, t=t, heads=heads, lam_init=lam_init),
        grid=(b, DIFF_HEADS // heads, nk),
        in_specs=[full(lam_rows), full(subln),
                  pl.BlockSpec((1, heads * 2 * LANES, t), lambda bi, hi, i: (bi * nk + i, hi, 0)),
                  pl.BlockSpec((1, s, heads * LANES), lambda bi, hi, i: (bi, 0, hi)),
                  pl.BlockSpec((nk, heads * DIFF_VA, t), lambda bi, hi, i: (bi, hi, 0))],
        out_specs=pl.BlockSpec((1, t, heads * DIFF_V), lambda bi, hi, i: (bi, i, hi)),
        out_shape=jax.ShapeDtypeStruct((b, s, DIFF_HEADS * DIFF_V), BF16),
        scratch_shapes=_flash_scratch(t, DIFF_VA, 2 * heads),
        compiler_params=_cparams(("parallel", "parallel", "arbitrary")),
        name="diff_flash",
    )(lam_rows, subln, dqt, dkb, dvt)


def _two_part_softmax_pv(s_past, s_new, v_past, v_new):
    m = jnp.maximum(jnp.max(s_past, axis=-1, keepdims=True), jnp.max(s_new, axis=-1, keepdims=True))
    p_past = jnp.exp2(s_past - m)
    p_new = jnp.exp2(s_new - m)
    l = jnp.sum(p_past, axis=-1, keepdims=True) + jnp.sum(p_new, axis=-1, keepdims=True)
    acc = (jnp.dot(p_past.astype(BF16), v_past, preferred_element_type=F32)
           + jnp.dot(p_new.astype(BF16), v_new, preferred_element_type=F32))
    return acc / l


def _token_major(qt_ref):
    return qt_ref[0].astype(F32).T.astype(BF16)


def _mla_step_kernel(qt_ref, kp_ref, vp_ref, kn_ref, vn_ref, o_ref):
    q_all = _token_major(qt_ref)
    for h in range(MLA_HEADS):
        ks = slice(h * HEAD_PAD, (h + 1) * HEAD_PAD)
        vs = slice(h * MLA_V, (h + 1) * MLA_V)
        q = q_all[:, ks]
        o = _two_part_softmax_pv(_nt_dot(q, kp_ref[0, :, ks]), _nt_dot(q, kn_ref[0, :, ks]),
                                 vp_ref[0, :, vs], vn_ref[0, :, vs])
        o_ref[0, :, vs] = o.astype(BF16)


def _mla_step(qt, km_past, vm_past, km_new, vm_new):
    b, _, n = qt.shape
    blk = lambda a: pl.BlockSpec((1,) + a.shape[1:], lambda bi: (bi, 0, 0))
    args = (qt, km_past, vm_past, km_new, vm_new)
    return pl.pallas_call(
        _mla_step_kernel,
        grid=(b,),
        in_specs=[blk(a) for a in args],
        out_specs=pl.BlockSpec((1, n, MLA_HEADS * MLA_V), lambda bi: (bi, 0, 0)),
        out_shape=jax.ShapeDtypeStruct((b, n, MLA_HEADS * MLA_V), BF16),
        compiler_params=_cparams(("parallel",)),
        name="mla_step",
    )(*args)


def _diff_step_kernel(lam_ref, subln_ref, qt_ref, kp_ref, vp_ref, kn_ref, vn_ref, o_ref, *, lam_init):
    lam = _diff_lambda(lam_ref, lam_init)
    q = _token_major(qt_ref)
    for h in range(DIFF_HEADS):
        sl = slice(h * LANES, (h + 1) * LANES)
        q1 = q[:, 2 * h * LANES:(2 * h + 1) * LANES]
        q2 = q[:, (2 * h + 1) * LANES:(2 * h + 2) * LANES]
        kp = kp_ref[0, :, h, :].astype(BF16)
        vp = vp_ref[0, :, h, :].astype(BF16)
        kn = kn_ref[0, :, sl]
        vn = vn_ref[0, :, sl]
        o1 = _two_part_softmax_pv(_nt_dot(q1, kp), _nt_dot(q1, kn), vp, vn)
        o2 = _two_part_softmax_pv(_nt_dot(q2, kp), _nt_dot(q2, kn), vp, vn)
        o_ref[0, :, sl] = _diff_finish(o1, o2, lam, subln_ref[...], lam_init).astype(BF16)


def _diff_step(lam_rows, subln, dqt, k_past, v_past, dkb, dvb, *, lam_init):
    b, _, n = dqt.shape
    full = lambda a: pl.BlockSpec(a.shape, lambda bi: (0,) * a.ndim)
    blk = lambda a: pl.BlockSpec((1,) + a.shape[1:], lambda bi: (bi,) + (0,) * (a.ndim - 1))
    args = (dqt, k_past, v_past, dkb, dvb)
    return pl.pallas_call(
        functools.partial(_diff_step_kernel, lam_init=lam_init),
        grid=(b,),
        in_specs=[full(lam_rows), full(subln)] + [blk(a) for a in args],
        out_specs=pl.BlockSpec((1, n, DIFF_HEADS * DIFF_V), lambda bi: (bi, 0, 0)),
        out_shape=jax.ShapeDtypeStruct((b, n, DIFF_HEADS * DIFF_V), BF16),
        compiler_params=_cparams(("parallel",)),
        name="diff_step",
    )(lam_rows, subln, *args)


def _first_index_of_max(vals, lane):
    m = jnp.max(vals, axis=-1, keepdims=True)
    idx = jnp.min(jnp.where(vals == m, lane, float(LANES)), axis=-1, keepdims=True)
    return m, idx


_SLAB_MEMBER = EXP_PER_GROUP
_SLAB_RANK = EXP_PER_GROUP + 1


def _post_attn_kernel(x_ref, mla_ref, diff_ref, wo_a_ref, wo_b_ref, g_ref, b_ref, wr_ref, br_ref, ltri_ref, utri_ref,
                      h_ref, hb_ref, slab_ref, rank_ref, cnt_ref, *, alpha):
    a = (jnp.dot(mla_ref[...], wo_a_ref[...], preferred_element_type=F32)
         + jnp.dot(diff_ref[...], wo_b_ref[...], preferred_element_type=F32))
    h = _layer_norm(alpha * x_ref[...] + a, g_ref[...], b_ref[...])
    h_ref[...] = h
    hb = h.astype(BF16)
    hb_ref[...] = hb

    logit = jnp.dot(hb, wr_ref[...], preferred_element_type=F32) + br_ref[...]
    lane_i = lax.broadcasted_iota(jnp.int32, logit.shape, 1)
    lane = lane_i.astype(F32)
    group_of_lane = (lane_i >> (EXP_PER_GROUP.bit_length() - 1)).astype(F32)
    neg = -jnp.inf
    g_logit = jnp.where((lane_i >= N_EXPERTS) & (lane_i < N_EXPERTS + N_GROUPS), logit, neg)
    g_max, g_lane = _first_index_of_max(g_logit, lane)
    g_w = 1.0 / jnp.sum(jnp.exp(g_logit - g_max), axis=-1, keepdims=True)
    g_idx = g_lane - float(N_EXPERTS)
    e_logit = jnp.where(group_of_lane == g_idx, logit, neg)
    m1, i1 = _first_index_of_max(e_logit, lane)
    m2, i2 = _first_index_of_max(jnp.where(lane == i1, neg, e_logit), lane)
    r = jnp.exp(m2 - m1)
    p1 = 1.0 / (1.0 + r)
    p2 = r / (1.0 + r)
    gate = g_w * (jnp.where(lane == i1, p1, 0.0) + jnp.where(lane == i2, p2, 0.0))

    onehot = jnp.where(lane == g_idx, 1.0, 0.0)
    rank_c = jnp.dot(ltri_ref[...], onehot.astype(BF16), preferred_element_type=F32)
    onehot_t = onehot.T[:BF16_ROWS]
    rank_r = jnp.dot(onehot_t.astype(BF16), utri_ref[...], preferred_element_type=F32)
    rank_ref[0] = jnp.where(onehot_t > 0.0, rank_r, -1.0)
    cnt_ref[0] = jnp.broadcast_to(jnp.sum(onehot, axis=0, keepdims=True), cnt_ref.shape[1:])

    for g in range(N_GROUPS):
        member = g_idx == float(g)
        rank_g = jnp.sum(jnp.where(lane_i == g, rank_c, 0.0), axis=-1, keepdims=True)
        rolled = gate if g == 0 else pltpu.roll(gate, LANES - EXP_PER_GROUP * g, 1)
        extra = jnp.where(lane_i == _SLAB_MEMBER, jnp.where(member, 1.0, 0.0),
                          jnp.where(lane_i == _SLAB_RANK, jnp.where(member, rank_g, -1.0), 0.0))
        slab_ref[:, g * LANES:(g + 1) * LANES] = jnp.where(lane_i < EXP_PER_GROUP, rolled, extra)


def _post_attn(x, mla_o, diff_o, w, *, tm, alpha):
    t = x.shape[0]
    nt = t // tm
    row = lambda n: pl.BlockSpec((tm, n), lambda i: (i, 0))
    full = lambda a: pl.BlockSpec(a.shape, lambda i: (0,) * a.ndim)
    per_tile = lambda r, c: pl.BlockSpec((1, r, c), lambda i: (i, 0, 0))
    idx = jnp.arange(tm)
    ltri = (idx[None, :] < idx[:, None]).astype(BF16)
    weights = (w['w_out_a'], w['w_out_b'], w['ln1_g'], w['ln1_b'], w['w_route'], w['b_route'], ltri, ltri.T)
    return pl.pallas_call(
        functools.partial(_post_attn_kernel, alpha=alpha),
        grid=(nt,),
        in_specs=[row(D_MODEL), row(512), row(512)] + [full(a) for a in weights],
        out_specs=[row(D_MODEL), row(D_MODEL), row(N_GROUPS * LANES), per_tile(BF16_ROWS, tm), per_tile(8, LANES)],
        out_shape=[jax.ShapeDtypeStruct((t, D_MODEL), F32), jax.ShapeDtypeStruct((t, D_MODEL), BF16),
                   jax.ShapeDtypeStruct((t, N_GROUPS * LANES), F32),
                   jax.ShapeDtypeStruct((nt, BF16_ROWS, tm), F32), jax.ShapeDtypeStruct((nt, 8, LANES), F32)],
        compiler_params=_cparams(("parallel",)),
        name="post_attn",
    )(x, mla_o, diff_o, *weights)


MOE_TILE = 1024
MOE_ROWS = 256


def _split3(x):
    hi = x.astype(BF16)
    r1 = x - hi.astype(F32)
    mid = r1.astype(BF16)
    lo = (r1 - mid.astype(F32)).astype(BF16)
    return hi, mid, lo


def _moe_kernel(cnt_ref, hb_ref, slab_ref, rank_ref, wg_ref, wu_ref, wd_ref, yin_ref, g_ref, b_ref, y_ref,
                *, alpha, resident):
    g = pl.program_id(0)
    i = pl.program_id(1)
    n_tok = cnt_ref[i, g]
    slab = slab_ref[...]
    rank_col = slab[:, _SLAB_RANK:_SLAB_RANK + 1]
    rank_row = rank_ref[0, pl.ds(g, 1), :]
    gate3 = _split3(slab)
    sub = lax.broadcasted_iota(jnp.int32, (MOE_ROWS, 1), 0).astype(F32)
    lan = lax.broadcasted_iota(jnp.int32, (1, MOE_ROWS), 1).astype(F32)

    if resident:
        @pl.when(g == 0)
        def _():
            y_ref[...] = alpha * yin_ref[...]
    else:
        y_ref[...] = jnp.where(g == 0, alpha, 1.0) * yin_ref[...]

    def one_pass(c, carry):
        base = (c * MOE_ROWS).astype(F32)
        sel = jnp.where(rank_row - base == sub, 1.0, 0.0).astype(BF16)
        sel_t = jnp.where(rank_col - base == lan, 1.0, 0.0).astype(BF16)
        x = jnp.dot(sel, hb_ref[...], preferred_element_type=F32).astype(BF16)
        gate_c = sum(jnp.dot(sel, part, preferred_element_type=F32) for part in gate3)
        ys = jnp.zeros((MOE_ROWS, D_MODEL), F32)
        for e in range(EXP_PER_GROUP):
            hid = (jax.nn.silu(jnp.dot(x, wg_ref[e], preferred_element_type=F32))
                   * jnp.dot(x, wu_ref[e], preferred_element_type=F32))
            ys = ys + jnp.dot((hid * gate_c[:, e:e + 1]).astype(BF16), wd_ref[e], preferred_element_type=F32)
        ys_hi = ys.astype(BF16)
        ys_lo = (ys - ys_hi.astype(F32)).astype(BF16)
        y_ref[...] += (jnp.dot(sel_t, ys_hi, preferred_element_type=F32)
                       + jnp.dot(sel_t, ys_lo, preferred_element_type=F32))
        return carry

    lax.fori_loop(0, (n_tok + MOE_ROWS - 1) // MOE_ROWS, one_pass, 0)

    @pl.when(g == N_GROUPS - 1)
    def _():
        y_ref[...] = _layer_norm(y_ref[...], g_ref[...], b_ref[...])


def _moe(h, hb, slab, rank_r, cnt, w, *, tm, alpha):
    t = h.shape[0]
    nt = t // tm
    row = lambda n: pl.BlockSpec((tm, n), lambda g, i, c: (i, 0))
    full = lambda a: pl.BlockSpec(a.shape, lambda g, i, c: (0,) * a.ndim)
    grp = lambda a: pl.BlockSpec((EXP_PER_GROUP,) + a.shape[1:], lambda g, i, c: (g, 0, 0),
                                 pipeline_mode=pl.Buffered(1))
    grid_spec = pltpu.PrefetchScalarGridSpec(
        num_scalar_prefetch=1,
        grid=(N_GROUPS, nt),
        in_specs=[row(D_MODEL), pl.BlockSpec((tm, LANES), lambda g, i, c: (i, g)),
                  pl.BlockSpec((1, BF16_ROWS, tm), lambda g, i, c: (i, 0, 0)),
                  grp(w['w_gate']), grp(w['w_up']), grp(w['w_down']), row(D_MODEL),
                  full(w['ln2_g']), full(w['ln2_b'])],
        out_specs=row(D_MODEL))
    return pl.pallas_call(
        functools.partial(_moe_kernel, alpha=alpha, resident=(nt == 1)),
        grid_spec=grid_spec,
        out_shape=jax.ShapeDtypeStruct((t, D_MODEL), F32),
        input_output_aliases={7: 0},
        compiler_params=_cparams(("arbitrary", "arbitrary")),
        name="moe",
    )(cnt, hb, slab, rank_r, w['w_gate'], w['w_up'], w['w_down'], h, w['ln2_g'], w['ln2_b'])


def _rope_tables(pos):
    pos = pos.astype(F32)[:, None]
    lane = jnp.arange(LANES)
    inv_m = ROPE_THETA ** (-jnp.arange(0, MLA_ROPE, 2, dtype=F32) / MLA_ROPE)
    ang_m = pos * inv_m
    inv_d = ROPE_THETA ** (-jnp.arange(0, DIFF_ROT, 2, dtype=F32) / DIFF_ROT)
    ang_d = pos * inv_d
    narrow = lax.optimization_barrier((jnp.cos(ang_m), jnp.sin(ang_m), jnp.cos(ang_d), jnp.sin(ang_d)))
    tabs = tuple(a[:, lane % a.shape[1]] for a in narrow)
    tabs_t = tuple(a.T for a in narrow)
    return tabs, tabs_t


def _prep_weights(l, w_in, mla_q_norm, mla_w_uq, mla_kv_norm, mla_w_ukv, diff_subln, w_out, ln1_g, ln1_b,
                  ln2_g, ln2_b, w_route_group, b_route_group, w_route_expert, b_route_expert,
                  w_exp_gate, w_exp_up, w_exp_down, lam_rows):
    wi = w_in[l]
    s0, s1, s2, s3, s4 = 256, 512, 544, 1056, 1568
    w_in_r = jnp.concatenate([wi[:, :s0], wi[:, s0:s1], wi[:, s3:s4], wi[:, s4:],
                              wi[:, s1:s2], jnp.zeros((D_MODEL, LANES - MLA_ROPE), F32)], axis=1)
    w_uq = jnp.pad(mla_w_uq[l].reshape(MLA_Q_RANK, MLA_HEADS, MLA_NOPE + MLA_ROPE),
                   ((0, 0), (0, 0), (0, HEAD_PAD - MLA_NOPE - MLA_ROPE))).reshape(MLA_Q_RANK, -1)
    ukv = mla_w_ukv[l].reshape(MLA_KV_RANK, MLA_HEADS, MLA_NOPE + MLA_V)
    w_uk = jnp.pad(ukv[..., :MLA_NOPE], ((0, 0), (0, 0), (0, HEAD_PAD - MLA_NOPE))).reshape(MLA_KV_RANK, -1)
    w_uv = ukv[..., MLA_NOPE:].reshape(MLA_KV_RANK, -1)
    rows = jnp.arange(LANES)[:, None]
    cols = jnp.arange(MLA_HEADS * HEAD_PAD)[None, :]
    e_place = ((rows < MLA_ROPE) & (cols % HEAD_PAD == rows + MLA_NOPE)).astype(BF16)
    w_route = jnp.concatenate([w_route_expert[l], w_route_group[l],
                               jnp.zeros((D_MODEL, LANES - N_EXPERTS - N_GROUPS), F32)], axis=1)
    b_route = jnp.concatenate([b_route_expert[l], b_route_group[l],
                               jnp.zeros((LANES - N_EXPERTS - N_GROUPS,), F32)])[None, :]
    return dict(
        w_in=w_in_r.astype(BF16), w_dqt=wi[:, s2:s3].T.astype(BF16),
        q_norm=mla_q_norm[l][None, :], w_uqt=w_uq.T.astype(BF16),
        kv_norm=mla_kv_norm[l][None, :], w_ukv=jnp.concatenate([w_uk, w_uv], axis=1).astype(BF16),
        w_uvt=w_uv.T.astype(BF16),
        e_place=e_place, subln=diff_subln[l][None, :], lam_rows=lam_rows,
        w_out_a=w_out[l][:MLA_HEADS * MLA_V].astype(BF16), w_out_b=w_out[l][MLA_HEADS * MLA_V:].astype(BF16),
        ln1_g=ln1_g[l][None, :], ln1_b=ln1_b[l][None, :], ln2_g=ln2_g[l][None, :], ln2_b=ln2_b[l][None, :],
        w_route=w_route.astype(BF16), b_route=b_route,
        w_gate=w_exp_gate[l].astype(BF16), w_up=w_exp_up[l].astype(BF16), w_down=w_exp_down[l].astype(BF16))


def _block_output(x, mla_o, diff_o, w, alpha):
    t = x.shape[0]
    tm = min(MOE_TILE, t)
    h, hb, slab, rank_r, cnt = _post_attn(x, mla_o.reshape(t, -1), diff_o.reshape(t, -1), w, tm=tm, alpha=alpha)
    cnt = cnt[:, 0, :N_GROUPS].astype(jnp.int32)
    return _moe(h, hb, slab, rank_r, cnt, w, tm=tm, alpha=alpha)


def kernel(x_prompt, x_sample, cache_mla_ckv, cache_mla_kpe, cache_diff_k, cache_diff_v, w_in, mla_q_norm, mla_w_uq, mla_kv_norm, mla_w_ukv, diff_lambda_q1, diff_lambda_k1, diff_lambda_q2, diff_lambda_k2, diff_subln, w_out, ln1_g, ln1_b, ln2_g, ln2_b, w_route_group, b_route_group, w_route_expert, b_route_expert, w_exp_gate, w_exp_up, w_exp_down):
    depth = w_in.shape[0]
    bp, sp, _ = x_prompt.shape
    bs, ss, _ = x_sample.shape
    past = cache_mla_ckv.shape[2]
    alpha = (2.0 * depth) ** 0.25
    tabs_p, tabs_pt = _rope_tables(jnp.arange(sp))
    tabs_s, tabs_st = _rope_tables(past + jnp.arange(ss))
    hp = x_prompt.reshape(bp * sp, D_MODEL)
    hs = x_sample.reshape(bs * ss, D_MODEL)
    outs = [[] for _ in range(8)]
    for l in range(depth):
        lam_init = 0.8 - 0.6 * math.exp(-0.3 * l)
        lam_rows = jnp.stack([diff_lambda_q1[l], diff_lambda_k1[l], diff_lambda_q2[l], diff_lambda_k2[l]])
        w = _prep_weights(l, w_in, mla_q_norm, mla_w_uq, mla_kv_norm, mla_w_ukv, diff_subln, w_out,
                          ln1_g, ln1_b, ln2_g, ln2_b, w_route_group, b_route_group, w_route_expert,
                          b_route_expert, w_exp_gate, w_exp_up, w_exp_down, lam_rows)

        ckv, kpe, dk, dv, qt, km, _, vt, dqt, dkb, _, dvt = _proj(hp, tabs_p, tabs_pt, w, tm=ATTN_TILE)
        b3 = lambda a: a.reshape(bp, sp, -1)
        mla_o = _mla_flash(qt, b3(km), vt, t=ATTN_TILE, heads=4)
        diff_o = _diff_flash(w['lam_rows'], w['subln'], dqt, b3(dkb), dvt, t=ATTN_TILE, heads=2,
                             lam_init=lam_init)
        outs[0].append(ckv.reshape(bp, sp, MLA_KV_RANK))
        outs[1].append(kpe.reshape(bp, sp, MLA_ROPE))
        outs[2].append(dk.reshape(bp, sp, DIFF_HEADS, 2 * DIFF_D))
        outs[3].append(dv.reshape(bp, sp, DIFF_HEADS, DIFF_V))
        hp = _block_output(hp, mla_o, diff_o, w, alpha)

        ckv_s, kpe_s, dk_s, dv_s, qt_s, km_s, vm_s, _, dqt_s, dkb_s, dvb_s, _ = _proj(hs, tabs_s, tabs_st, w, tm=ss)
        km_past, vm_past = _expand(cache_mla_ckv[l].reshape(bs * past, MLA_KV_RANK),
                                   cache_mla_kpe[l].reshape(bs * past, MLA_ROPE), w, tm=512)
        s3 = lambda a: a.reshape(bs, ss, -1)
        p3 = lambda a: a.reshape(bs, past, -1)
        mla_o_s = _mla_step(qt_s, p3(km_past), p3(vm_past), s3(km_s), s3(vm_s))
        diff_o_s = _diff_step(w['lam_rows'], w['subln'], dqt_s, cache_diff_k[l], cache_diff_v[l],
                              s3(dkb_s), s3(dvb_s), lam_init=lam_init)
        outs[4].append(ckv_s.reshape(bs, ss, MLA_KV_RANK))
        outs[5].append(kpe_s.reshape(bs, ss, MLA_ROPE))
        outs[6].append(dk_s.reshape(bs, ss, DIFF_HEADS, 2 * DIFF_D))
        outs[7].append(dv_s.reshape(bs, ss, DIFF_HEADS, DIFF_V))
        hs = _block_output(hs, mla_o_s, diff_o_s, w, alpha)

    return (hp.reshape(bp, sp, D_MODEL), hs.reshape(bs, ss, D_MODEL)) + tuple(jnp.stack(o) for o in outs)
```

```python
import functools
import math

import jax
import jax.numpy as jnp
from jax import lax
from jax.experimental import pallas as pl
from jax.experimental.pallas import tpu as pltpu

F32 = jnp.float32
BF16 = jnp.bfloat16

D_MODEL = 1024
CHUNK = 64
ROPE_THETA = 500000.0
MLA_HEADS = 8
MLA_NOPE = 64
MLA_ROPE = 32
MLA_V = 64
MLA_Q_RANK = 256
MLA_KV_RANK = 256
MLA_SCALE = (MLA_NOPE + MLA_ROPE) ** -0.5
DIFF_HEADS = 4
DIFF_D = 64
DIFF_V = 2 * DIFF_D
DIFF_ROT = DIFF_D // 4
DIFF_SCALE = DIFF_D ** -0.5
N_GROUPS = 4
EXP_PER_GROUP = 8
N_EXPERTS = N_GROUPS * EXP_PER_GROUP
D_EXPERT = 256
EPS_LN = 1e-5
EPS_RMS = 1e-6
LOG2E = math.log2(math.e)

LANES = 128
HEAD_PAD = LANES
ATTN_TILE = 256
BF16_ROWS = 16
MLA_VA = MLA_V + BF16_ROWS
DIFF_VA = DIFF_V + BF16_ROWS
VMEM_LIMIT = 56 * 1024 * 1024

_CQ, _CKV, _DK, _DV, _KR, _IN_COLS_R = 0, 256, 512, 1024, 1536, 1664


def _cparams(sem):
    return pltpu.CompilerParams(dimension_semantics=sem, vmem_limit_bytes=VMEM_LIMIT)


def _rms(x, g):
    return x * lax.rsqrt(jnp.mean(x * x, axis=-1, keepdims=True) + EPS_RMS) * g


def _layer_norm(x, g, b):
    mu = jnp.mean(x, axis=-1, keepdims=True)
    xc = x - mu
    var = jnp.mean(xc * xc, axis=-1, keepdims=True)
    return xc * lax.rsqrt(var + EPS_LN) * g + b


def _nt_dot(a, b):
    return lax.dot_general(a, b, (((1,), (1,)), ((), ())), preferred_element_type=F32)


def _rope_coeffs(cos_t, sin_t, lo, half, period):
    lane = lax.broadcasted_iota(jnp.int32, cos_t.shape, 1) & (period - 1)
    is1 = (lane >= lo) & (lane < lo + half)
    is2 = (lane >= lo + half) & (lane < lo + 2 * half)
    c = jnp.where(is1 | is2, cos_t, 1.0)
    a = jnp.where(is1, -sin_t, 0.0)
    b = jnp.where(is2, sin_t, 0.0)
    return c, a, b


def _rope_apply(x, coeffs, half):
    c, a, b = coeffs
    return x * c + pltpu.roll(x, LANES - half, 1) * a + pltpu.roll(x, half, 1) * b


def _rope_rows(x, cos_t, sin_t):
    half = cos_t.shape[0]
    x1, x2 = x[:half], x[half:]
    return x1 * cos_t - x2 * sin_t, x1 * sin_t + x2 * cos_t


def _proj_kernel(x_ref, w_in_ref, w_dqt_ref, qn_ref, w_uqt_ref, kvn_ref, w_ukv_ref, w_uvt_ref, e_ref,
                 cm_ref, sm_ref, cd_ref, sd_ref, cmt_ref, smt_ref, cdt_ref, sdt_ref,
                 ckv_ref, kpe_ref, dk_ref, dv_ref, qt_ref, km_ref, vm_ref, vt_ref,
                 dqt_ref, dkb_ref, dvb_ref, dvt_ref):
    xb = x_ref[...].astype(BF16)
    tm = xb.shape[0]
    z = jnp.dot(xb, w_in_ref[...], preferred_element_type=F32)
    half_m = MLA_ROPE // 2
    half_d = DIFF_ROT // 2
    coef_k = _rope_coeffs(cm_ref[...], sm_ref[...], 0, half_m, LANES)
    coef_d = _rope_coeffs(cd_ref[...], sd_ref[...], 0, half_d, DIFF_D)

    cq = _rms(z[:, _CQ:_CQ + MLA_Q_RANK], qn_ref[...])
    qt = _nt_dot(w_uqt_ref[...], cq.astype(BF16)) * (MLA_SCALE * LOG2E)
    cos_mt, sin_mt = cmt_ref[...], smt_ref[...]
    for h in range(MLA_HEADS):
        r0 = h * HEAD_PAD
        qt_ref[0, r0:r0 + MLA_NOPE, :] = qt[r0:r0 + MLA_NOPE].astype(BF16)
        o1, o2 = _rope_rows(qt[r0 + MLA_NOPE:r0 + MLA_NOPE + MLA_ROPE], cos_mt, sin_mt)
        qt_ref[0, r0 + MLA_NOPE:r0 + MLA_NOPE + half_m, :] = o1.astype(BF16)
        qt_ref[0, r0 + MLA_NOPE + half_m:r0 + MLA_NOPE + MLA_ROPE, :] = o2.astype(BF16)
        qt_ref[0, r0 + MLA_NOPE + MLA_ROPE:r0 + HEAD_PAD, :] = jnp.zeros((HEAD_PAD - MLA_NOPE - MLA_ROPE, tm), BF16)

    ckv = _rms(z[:, _CKV:_CKV + MLA_KV_RANK], kvn_ref[...])
    ckv_ref[...] = ckv
    ckv_b = ckv.astype(BF16)
    kslab = _rope_apply(z[:, _KR:_KR + LANES], coef_k, half_m)
    kpe_ref[...] = kslab[:, :MLA_ROPE]
    kv = jnp.dot(ckv_b, w_ukv_ref[...], preferred_element_type=F32)
    k_full = kv[:, :MLA_HEADS * HEAD_PAD] + jnp.dot(kslab.astype(BF16), e_ref[...], preferred_element_type=F32)
    km_ref[...] = k_full.astype(BF16)
    vm_ref[...] = kv[:, MLA_HEADS * HEAD_PAD:].astype(BF16)
    ones_rows = jnp.where(lax.broadcasted_iota(jnp.int32, (BF16_ROWS, tm), 0) == 0, 1.0, 0.0).astype(BF16)
    vt = _nt_dot(w_uvt_ref[...], ckv_b)
    for h in range(MLA_HEADS):
        vt_ref[0, h * MLA_VA:h * MLA_VA + MLA_V, :] = vt[h * MLA_V:(h + 1) * MLA_V].astype(BF16)
        vt_ref[0, h * MLA_VA + MLA_V:(h + 1) * MLA_VA, :] = ones_rows

    dqt = _nt_dot(w_dqt_ref[...], xb) * (DIFF_SCALE * LOG2E)
    cos_dt, sin_dt = cdt_ref[...], sdt_ref[...]
    zeros = jnp.zeros((DIFF_D, tm), BF16)
    for n in range(2 * DIFF_HEADS):
        r0 = n * DIFF_D
        o1, o2 = _rope_rows(dqt[r0:r0 + DIFF_ROT], cos_dt, sin_dt)
        qn = jnp.concatenate([o1, o2, dqt[r0 + DIFF_ROT:r0 + DIFF_D]], axis=0).astype(BF16)
        lo, hi = (qn, zeros) if n % 2 == 0 else (zeros, qn)
        dqt_ref[0, n * LANES:n * LANES + DIFF_D, :] = lo
        dqt_ref[0, n * LANES + DIFF_D:(n + 1) * LANES, :] = hi

    dv = z[:, _DV:_DV + DIFF_HEADS * DIFF_V]
    dvb_ref[...] = dv.astype(BF16)
    dvt = dv.T
    for h in range(DIFF_HEADS):
        sl = slice(h * LANES, (h + 1) * LANES)
        dk = _rope_apply(z[:, _DK + h * LANES:_DK + (h + 1) * LANES], coef_d, half_d)
        dk_ref[:, h, :] = dk
        dkb_ref[:, sl] = dk.astype(BF16)
        dv_ref[:, h, :] = dv[:, sl]
        dvt_ref[0, h * DIFF_VA:h * DIFF_VA + DIFF_V, :] = dvt[sl].astype(BF16)
        dvt_ref[0, h * DIFF_VA + DIFF_V:(h + 1) * DIFF_VA, :] = ones_rows


def _proj(x, tabs, tabs_t, w, *, tm):
    t = x.shape[0]
    n_pos_blocks = tabs[0].shape[0] // tm
    row = lambda n: pl.BlockSpec((tm, n), lambda i: (i, 0))
    full = lambda a: pl.BlockSpec(a.shape, lambda i: (0,) * a.ndim)
    tab = pl.BlockSpec((tm, LANES), lambda i: (i % n_pos_blocks, 0))
    tab_t = lambda a: pl.BlockSpec((a.shape[0], tm), lambda i: (0, i % n_pos_blocks))
    tr = lambda n: pl.BlockSpec((1, n, tm), lambda i: (i, 0, 0))
    hd = lambda n: pl.BlockSpec((tm, DIFF_HEADS, n), lambda i: (i, 0, 0))
    weights = (w['w_in'], w['w_dqt'], w['q_norm'], w['w_uqt'], w['kv_norm'], w['w_ukv'], w['w_uvt'], w['e_place'])
    outs = ((row, MLA_KV_RANK, F32), (row, MLA_ROPE, F32), (hd, 2 * DIFF_D, F32), (hd, DIFF_V, F32),
            (tr, 1024, BF16), (row, 1024, BF16), (row, 512, BF16), (tr, MLA_HEADS * MLA_VA, BF16),
            (tr, 1024, BF16), (row, 512, BF16), (row, 512, BF16), (tr, DIFF_HEADS * DIFF_VA, BF16))
    shape = lambda kind, n: {tr: (t // tm, n, tm), hd: (t, DIFF_HEADS, n), row: (t, n)}[kind]
    return pl.pallas_call(
        _proj_kernel,
        grid=(t // tm,),
        in_specs=[row(D_MODEL)] + [full(a) for a in weights] + [tab] * 4 + [tab_t(a) for a in tabs_t],
        out_specs=[kind(n) for kind, n, _ in outs],
        out_shape=[jax.ShapeDtypeStruct(shape(kind, n), d) for kind, n, d in outs],
        compiler_params=_cparams(("parallel",)),
        name="proj",
    )(x, *weights, *tabs, *tabs_t)


def _expand_kernel(ckv_ref, kpe_ref, w_ukv_ref, e_ref, km_ref, vm_ref):
    kv = jnp.dot(ckv_ref[...].astype(BF16), w_ukv_ref[...], preferred_element_type=F32)
    k_full = kv[:, :MLA_HEADS * HEAD_PAD] + jnp.dot(kpe_ref[...].astype(BF16), e_ref[:MLA_ROPE, :],
                                                   preferred_element_type=F32)
    km_ref[...] = k_full.astype(BF16)
    vm_ref[...] = kv[:, MLA_HEADS * HEAD_PAD:].astype(BF16)


def _expand(ckv, kpe, w, *, tm):
    r = ckv.shape[0]
    row = lambda n: pl.BlockSpec((tm, n), lambda i: (i, 0))
    full = lambda a: pl.BlockSpec(a.shape, lambda i: (0,) * a.ndim)
    return pl.pallas_call(
        _expand_kernel,
        grid=(r // tm,),
        in_specs=[row(MLA_KV_RANK), row(MLA_ROPE), full(w['w_ukv']), full(w['e_place'])],
        out_specs=[row(1024), row(512)],
        out_shape=[jax.ShapeDtypeStruct((r, 1024), BF16), jax.ShapeDtypeStruct((r, 512), BF16)],
        compiler_params=_cparams(("parallel",)),
        name="expand",
    )(ckv, kpe, w['w_ukv'], w['e_place'])


def _chunk_mask_t(t):
    shift = CHUNK.bit_length() - 1
    kc = lax.broadcasted_iota(jnp.int32, (t, t), 0) >> shift
    qc = lax.broadcasted_iota(jnp.int32, (t, t), 1) >> shift
    return kc <= qc


def _flash_scratch(t, dv, n):
    return [pltpu.VMEM((n, t, t), F32), pltpu.VMEM((n, t, t), F32),
            pltpu.VMEM((n, 1, t), F32), pltpu.VMEM((n, dv, t), F32)]


def _flash_loop(i, score, value, n, s_a, s_b, m_ref, acc_ref):
    t = s_a.shape[-1]
    mask = _chunk_mask_t(t)
    m_ref[...] = jnp.full(m_ref.shape, -jnp.inf, F32)
    acc_ref[...] = jnp.zeros(acc_ref.shape, F32)

    def step(j, s_cur, s_nxt, masked):
        if s_nxt is not None:
            for k in range(n):
                s_nxt[k] = score(j + 1, k)
        for k in range(n):
            s = s_cur[k]
            if masked:
                s = jnp.where(mask, s, -jnp.inf)
            m = m_ref[k]
            m_new = jnp.maximum(m, jnp.max(s, axis=0, keepdims=True))
            alpha = jnp.exp2(m - m_new)
            p = jnp.exp2(s - m_new)
            m_ref[k] = m_new
            acc_ref[k] = alpha * acc_ref[k] + jnp.dot(value(j, k), p.astype(BF16), preferred_element_type=F32)

    for k in range(n):
        s_a[k] = score(0, k)

    def pair(jj, carry):
        step(2 * jj, s_a, s_b, False)
        step(2 * jj + 1, s_b, s_a, False)
        return carry

    lax.fori_loop(0, i // 2, pair, 0)

    @pl.when(i % 2 == 0)
    def _():
        step(i, s_a, None, True)

    @pl.when(i % 2 == 1)
    def _():
        step(i - 1, s_a, s_b, False)
        step(i, s_b, None, True)


def _normalised(acc_ref, k, dv):
    return acc_ref[k, :dv, :] / acc_ref[k, dv:dv + 1, :]


def _mla_flash_kernel(qt_ref, k_ref, vt_ref, o_ref, s_a, s_b, m_ref, acc_ref, *, t, heads):
    def score(j, h):
        rows = pl.ds(pl.multiple_of(j * t, t), t)
        return jnp.dot(k_ref[0, rows, h * HEAD_PAD:(h + 1) * HEAD_PAD],
                       qt_ref[0, h * HEAD_PAD:(h + 1) * HEAD_PAD, :], preferred_element_type=F32)

    value = lambda j, h: vt_ref[j, h * MLA_VA:(h + 1) * MLA_VA, :]
    _flash_loop(pl.program_id(2), score, value, heads, s_a, s_b, m_ref, acc_ref)
    for h in range(0, heads, 2):
        pair = jnp.concatenate([_normalised(acc_ref, h, MLA_V), _normalised(acc_ref, h + 1, MLA_V)], axis=0)
        o_ref[0, :, h * MLA_V:(h + 2) * MLA_V] = pair.T.astype(BF16)


def _mla_flash(qt, km, vt, *, t, heads):
    b, s, _ = km.shape
    nk = s // t
    return pl.pallas_call(
        functools.partial(_mla_flash_kernel, t=t, heads=heads),
        grid=(b, MLA_HEADS // heads, nk),
        in_specs=[pl.BlockSpec((1, heads * HEAD_PAD, t), lambda bi, hi, i: (bi * nk + i, hi, 0)),
                  pl.BlockSpec((1, s, heads * HEAD_PAD), lambda bi, hi, i: (bi, 0, hi)),
                  pl.BlockSpec((nk, heads * MLA_VA, t), lambda bi, hi, i: (bi, hi, 0))],
        out_specs=pl.BlockSpec((1, t, heads * MLA_V), lambda bi, hi, i: (bi, i, hi)),
        out_shape=jax.ShapeDtypeStruct((b, s, MLA_HEADS * MLA_V), BF16),
        scratch_shapes=_flash_scratch(t, MLA_VA, heads),
        compiler_params=_cparams(("parallel", "parallel", "arbitrary")),
        name="mla_flash",
    )(qt, km, vt)


def _diff_lambda(lam_ref, lam_init):
    lq1, lk1, lq2, lk2 = (lam_ref[r:r + 1, :] for r in range(4))
    return (jnp.exp(jnp.sum(lq1 * lk1, axis=-1, keepdims=True))
            - jnp.exp(jnp.sum(lq2 * lk2, axis=-1, keepdims=True)) + lam_init)


def _diff_finish(o1, o2, lam, subln, lam_init):
    o = o1 - lam * o2
    return _rms(o, subln) * (1.0 - lam_init)


def _diff_flash_kernel(lam_ref, subln_ref, qt_ref, k_ref, vt_ref, o_ref, s_a, s_b, m_ref, acc_ref,
                       *, t, heads, lam_init):
    def score(j, n):
        rows = pl.ds(pl.multiple_of(j * t, t), t)
        return jnp.dot(k_ref[0, rows, (n // 2) * LANES:(n // 2 + 1) * LANES],
                       qt_ref[0, n * LANES:(n + 1) * LANES, :], preferred_element_type=F32)

    value = lambda j, n: vt_ref[j, (n // 2) * DIFF_VA:(n // 2 + 1) * DIFF_VA, :]
    _flash_loop(pl.program_id(2), score, value, 2 * heads, s_a, s_b, m_ref, acc_ref)
    lam = _diff_lambda(lam_ref, lam_init)
    for h in range(heads):
        o1 = _normalised(acc_ref, 2 * h, DIFF_V).T
        o2 = _normalised(acc_ref, 2 * h + 1, DIFF_V).T
        o = _diff_finish(o1, o2, lam, subln_ref[...], lam_init)
        o_ref[0, :, h * DIFF_V:(h + 1) * DIFF_V] = o.astype(BF16)


def _diff_flash(lam_rows, subln, dqt, dkb, dvt, *, t, heads, lam_init):
    b, s, _ = dkb.shape
    nk = s // t
    full = lambda a: pl.BlockSpec(a.shape, lambda bi, hi, i: (0,) * a.ndim)
    return pl.pallas_call(
        functools.partial(_diff_flash_kernel, t=t, heads=heads, lam_init=lam_init),
        grid=(b, DIFF_HEADS // heads, nk),
        in_specs=[full(lam_rows), full(subln),
                  pl.BlockSpec((1, heads * 2 * LANES, t), lambda bi, hi, i: (bi * nk + i, hi, 0)),
                  pl.BlockSpec((1, s, heads * LANES), lambda bi, hi, i: (bi, 0, hi)),
                  pl.BlockSpec((nk, heads * DIFF_VA, t), lambda bi, hi, i: (bi, hi, 0))],
        out_specs=pl.BlockSpec((1, t, heads * DIFF_V), lambda bi, hi, i: (bi, i, hi)),
        out_shape=jax.ShapeDtypeStruct((b, s, DIFF_HEADS * DIFF_V), BF16),
        scratch_shapes=_flash_scratch(t, DIFF_VA, 2 * heads),
        compiler_params=_cparams(("parallel", "parallel", "arbitrary")),
        name="diff_flash",
    )(lam_rows, subln, dqt, dkb, dvt)


def _two_part_softmax_pv(s_past, s_new, v_past, v_new):
    m = jnp.maximum(jnp.max(s_past, axis=-1, keepdims=True), jnp.max(s_new, axis=-1, keepdims=True))
    p_past = jnp.exp2(s_past - m)
    p_new = jnp.exp2(s_new - m)
    l = jnp.sum(p_past, axis=-1, keepdims=True) + jnp.sum(p_new, axis=-1, keepdims=True)
    acc = (jnp.dot(p_past.astype(BF16), v_past, preferred_element_type=F32)
           + jnp.dot(p_new.astype(BF16), v_new, preferred_element_type=F32))
    return acc / l


def _token_major(qt_ref):
    return qt_ref[0].astype(F32).T.astype(BF16)


def _mla_step_kernel(qt_ref, kp_ref, vp_ref, kn_ref, vn_ref, o_ref):
    q_all = _token_major(qt_ref)
    for h in range(MLA_HEADS):
        ks = slice(h * HEAD_PAD, (h + 1) * HEAD_PAD)
        vs = slice(h * MLA_V, (h + 1) * MLA_V)
        q = q_all[:, ks]
        o = _two_part_softmax_pv(_nt_dot(q, kp_ref[0, :, ks]), _nt_dot(q, kn_ref[0, :, ks]),
                                 vp_ref[0, :, vs], vn_ref[0, :, vs])
        o_ref[0, :, vs] = o.astype(BF16)


def _mla_step(qt, km_past, vm_past, km_new, vm_new):
    b, _, n = qt.shape
    blk = lambda a: pl.BlockSpec((1,) + a.shape[1:], lambda bi: (bi, 0, 0))
    args = (qt, km_past, vm_past, km_new, vm_new)
    return pl.pallas_call(
        _mla_step_kernel,
        grid=(b,),
        in_specs=[blk(a) for a in args],
        out_specs=pl.BlockSpec((1, n, MLA_HEADS * MLA_V), lambda bi: (bi, 0, 0)),
        out_shape=jax.ShapeDtypeStruct((b, n, MLA_HEADS * MLA_V), BF16),
        compiler_params=_cparams(("parallel",)),
        name="mla_step",
    )(*args)


def _diff_step_kernel(lam_ref, subln_ref, qt_ref, kp_ref, vp_ref, kn_ref, vn_ref, o_ref, *, lam_init):
    lam = _diff_lambda(lam_ref, lam_init)
    q = _token_major(qt_ref)
    for h in range(DIFF_HEADS):
        sl = slice(h * LANES, (h + 1) * LANES)
        q1 = q[:, 2 * h * LANES:(2 * h + 1) * LANES]
        q2 = q[:, (2 * h + 1) * LANES:(2 * h + 2) * LANES]
        kp = kp_ref[0, :, h, :].astype(BF16)
        vp = vp_ref[0, :, h, :].astype(BF16)
        kn = kn_ref[0, :, sl]
        vn = vn_ref[0, :, sl]
        o1 = _two_part_softmax_pv(_nt_dot(q1, kp), _nt_dot(q1, kn), vp, vn)
        o2 = _two_part_softmax_pv(_nt_dot(q2, kp), _nt_dot(q2, kn), vp, vn)
        o_ref[0, :, sl] = _diff_finish(o1, o2, lam, subln_ref[...], lam_init).astype(BF16)


def _diff_step(lam_rows, subln, dqt, k_past, v_past, dkb, dvb, *, lam_init):
    b, _, n = dqt.shape
    full = lambda a: pl.BlockSpec(a.shape, lambda bi: (0,) * a.ndim)
    blk = lambda a: pl.BlockSpec((1,) + a.shape[1:], lambda bi: (bi,) + (0,) * (a.ndim - 1))
    args = (dqt, k_past, v_past, dkb, dvb)
    return pl.pallas_call(
        functools.partial(_diff_step_kernel, lam_init=lam_init),
        grid=(b,),
        in_specs=[full(lam_rows), full(subln)] + [blk(a) for a in args],
        out_specs=pl.BlockSpec((1, n, DIFF_HEADS * DIFF_V), lambda bi: (bi, 0, 0)),
        out_shape=jax.ShapeDtypeStruct((b, n, DIFF_HEADS * DIFF_V), BF16),
        compiler_params=_cparams(("parallel",)),
        name="diff_step",
    )(lam_rows, subln, *args)


def _first_index_of_max(vals, lane):
    m = jnp.max(vals, axis=-1, keepdims=True)
    idx = jnp.min(jnp.where(vals == m, lane, float(LANES)), axis=-1, keepdims=True)
    return m, idx


_SLAB_MEMBER = EXP_PER_GROUP
_SLAB_RANK = EXP_PER_GROUP + 1


def _post_attn_kernel(x_ref, mla_ref, diff_ref, wo_a_ref, wo_b_ref, g_ref, b_ref, wr_ref, br_ref, ltri_ref, utri_ref,
                      h_ref, hb_ref, slab_ref, rank_ref, cnt_ref, *, alpha):
    a = (jnp.dot(mla_ref[...], wo_a_ref[...], preferred_element_type=F32)
         + jnp.dot(diff_ref[...], wo_b_ref[...], preferred_element_type=F32))
    h = _layer_norm(alpha * x_ref[...] + a, g_ref[...], b_ref[...])
    h_ref[...] = h
    hb = h.astype(BF16)
    hb_ref[...] = hb

    logit = jnp.dot(hb, wr_ref[...], preferred_element_type=F32) + br_ref[...]
    lane_i = lax.broadcasted_iota(jnp.int32, logit.shape, 1)
    lane = lane_i.astype(F32)
    group_of_lane = (lane_i >> (EXP_PER_GROUP.bit_length() - 1)).astype(F32)
    neg = -jnp.inf
    g_logit = jnp.where((lane_i >= N_EXPERTS) & (lane_i < N_EXPERTS + N_GROUPS), logit, neg)
    g_max, g_lane = _first_index_of_max(g_logit, lane)
    g_w = 1.0 / jnp.sum(jnp.exp(g_logit - g_max), axis=-1, keepdims=True)
    g_idx = g_lane - float(N_EXPERTS)
    e_logit = jnp.where(group_of_lane == g_idx, logit, neg)
    m1, i1 = _first_index_of_max(e_logit, lane)
    m2, i2 = _first_index_of_max(jnp.where(lane == i1, neg, e_logit), lane)
    r = jnp.exp(m2 - m1)
    p1 = 1.0 / (1.0 + r)
    p2 = r / (1.0 + r)
    gate = g_w * (jnp.where(lane == i1, p1, 0.0) + jnp.where(lane == i2, p2, 0.0))

    onehot = jnp.where(lane == g_idx, 1.0, 0.0)
    rank_c = jnp.dot(ltri_ref[...], onehot.astype(BF16), preferred_element_type=F32)
    onehot_t = onehot.T[:BF16_ROWS]
    rank_r = jnp.dot(onehot_t.astype(BF16), utri_ref[...], preferred_element_type=F32)
    rank_ref[0] = jnp.where(onehot_t > 0.0, rank_r, -1.0)
    cnt_ref[0] = jnp.broadcast_to(jnp.sum(onehot, axis=0, keepdims=True), cnt_ref.shape[1:])

    for g in range(N_GROUPS):
        member = g_idx == float(g)
        rank_g = jnp.sum(jnp.where(lane_i == g, rank_c, 0.0), axis=-1, keepdims=True)
        rolled = gate if g == 0 else pltpu.roll(gate, LANES - EXP_PER_GROUP * g, 1)
        extra = jnp.where(lane_i == _SLAB_MEMBER, jnp.where(member, 1.0, 0.0),
                          jnp.where(lane_i == _SLAB_RANK, jnp.where(member, rank_g, -1.0), 0.0))
        slab_ref[:, g * LANES:(g + 1) * LANES] = jnp.where(lane_i < EXP_PER_GROUP, rolled, extra)


def _post_attn(x, mla_o, diff_o, w, *, tm, alpha):
    t = x.shape[0]
    nt = t // tm
    row = lambda n: pl.BlockSpec((tm, n), lambda i: (i, 0))
    full = lambda a: pl.BlockSpec(a.shape, lambda i: (0,) * a.ndim)
    per_tile = lambda r, c: pl.BlockSpec((1, r, c), lambda i: (i, 0, 0))
    idx = jnp.arange(tm)
    ltri = (idx[None, :] < idx[:, None]).astype(BF16)
    weights = (w['w_out_a'], w['w_out_b'], w['ln1_g'], w['ln1_b'], w['w_route'], w['b_route'], ltri, ltri.T)
    return pl.pallas_call(
        functools.partial(_post_attn_kernel, alpha=alpha),
        grid=(nt,),
        in_specs=[row(D_MODEL), row(512), row(512)] + [full(a) for a in weights],
        out_specs=[row(D_MODEL), row(D_MODEL), row(N_GROUPS * LANES), per_tile(BF16_ROWS, tm), per_tile(8, LANES)],
        out_shape=[jax.ShapeDtypeStruct((t, D_MODEL), F32), jax.ShapeDtypeStruct((t, D_MODEL), BF16),
                   jax.ShapeDtypeStruct((t, N_GROUPS * LANES), F32),
                   jax.ShapeDtypeStruct((nt, BF16_ROWS, tm), F32), jax.ShapeDtypeStruct((nt, 8, LANES), F32)],
        compiler_params=_cparams(("parallel",)),
        name="post_attn",
    )(x, mla_o, diff_o, *weights)


MOE_TILE = 1024
MOE_ROWS = 256


def _split3(x):
    hi = x.astype(BF16)
    r1 = x - hi.astype(F32)
    mid = r1.astype(BF16)
    lo = (r1 - mid.astype(F32)).astype(BF16)
    return hi, mid, lo


def _moe_kernel(cnt_ref, hb_ref, slab_ref, rank_ref, wg_ref, wu_ref, wd_ref, yin_ref, g_ref, b_ref, y_ref,
                *, g, alpha):
    n_tok = cnt_ref[pl.program_id(0), g]
    slab = slab_ref[...]
    rank_col = slab[:, _SLAB_RANK:_SLAB_RANK + 1]
    rank_row = rank_ref[0, g:g + 1, :]
    gate3 = _split3(slab)
    sub = lax.broadcasted_iota(jnp.int32, (MOE_ROWS, 1), 0).astype(F32)
    lan = lax.broadcasted_iota(jnp.int32, (1, MOE_ROWS), 1).astype(F32)

    y_ref[...] = alpha * yin_ref[...] if g == 0 else yin_ref[...]

    def one_pass(c, carry):
        base = (c * MOE_ROWS).astype(F32)
        sel = jnp.where(rank_row - base == sub, 1.0, 0.0).astype(BF16)
        sel_t = jnp.where(rank_col - base == lan, 1.0, 0.0).astype(BF16)
        x = jnp.dot(sel, hb_ref[...], preferred_element_type=F32).astype(BF16)
        gate_c = sum(jnp.dot(sel, part, preferred_element_type=F32) for part in gate3)
        ys = jnp.zeros((MOE_ROWS, D_MODEL), F32)
        for e in range(EXP_PER_GROUP):
            hid = (jax.nn.silu(jnp.dot(x, wg_ref[e], preferred_element_type=F32))
                   * jnp.dot(x, wu_ref[e], preferred_element_type=F32))
            ys = ys + jnp.dot((hid * gate_c[:, e:e + 1]).astype(BF16), wd_ref[e], preferred_element_type=F32)
        ys_hi = ys.astype(BF16)
        ys_lo = (ys - ys_hi.astype(F32)).astype(BF16)
        y_ref[...] += (jnp.dot(sel_t, ys_hi, preferred_element_type=F32)
                       + jnp.dot(sel_t, ys_lo, preferred_element_type=F32))
        return carry

    lax.fori_loop(0, (n_tok + MOE_ROWS - 1) // MOE_ROWS, one_pass, 0)

    if g == N_GROUPS - 1:
        y_ref[...] = _layer_norm(y_ref[...], g_ref[...], b_ref[...])


def _moe_group(g, y, hb, slab, rank_r, cnt, w, *, tm, alpha):
    t = y.shape[0]
    row = lambda n: pl.BlockSpec((tm, n), lambda i, c: (i, 0))
    full = lambda a: pl.BlockSpec(a.shape, lambda i, c: (0,) * a.ndim)
    grp = lambda a: pl.BlockSpec((EXP_PER_GROUP,) + a.shape[1:], lambda i, c: (g, 0, 0),
                                 pipeline_mode=pl.Buffered(1))
    grid_spec = pltpu.PrefetchScalarGridSpec(
        num_scalar_prefetch=1,
        grid=(t // tm,),
        in_specs=[row(D_MODEL), pl.BlockSpec((tm, LANES), lambda i, c: (i, g)),
                  pl.BlockSpec((1, BF16_ROWS, tm), lambda i, c: (i, 0, 0)),
                  grp(w['w_gate']), grp(w['w_up']), grp(w['w_down']), row(D_MODEL),
                  full(w['ln2_g']), full(w['ln2_b'])],
        out_specs=row(D_MODEL))
    return pl.pallas_call(
        functools.partial(_moe_kernel, g=g, alpha=alpha),
        grid_spec=grid_spec,
        out_shape=jax.ShapeDtypeStruct((t, D_MODEL), F32),
        compiler_params=_cparams(("arbitrary",)),
        name=f"moe_group{g}",
    )(cnt, hb, slab, rank_r, w['w_gate'], w['w_up'], w['w_down'], y, w['ln2_g'], w['ln2_b'])


def _moe(h, hb, slab, rank_r, cnt, w, *, tm, alpha):
    y = h
    for g in range(N_GROUPS):
        y = _moe_group(g, y, hb, slab, rank_r, cnt, w, tm=tm, alpha=alpha)
    return y


def _rope_tables(pos):
    pos = pos.astype(F32)[:, None]
    lane = jnp.arange(LANES)
    inv_m = ROPE_THETA ** (-jnp.arange(0, MLA_ROPE, 2, dtype=F32) / MLA_ROPE)
    ang_m = pos * inv_m
    inv_d = ROPE_THETA ** (-jnp.arange(0, DIFF_ROT, 2, dtype=F32) / DIFF_ROT)
    ang_d = pos * inv_d
    narrow = lax.optimization_barrier((jnp.cos(ang_m), jnp.sin(ang_m), jnp.cos(ang_d), jnp.sin(ang_d)))
    tabs = tuple(a[:, lane % a.shape[1]] for a in narrow)
    tabs_t = tuple(a.T for a in narrow)
    return tabs, tabs_t


def _prep_weights(l, w_in, mla_q_norm, mla_w_uq, mla_kv_norm, mla_w_ukv, diff_subln, w_out, ln1_g, ln1_b,
                  ln2_g, ln2_b, w_route_group, b_route_group, w_route_expert, b_route_expert,
                  w_exp_gate, w_exp_up, w_exp_down, lam_rows):
    wi = w_in[l]
    s0, s1, s2, s3, s4 = 256, 512, 544, 1056, 1568
    w_in_r = jnp.concatenate([wi[:, :s0], wi[:, s0:s1], wi[:, s3:s4], wi[:, s4:],
                              wi[:, s1:s2], jnp.zeros((D_MODEL, LANES - MLA_ROPE), F32)], axis=1)
    w_uq = jnp.pad(mla_w_uq[l].reshape(MLA_Q_RANK, MLA_HEADS, MLA_NOPE + MLA_ROPE),
                   ((0, 0), (0, 0), (0, HEAD_PAD - MLA_NOPE - MLA_ROPE))).reshape(MLA_Q_RANK, -1)
    ukv = mla_w_ukv[l].reshape(MLA_KV_RANK, MLA_HEADS, MLA_NOPE + MLA_V)
    w_uk = jnp.pad(ukv[..., :MLA_NOPE], ((0, 0), (0, 0), (0, HEAD_PAD - MLA_NOPE))).reshape(MLA_KV_RANK, -1)
    w_uv = ukv[..., MLA_NOPE:].reshape(MLA_KV_RANK, -1)
    rows = jnp.arange(LANES)[:, None]
    cols = jnp.arange(MLA_HEADS * HEAD_PAD)[None, :]
    e_place = ((rows < MLA_ROPE) & (cols % HEAD_PAD == rows + MLA_NOPE)).astype(BF16)
    w_route = jnp.concatenate([w_route_expert[l], w_route_group[l],
                               jnp.zeros((D_MODEL, LANES - N_EXPERTS - N_GROUPS), F32)], axis=1)
    b_route = jnp.concatenate([b_route_expert[l], b_route_group[l],
                               jnp.zeros((LANES - N_EXPERTS - N_GROUPS,), F32)])[None, :]
    return dict(
        w_in=w_in_r.astype(BF16), w_dqt=wi[:, s2:s3].T.astype(BF16),
        q_norm=mla_q_norm[l][None, :], w_uqt=w_uq.T.astype(BF16),
        kv_norm=mla_kv_norm[l][None, :], w_ukv=jnp.concatenate([w_uk, w_uv], axis=1).astype(BF16),
        w_uvt=w_uv.T.astype(BF16),
        e_place=e_place, subln=diff_subln[l][None, :], lam_rows=lam_rows,
        w_out_a=w_out[l][:MLA_HEADS * MLA_V].astype(BF16), w_out_b=w_out[l][MLA_HEADS * MLA_V:].astype(BF16),
        ln1_g=ln1_g[l][None, :], ln1_b=ln1_b[l][None, :], ln2_g=ln2_g[l][None, :], ln2_b=ln2_b[l][None, :],
        w_route=w_route.astype(BF16), b_route=b_route,
        w_gate=w_exp_gate[l].astype(BF16), w_up=w_exp_up[l].astype(BF16), w_down=w_exp_down[l].astype(BF16))


def _block_output(x, mla_o, diff_o, w, alpha):
    t = x.shape[0]
    tm = min(MOE_TILE, t)
    h, hb, slab, rank_r, cnt = _post_attn(x, mla_o.reshape(t, -1), diff_o.reshape(t, -1), w, tm=tm, alpha=alpha)
    cnt = cnt[:, 0, :N_GROUPS].astype(jnp.int32)
    return _moe(h, hb, slab, rank_r, cnt, w, tm=tm, alpha=alpha)


def kernel(x_prompt, x_sample, cache_mla_ckv, cache_mla_kpe, cache_diff_k, cache_diff_v, w_in, mla_q_norm, mla_w_uq, mla_kv_norm, mla_w_ukv, diff_lambda_q1, diff_lambda_k1, diff_lambda_q2, diff_lambda_k2, diff_subln, w_out, ln1_g, ln1_b, ln2_g, ln2_b, w_route_group, b_route_group, w_route_expert, b_route_expert, w_exp_gate, w_exp_up, w_exp_down):
    depth = w_in.shape[0]
    bp, sp, _ = x_prompt.shape
    bs, ss, _ = x_sample.shape
    past = cache_mla_ckv.shape[2]
    alpha = (2.0 * depth) ** 0.25
    tabs_p, tabs_pt = _rope_tables(jnp.arange(sp))
    tabs_s, tabs_st = _rope_tables(past + jnp.arange(ss))
    hp = x_prompt.reshape(bp * sp, D_MODEL)
    hs = x_sample.reshape(bs * ss, D_MODEL)
    outs = [[] for _ in range(8)]
    for l in range(depth):
        lam_init = 0.8 - 0.6 * math.exp(-0.3 * l)
        lam_rows = jnp.stack([diff_lambda_q1[l], diff_lambda_k1[l], diff_lambda_q2[l], diff_lambda_k2[l]])
        w = _prep_weights(l, w_in, mla_q_norm, mla_w_uq, mla_kv_norm, mla_w_ukv, diff_subln, w_out,
                          ln1_g, ln1_b, ln2_g, ln2_b, w_route_group, b_route_group, w_route_expert,
                          b_route_expert, w_exp_gate, w_exp_up, w_exp_down, lam_rows)

        ckv, kpe, dk, dv, qt, km, _, vt, dqt, dkb, _, dvt = _proj(hp, tabs_p, tabs_pt, w, tm=ATTN_TILE)
        b3 = lambda a: a.reshape(bp, sp, -1)
        mla_o = _mla_flash(qt, b3(km), vt, t=ATTN_TILE, heads=4)
        diff_o = _diff_flash(w['lam_rows'], w['subln'], dqt, b3(dkb), dvt, t=ATTN_TILE, heads=2,
                             lam_init=lam_init)
        outs[0].append(ckv.reshape(bp, sp, MLA_KV_RANK))
        outs[1].append(kpe.reshape(bp, sp, MLA_ROPE))
        outs[2].append(dk.reshape(bp, sp, DIFF_HEADS, 2 * DIFF_D))
        outs[3].append(dv.reshape(bp, sp, DIFF_HEADS, DIFF_V))
        hp = _block_output(hp, mla_o, diff_o, w, alpha)

        ckv_s, kpe_s, dk_s, dv_s, qt_s, km_s, vm_s, _, dqt_s, dkb_s, dvb_s, _ = _proj(hs, tabs_s, tabs_st, w, tm=ss)
        km_past, vm_past = _expand(cache_mla_ckv[l].reshape(bs * past, MLA_KV_RANK),
                                   cache_mla_kpe[l].reshape(bs * past, MLA_ROPE), w, tm=512)
        s3 = lambda a: a.reshape(bs, ss, -1)
        p3 = lambda a: a.reshape(bs, past, -1)
        mla_o_s = _mla_step(qt_s, p3(km_past), p3(vm_past), s3(km_s), s3(vm_s))
        diff_o_s = _diff_step(w['lam_rows'], w['subln'], dqt_s, cache_diff_k[l], cache_diff_v[l],
                              s3(dkb_s), s3(dvb_s), lam_init=lam_init)
        outs[4].append(ckv_s.reshape(bs, ss, MLA_KV_RANK))
        outs[5].append(kpe_s.reshape(bs, ss, MLA_ROPE))
        outs[6].append(dk_s.reshape(bs, ss, DIFF_HEADS, 2 * DIFF_D))
        outs[7].append(dv_s.reshape(bs, ss, DIFF_HEADS, DIFF_V))
        hs = _block_output(hs, mla_o_s, diff_o_s, w, alpha)

    return (hp.reshape(bp, sp, D_MODEL), hs.reshape(bs, ss, D_MODEL)) + tuple(jnp.stack(o) for o in outs)
```

```python
import functools
import math

import jax
import jax.numpy as jnp
from jax import lax
from jax.experimental import pallas as pl
from jax.experimental.pallas import tpu as pltpu

F32 = jnp.float32
BF16 = jnp.bfloat16

D_MODEL = 1024
CHUNK = 64
ROPE_THETA = 500000.0
MLA_HEADS = 8
MLA_NOPE = 64
MLA_ROPE = 32
MLA_V = 64
MLA_Q_RANK = 256
MLA_KV_RANK = 256
MLA_SCALE = (MLA_NOPE + MLA_ROPE) ** -0.5
DIFF_HEADS = 4
DIFF_D = 64
DIFF_V = 2 * DIFF_D
DIFF_ROT = DIFF_D // 4
DIFF_SCALE = DIFF_D ** -0.5
N_GROUPS = 4
EXP_PER_GROUP = 8
N_EXPERTS = N_GROUPS * EXP_PER_GROUP
D_EXPERT = 256
EPS_LN = 1e-5
EPS_RMS = 1e-6
LOG2E = math.log2(math.e)

LANES = 128
HEAD_PAD = LANES
ATTN_TILE = 256
BF16_ROWS = 16
MLA_VA = MLA_V + BF16_ROWS
DIFF_VA = DIFF_V + BF16_ROWS
VMEM_LIMIT = 56 * 1024 * 1024

_CQ, _CKV, _DK, _DV, _KR, _IN_COLS_R = 0, 256, 512, 1024, 1536, 1664


def _cparams(sem):
    return pltpu.CompilerParams(dimension_semantics=sem, vmem_limit_bytes=VMEM_LIMIT)


def _rms(x, g):
    return x * lax.rsqrt(jnp.mean(x * x, axis=-1, keepdims=True) + EPS_RMS) * g


def _layer_norm(x, g, b):
    mu = jnp.mean(x, axis=-1, keepdims=True)
    xc = x - mu
    var = jnp.mean(xc * xc, axis=-1, keepdims=True)
    return xc * lax.rsqrt(var + EPS_LN) * g + b


def _nt_dot(a, b):
    return lax.dot_general(a, b, (((1,), (1,)), ((), ())), preferred_element_type=F32)


def _rope_coeffs(cos_t, sin_t, lo, half, period):
    lane = lax.broadcasted_iota(jnp.int32, cos_t.shape, 1) & (period - 1)
    is1 = (lane >= lo) & (lane < lo + half)
    is2 = (lane >= lo + half) & (lane < lo + 2 * half)
    c = jnp.where(is1 | is2, cos_t, 1.0)
    a = jnp.where(is1, -sin_t, 0.0)
    b = jnp.where(is2, sin_t, 0.0)
    return c, a, b


def _rope_apply(x, coeffs, half):
    c, a, b = coeffs
    return x * c + pltpu.roll(x, LANES - half, 1) * a + pltpu.roll(x, half, 1) * b


def _rope_rows(x, cos_t, sin_t):
    half = cos_t.shape[0]
    x1, x2 = x[:half], x[half:]
    return x1 * cos_t - x2 * sin_t, x1 * sin_t + x2 * cos_t


def _proj_kernel(x_ref, w_in_ref, w_dqt_ref, qn_ref, w_uqt_ref, kvn_ref, w_ukv_ref, w_uvt_ref, e_ref,
                 cm_ref, sm_ref, cd_ref, sd_ref, cmt_ref, smt_ref, cdt_ref, sdt_ref,
                 ckv_ref, kpe_ref, dk_ref, dv_ref, qt_ref, km_ref, vm_ref, vt_ref,
                 dqt_ref, dkb_ref, dvb_ref, dvt_ref):
    xb = x_ref[...].astype(BF16)
    tm = xb.shape[0]
    z = jnp.dot(xb, w_in_ref[...], preferred_element_type=F32)
    half_m = MLA_ROPE // 2
    half_d = DIFF_ROT // 2
    coef_k = _rope_coeffs(cm_ref[...], sm_ref[...], 0, half_m, LANES)
    coef_d = _rope_coeffs(cd_ref[...], sd_ref[...], 0, half_d, DIFF_D)

    cq = _rms(z[:, _CQ:_CQ + MLA_Q_RANK], qn_ref[...])
    qt = _nt_dot(w_uqt_ref[...], cq.astype(BF16)) * (MLA_SCALE * LOG2E)
    cos_mt, sin_mt = cmt_ref[...], smt_ref[...]
    for h in range(MLA_HEADS):
        r0 = h * HEAD_PAD
        qt_ref[0, r0:r0 + MLA_NOPE, :] = qt[r0:r0 + MLA_NOPE].astype(BF16)
        o1, o2 = _rope_rows(qt[r0 + MLA_NOPE:r0 + MLA_NOPE + MLA_ROPE], cos_mt, sin_mt)
        qt_ref[0, r0 + MLA_NOPE:r0 + MLA_NOPE + half_m, :] = o1.astype(BF16)
        qt_ref[0, r0 + MLA_NOPE + half_m:r0 + MLA_NOPE + MLA_ROPE, :] = o2.astype(BF16)
        qt_ref[0, r0 + MLA_NOPE + MLA_ROPE:r0 + HEAD_PAD, :] = jnp.zeros((HEAD_PAD - MLA_NOPE - MLA_ROPE, tm), BF16)

    ckv = _rms(z[:, _CKV:_CKV + MLA_KV_RANK], kvn_ref[...])
    ckv_ref[...] = ckv
    ckv_b = ckv.astype(BF16)
    kslab = _rope_apply(z[:, _KR:_KR + LANES], coef_k, half_m)
    kpe_ref[...] = kslab[:, :MLA_ROPE]
    kv = jnp.dot(ckv_b, w_ukv_ref[...], preferred_element_type=F32)
    k_full = kv[:, :MLA_HEADS * HEAD_PAD] + jnp.dot(kslab.astype(BF16), e_ref[...], preferred_element_type=F32)
    km_ref[...] = k_full.astype(BF16)
    vm_ref[...] = kv[:, MLA_HEADS * HEAD_PAD:].astype(BF16)
    ones_rows = jnp.where(lax.broadcasted_iota(jnp.int32, (BF16_ROWS, tm), 0) == 0, 1.0, 0.0).astype(BF16)
    vt = _nt_dot(w_uvt_ref[...], ckv_b)
    for h in range(MLA_HEADS):
        vt_ref[0, h * MLA_VA:h * MLA_VA + MLA_V, :] = vt[h * MLA_V:(h + 1) * MLA_V].astype(BF16)
        vt_ref[0, h * MLA_VA + MLA_V:(h + 1) * MLA_VA, :] = ones_rows

    dqt = _nt_dot(w_dqt_ref[...], xb) * (DIFF_SCALE * LOG2E)
    cos_dt, sin_dt = cdt_ref[...], sdt_ref[...]
    zeros = jnp.zeros((DIFF_D, tm), BF16)
    for n in range(2 * DIFF_HEADS):
        r0 = n * DIFF_D
        o1, o2 = _rope_rows(dqt[r0:r0 + DIFF_ROT], cos_dt, sin_dt)
        qn = jnp.concatenate([o1, o2, dqt[r0 + DIFF_ROT:r0 + DIFF_D]], axis=0).astype(BF16)
        lo, hi = (qn, zeros) if n % 2 == 0 else (zeros, qn)
        dqt_ref[0, n * LANES:n * LANES + DIFF_D, :] = lo
        dqt_ref[0, n * LANES + DIFF_D:(n + 1) * LANES, :] = hi

    dv = z[:, _DV:_DV + DIFF_HEADS * DIFF_V]
    dvb_ref[...] = dv.astype(BF16)
    dvt = dv.T
    for h in range(DIFF_HEADS):
        sl = slice(h * LANES, (h + 1) * LANES)
        dk = _rope_apply(z[:, _DK + h * LANES:_DK + (h + 1) * LANES], coef_d, half_d)
        dk_ref[:, h, :] = dk
        dkb_ref[:, sl] = dk.astype(BF16)
        dv_ref[:, h, :] = dv[:, sl]
        dvt_ref[0, h * DIFF_VA:h * DIFF_VA + DIFF_V, :] = dvt[sl].astype(BF16)
        dvt_ref[0, h * DIFF_VA + DIFF_V:(h + 1) * DIFF_VA, :] = ones_rows


def _proj(x, tabs, tabs_t, w, *, tm):
    t = x.shape[0]
    n_pos_blocks = tabs[0].shape[0] // tm
    row = lambda n: pl.BlockSpec((tm, n), lambda i: (i, 0))
    full = lambda a: pl.BlockSpec(a.shape, lambda i: (0,) * a.ndim)
    tab = pl.BlockSpec((tm, LANES), lambda i: (i % n_pos_blocks, 0))
    tab_t = lambda a: pl.BlockSpec((a.shape[0], tm), lambda i: (0, i % n_pos_blocks))
    tr = lambda n: pl.BlockSpec((1, n, tm), lambda i: (i, 0, 0))
    hd = lambda n: pl.BlockSpec((tm, DIFF_HEADS, n), lambda i: (i, 0, 0))
    weights = (w['w_in'], w['w_dqt'], w['q_norm'], w['w_uqt'], w['kv_norm'], w['w_ukv'], w['w_uvt'], w['e_place'])
    outs = ((row, MLA_KV_RANK, F32), (row, MLA_ROPE, F32), (hd, 2 * DIFF_D, F32), (hd, DIFF_V, F32),
            (tr, 1024, BF16), (row, 1024, BF16), (row, 512, BF16), (tr, MLA_HEADS * MLA_VA, BF16),
            (tr, 1024, BF16), (row, 512, BF16), (row, 512, BF16), (tr, DIFF_HEADS * DIFF_VA, BF16))
    shape = lambda kind, n: {tr: (t // tm, n, tm), hd: (t, DIFF_HEADS, n), row: (t, n)}[kind]
    return pl.pallas_call(
        _proj_kernel,
        grid=(t // tm,),
        in_specs=[row(D_MODEL)] + [full(a) for a in weights] + [tab] * 4 + [tab_t(a) for a in tabs_t],
        out_specs=[kind(n) for kind, n, _ in outs],
        out_shape=[jax.ShapeDtypeStruct(shape(kind, n), d) for kind, n, d in outs],
        compiler_params=_cparams(("parallel",)),
        name="proj",
    )(x, *weights, *tabs, *tabs_t)


def _expand_kernel(ckv_ref, kpe_ref, w_ukv_ref, e_ref, km_ref, vm_ref):
    kv = jnp.dot(ckv_ref[...].astype(BF16), w_ukv_ref[...], preferred_element_type=F32)
    k_full = kv[:, :MLA_HEADS * HEAD_PAD] + jnp.dot(kpe_ref[...].astype(BF16), e_ref[:MLA_ROPE, :],
                                                   preferred_element_type=F32)
    km_ref[...] = k_full.astype(BF16)
    vm_ref[...] = kv[:, MLA_HEADS * HEAD_PAD:].astype(BF16)


def _expand(ckv, kpe, w, *, tm):
    r = ckv.shape[0]
    row = lambda n: pl.BlockSpec((tm, n), lambda i: (i, 0))
    full = lambda a: pl.BlockSpec(a.shape, lambda i: (0,) * a.ndim)
    return pl.pallas_call(
        _expand_kernel,
        grid=(r // tm,),
        in_specs=[row(MLA_KV_RANK), row(MLA_ROPE), full(w['w_ukv']), full(w['e_place'])],
        out_specs=[row(1024), row(512)],
        out_shape=[jax.ShapeDtypeStruct((r, 1024), BF16), jax.ShapeDtypeStruct((r, 512), BF16)],
        compiler_params=_cparams(("parallel",)),
        name="expand",
    )(ckv, kpe, w['w_ukv'], w['e_place'])


def _chunk_mask_t(t):
    shift = CHUNK.bit_length() - 1
    kc = lax.broadcasted_iota(jnp.int32, (t, t), 0) >> shift
    qc = lax.broadcasted_iota(jnp.int32, (t, t), 1) >> shift
    return kc <= qc


def _flash_scratch(t, dv, n):
    return [pltpu.VMEM((n, t, t), F32), pltpu.VMEM((n, t, t), F32),
            pltpu.VMEM((n, 1, t), F32), pltpu.VMEM((n, dv, t), F32)]


def _flash_loop(i, score, value, n, s_a, s_b, m_ref, acc_ref, ahead):
    t = s_a.shape[-1]
    mask = _chunk_mask_t(t)
    m_ref[...] = jnp.full(m_ref.shape, -jnp.inf, F32)
    acc_ref[...] = jnp.zeros(acc_ref.shape, F32)
    first, second = tuple(range(n)), tuple(range(n, 2 * n))
    both = first + second

    def step(j, s_cur, s_nxt, cur, masked, nxt):
        pending = list(nxt)

        def issue_scores(count):
            for k in pending[:count]:
                s_nxt[k] = score(j + 1, k // n, k % n)
            del pending[:count]

        issue_scores(ahead)
        for k in cur:
            issue_scores(1)
            s = s_cur[k]
            if k in masked:
                s = jnp.where(mask, s, -jnp.inf)
            m = m_ref[k]
            m_new = jnp.maximum(m, jnp.max(s, axis=0, keepdims=True))
            alpha = jnp.exp2(m - m_new)
            p = jnp.exp2(s - m_new)
            m_ref[k] = m_new
            acc_ref[k] = alpha * acc_ref[k] + jnp.dot(value(j, k % n), p.astype(BF16), preferred_element_type=F32)
        issue_scores(len(pending))

    for k in both:
        s_a[k] = score(0, k // n, k % n)

    def pair(jj, carry):
        step(2 * jj, s_a, s_b, both, (), both)
        step(2 * jj + 1, s_b, s_a, both, (), both)
        return carry

    lax.fori_loop(0, i, pair, 0)
    step(2 * i, s_a, s_b, both, first, second)
    step(2 * i + 1, s_b, None, second, second, ())


def _normalised(acc_ref, k, dv):
    return acc_ref[k, :dv, :] / acc_ref[k, dv:dv + 1, :]


def _mla_flash_kernel(qt_ref, k_ref, vt_ref, o_ref, s_a, s_b, m_ref, acc_ref, *, t, heads):
    def score(j, half, h):
        rows = pl.ds(pl.multiple_of(j * t, t), t)
        return jnp.dot(k_ref[0, rows, h * HEAD_PAD:(h + 1) * HEAD_PAD],
                       qt_ref[half, h * HEAD_PAD:(h + 1) * HEAD_PAD, :], preferred_element_type=F32)

    value = lambda j, h: vt_ref[j, h * MLA_VA:(h + 1) * MLA_VA, :]
    _flash_loop(pl.program_id(2), score, value, heads, s_a, s_b, m_ref, acc_ref, ahead=2)
    for half in range(2):
        for h in range(0, heads, 2):
            k = half * heads + h
            pair = jnp.concatenate([_normalised(acc_ref, k, MLA_V), _normalised(acc_ref, k + 1, MLA_V)], axis=0)
            o_ref[0, half * t:(half + 1) * t, h * MLA_V:(h + 2) * MLA_V] = pair.T.astype(BF16)


def _mla_flash(qt, km, vt, *, t, heads):
    b, s, _ = km.shape
    nk = s // t
    nq2 = nk // 2
    return pl.pallas_call(
        functools.partial(_mla_flash_kernel, t=t, heads=heads),
        grid=(b, MLA_HEADS // heads, nq2),
        in_specs=[pl.BlockSpec((2, heads * HEAD_PAD, t), lambda bi, hi, i: (bi * nq2 + i, hi, 0)),
                  pl.BlockSpec((1, s, heads * HEAD_PAD), lambda bi, hi, i: (bi, 0, hi)),
                  pl.BlockSpec((nk, heads * MLA_VA, t), lambda bi, hi, i: (bi, hi, 0))],
        out_specs=pl.BlockSpec((1, 2 * t, heads * MLA_V), lambda bi, hi, i: (bi, i, hi)),
        out_shape=jax.ShapeDtypeStruct((b, s, MLA_HEADS * MLA_V), BF16),
        scratch_shapes=_flash_scratch(t, MLA_VA, 2 * heads),
        compiler_params=_cparams(("parallel", "parallel", "arbitrary")),
        name="mla_flash",
    )(qt, km, vt)


def _diff_lambda(lam_ref, lam_init):
    lq1, lk1, lq2, lk2 = (lam_ref[r:r + 1, :] for r in range(4))
    return (jnp.exp(jnp.sum(lq1 * lk1, axis=-1, keepdims=True))
            - jnp.exp(jnp.sum(lq2 * lk2, axis=-1, keepdims=True)) + lam_init)


def _diff_finish(o1, o2, lam, subln, lam_init):
    o = o1 - lam * o2
    return _rms(o, subln) * (1.0 - lam_init)


def _diff_flash_kernel(lam_ref, subln_ref, qt_ref, k_ref, vt_ref, o_ref, s_a, s_b, m_ref, acc_ref,
                       *, t, heads, lam_init):
    def score(j, half, n):
        rows = pl.ds(pl.multiple_of(j * t, t), t)
        return jnp.dot(k_ref[0, rows, (n // 2) * LANES:(n // 2 + 1) * LANES],
                       qt_ref[half, n * LANES:(n + 1) * LANES, :], preferred_element_type=F32)

    value = lambda j, n: vt_ref[j, (n // 2) * DIFF_VA:(n // 2 + 1) * DIFF_VA, :]
    _flash_loop(pl.program_id(2), score, value, 2 * heads, s_a, s_b, m_ref, acc_ref, ahead=1)
    lam = _diff_lambda(lam_ref, lam_init)
    for half in range(2):
        for h in range(heads):
            k = half * 2 * heads + 2 * h
            o1 = _normalised(acc_ref, k, DIFF_V).T
            o2 = _normalised(acc_ref, k + 1, DIFF_V).T
            o = _diff_finish(o1, o2, lam, subln_ref[...], lam_init)
            o_ref[0, half * t:(half + 1) * t, h * DIFF_V:(h + 1) * DIFF_V] = o.astype(BF16)


def _diff_flash(lam_rows, subln, dqt, dkb, dvt, *, t, heads, lam_init):
    b, s, _ = dkb.shape
    nk = s // t
    nq2 = nk // 2
    full = lambda a: pl.BlockSpec(a.shape, lambda bi, hi, i: (0,) * a.ndim)
    return pl.pallas_call(
        functools.partial(_diff_flash_kernel, t=t, heads=heads, lam_init=lam_init),
        grid=(b, DIFF_HEADS // heads, nq2),
        in_specs=[full(lam_rows), full(subln),
                  pl.BlockSpec((2, heads * 2 * LANES, t), lambda bi, hi, i: (bi * nq2 + i, hi, 0)),
                  pl.BlockSpec((1, s, heads * LANES), lambda bi, hi, i: (bi, 0, hi)),
                  pl.BlockSpec((nk, heads * DIFF_VA, t), lambda bi, hi, i: (bi, hi, 0))],
        out_specs=pl.BlockSpec((1, 2 * t, heads * DIFF_V), lambda bi, hi, i: (bi, i, hi)),
        out_shape=jax.ShapeDtypeStruct((b, s, DIFF_HEADS * DIFF_V), BF16),
        scratch_shapes=_flash_scratch(t, DIFF_VA, 4 * heads),
        compiler_params=_cparams(("parallel", "parallel", "arbitrary")),
        name="diff_flash",
    )(lam_rows, subln, dqt, dkb, dvt)


def _two_part_softmax_pv(s_past, s_new, v_past, v_new):
    m = jnp.maximum(jnp.max(s_past, axis=-1, keepdims=True), jnp.max(s_new, axis=-1, keepdims=True))
    p_past = jnp.exp2(s_past - m)
    p_new = jnp.exp2(s_new - m)
    l = jnp.sum(p_past, axis=-1, keepdims=True) + jnp.sum(p_new, axis=-1, keepdims=True)
    acc = (jnp.dot(p_past.astype(BF16), v_past, preferred_element_type=F32)
           + jnp.dot(p_new.astype(BF16), v_new, preferred_element_type=F32))
    return acc / l


def _token_major(qt_ref):
    return qt_ref[0].astype(F32).T.astype(BF16)


def _mla_step_kernel(qt_ref, kp_ref, vp_ref, kn_ref, vn_ref, o_ref):
    q_all = _token_major(qt_ref)
    for h in range(MLA_HEADS):
        ks = slice(h * HEAD_PAD, (h + 1) * HEAD_PAD)
        vs = slice(h * MLA_V, (h + 1) * MLA_V)
        q = q_all[:, ks]
        o = _two_part_softmax_pv(_nt_dot(q, kp_ref[0, :, ks]), _nt_dot(q, kn_ref[0, :, ks]),
                                 vp_ref[0, :, vs], vn_ref[0, :, vs])
        o_ref[0, :, vs] = o.astype(BF16)


def _mla_step(qt, km_past, vm_past, km_new, vm_new):
    b, _, n = qt.shape
    blk = lambda a: pl.BlockSpec((1,) + a.shape[1:], lambda bi: (bi, 0, 0))
    args = (qt, km_past, vm_past, km_new, vm_new)
    return pl.pallas_call(
        _mla_step_kernel,
        grid=(b,),
        in_specs=[blk(a) for a in args],
        out_specs=pl.BlockSpec((1, n, MLA_HEADS * MLA_V), lambda bi: (bi, 0, 0)),
        out_shape=jax.ShapeDtypeStruct((b, n, MLA_HEADS * MLA_V), BF16),
        compiler_params=_cparams(("parallel",)),
        name="mla_step",
    )(*args)


def _diff_step_kernel(lam_ref, subln_ref, qt_ref, kp_ref, vp_ref, kn_ref, vn_ref, o_ref, *, lam_init):
    lam = _diff_lambda(lam_ref, lam_init)
    q = _token_major(qt_ref)
    for h in range(DIFF_HEADS):
        sl = slice(h * LANES, (h + 1) * LANES)
        q1 = q[:, 2 * h * LANES:(2 * h + 1) * LANES]
        q2 = q[:, (2 * h + 1) * LANES:(2 * h + 2) * LANES]
        kp = kp_ref[0, :, h, :].astype(BF16)
        vp = vp_ref[0, :, h, :].astype(BF16)
        kn = kn_ref[0, :, sl]
        vn = vn_ref[0, :, sl]
        o1 = _two_part_softmax_pv(_nt_dot(q1, kp), _nt_dot(q1, kn), vp, vn)
        o2 = _two_part_softmax_pv(_nt_dot(q2, kp), _nt_dot(q2, kn), vp, vn)
        o_ref[0, :, sl] = _diff_finish(o1, o2, lam, subln_ref[...], lam_init).astype(BF16)


def _diff_step(lam_rows, subln, dqt, k_past, v_past, dkb, dvb, *, lam_init):
    b, _, n = dqt.shape
    full = lambda a: pl.BlockSpec(a.shape, lambda bi: (0,) * a.ndim)
    blk = lambda a: pl.BlockSpec((1,) + a.shape[1:], lambda bi: (bi,) + (0,) * (a.ndim - 1))
    args = (dqt, k_past, v_past, dkb, dvb)
    return pl.pallas_call(
        functools.partial(_diff_step_kernel, lam_init=lam_init),
        grid=(b,),
        in_specs=[full(lam_rows), full(subln)] + [blk(a) for a in args],
        out_specs=pl.BlockSpec((1, n, DIFF_HEADS * DIFF_V), lambda bi: (bi, 0, 0)),
        out_shape=jax.ShapeDtypeStruct((b, n, DIFF_HEADS * DIFF_V), BF16),
        compiler_params=_cparams(("parallel",)),
        name="diff_step",
    )(lam_rows, subln, *args)


def _first_index_of_max(vals, lane):
    m = jnp.max(vals, axis=-1, keepdims=True)
    idx = jnp.min(jnp.where(vals == m, lane, float(LANES)), axis=-1, keepdims=True)
    return m, idx


_SLAB_MEMBER = EXP_PER_GROUP
_SLAB_RANK = EXP_PER_GROUP + 1


def _post_attn_kernel(x_ref, mla_ref, diff_ref, wo_a_ref, wo_b_ref, g_ref, b_ref, wr_ref, br_ref, ltri_ref, utri_ref,
                      h_ref, hb_ref, slab_ref, rank_ref, cnt_ref, *, alpha):
    a = (jnp.dot(mla_ref[...], wo_a_ref[...], preferred_element_type=F32)
         + jnp.dot(diff_ref[...], wo_b_ref[...], preferred_element_type=F32))
    h = _layer_norm(alpha * x_ref[...] + a, g_ref[...], b_ref[...])
    h_ref[...] = h
    hb = h.astype(BF16)
    hb_ref[...] = hb

    logit = jnp.dot(hb, wr_ref[...], preferred_element_type=F32) + br_ref[...]
    lane_i = lax.broadcasted_iota(jnp.int32, logit.shape, 1)
    lane = lane_i.astype(F32)
    group_of_lane = (lane_i >> (EXP_PER_GROUP.bit_length() - 1)).astype(F32)
    neg = -jnp.inf
    g_logit = jnp.where((lane_i >= N_EXPERTS) & (lane_i < N_EXPERTS + N_GROUPS), logit, neg)
    g_max, g_lane = _first_index_of_max(g_logit, lane)
    g_w = 1.0 / jnp.sum(jnp.exp(g_logit - g_max), axis=-1, keepdims=True)
    g_idx = g_lane - float(N_EXPERTS)
    e_logit = jnp.where(group_of_lane == g_idx, logit, neg)
    m1, i1 = _first_index_of_max(e_logit, lane)
    m2, i2 = _first_index_of_max(jnp.where(lane == i1, neg, e_logit), lane)
    r = jnp.exp(m2 - m1)
    p1 = 1.0 / (1.0 + r)
    p2 = r / (1.0 + r)
    gate = g_w * (jnp.where(lane == i1, p1, 0.0) + jnp.where(lane == i2, p2, 0.0))

    onehot = jnp.where(lane == g_idx, 1.0, 0.0)
    rank_c = jnp.dot(ltri_ref[...], onehot.astype(BF16), preferred_element_type=F32)
    onehot_t = onehot.T[:BF16_ROWS]
    rank_r = jnp.dot(onehot_t.astype(BF16), utri_ref[...], preferred_element_type=F32)
    rank_ref[0] = jnp.where(onehot_t > 0.0, rank_r, -1.0)
    cnt_ref[0] = jnp.broadcast_to(jnp.sum(onehot, axis=0, keepdims=True), cnt_ref.shape[1:])

    for g in range(N_GROUPS):
        member = g_idx == float(g)
        rank_g = jnp.sum(jnp.where(lane_i == g, rank_c, 0.0), axis=-1, keepdims=True)
        rolled = gate if g == 0 else pltpu.roll(gate, LANES - EXP_PER_GROUP * g, 1)
        extra = jnp.where(lane_i == _SLAB_MEMBER, jnp.where(member, 1.0, 0.0),
                          jnp.where(lane_i == _SLAB_RANK, jnp.where(member, rank_g, -1.0), 0.0))
        slab_ref[:, g * LANES:(g + 1) * LANES] = jnp.where(lane_i < EXP_PER_GROUP, rolled, extra)


def _post_attn(x, mla_o, diff_o, w, *, tm, alpha):
    t = x.shape[0]
    nt = t // tm
    row = lambda n: pl.BlockSpec((tm, n), lambda i: (i, 0))
    full = lambda a: pl.BlockSpec(a.shape, lambda i: (0,) * a.ndim)
    per_tile = lambda r, c: pl.BlockSpec((1, r, c), lambda i: (i, 0, 0))
    idx = jnp.arange(tm)
    ltri = (idx[None, :] < idx[:, None]).astype(BF16)
    weights = (w['w_out_a'], w['w_out_b'], w['ln1_g'], w['ln1_b'], w['w_route'], w['b_route'], ltri, ltri.T)
    return pl.pallas_call(
        functools.partial(_post_attn_kernel, alpha=alpha),
        grid=(nt,),
        in_specs=[row(D_MODEL), row(512), row(512)] + [full(a) for a in weights],
        out_specs=[row(D_MODEL), row(D_MODEL), row(N_GROUPS * LANES), per_tile(BF16_ROWS, tm), per_tile(8, LANES)],
        out_shape=[jax.ShapeDtypeStruct((t, D_MODEL), F32), jax.ShapeDtypeStruct((t, D_MODEL), BF16),
                   jax.ShapeDtypeStruct((t, N_GROUPS * LANES), F32),
                   jax.ShapeDtypeStruct((nt, BF16_ROWS, tm), F32), jax.ShapeDtypeStruct((nt, 8, LANES), F32)],
        compiler_params=_cparams(("parallel",)),
        name="post_attn",
    )(x, mla_o, diff_o, *weights)


MOE_TILE = 1024
MOE_EXTRA_ROWS = 128


def _split3(x):
    hi = x.astype(BF16)
    r1 = x - hi.astype(F32)
    mid = r1.astype(BF16)
    lo = (r1 - mid.astype(F32)).astype(BF16)
    return hi, mid, lo


def _moe_kernel(cnt_ref, hb_ref, slab_ref, rank_ref, wg_ref, wu_ref, wd_ref, yin_ref, g_ref, b_ref, y_ref,
                *, g, alpha):
    n_tok = cnt_ref[pl.program_id(0), g]
    slab = slab_ref[...]
    rank_col = slab[:, _SLAB_RANK:_SLAB_RANK + 1]
    rank_row = rank_ref[0, g:g + 1, :]
    gate3 = _split3(slab)

    y_ref[...] = alpha * yin_ref[...] if g == 0 else yin_ref[...]

    def one_pass(base, rows):
        sub = lax.broadcasted_iota(jnp.int32, (rows, 1), 0).astype(F32)
        lan = lax.broadcasted_iota(jnp.int32, (1, rows), 1).astype(F32)
        sel = jnp.where(rank_row - base == sub, 1.0, 0.0).astype(BF16)
        sel_t = jnp.where(rank_col - base == lan, 1.0, 0.0).astype(BF16)
        x = jnp.dot(sel, hb_ref[...], preferred_element_type=F32).astype(BF16)
        gate_c = sum(jnp.dot(sel, part, preferred_element_type=F32) for part in gate3)
        ys = jnp.zeros((rows, D_MODEL), F32)
        for e in range(EXP_PER_GROUP):
            hid = (jax.nn.silu(jnp.dot(x, wg_ref[e], preferred_element_type=F32))
                   * jnp.dot(x, wu_ref[e], preferred_element_type=F32))
            ys = ys + jnp.dot((hid * gate_c[:, e:e + 1]).astype(BF16), wd_ref[e], preferred_element_type=F32)
        ys_hi = ys.astype(BF16)
        ys_lo = (ys - ys_hi.astype(F32)).astype(BF16)
        y_ref[...] += (jnp.dot(sel_t, ys_hi, preferred_element_type=F32)
                       + jnp.dot(sel_t, ys_lo, preferred_element_type=F32))

    first_rows = slab.shape[0] // N_GROUPS
    one_pass(0.0, first_rows)

    def extra_pass(c, carry):
        one_pass((first_rows + c * MOE_EXTRA_ROWS).astype(F32), MOE_EXTRA_ROWS)
        return carry

    lax.fori_loop(0, (jnp.maximum(n_tok - first_rows, 0) + MOE_EXTRA_ROWS - 1) // MOE_EXTRA_ROWS, extra_pass, 0)

    if g == N_GROUPS - 1:
        y_ref[...] = _layer_norm(y_ref[...], g_ref[...], b_ref[...])


def _moe_group(g, y, hb, slab, rank_r, cnt, w, *, tm, alpha):
    t = y.shape[0]
    row = lambda n: pl.BlockSpec((tm, n), lambda i, c: (i, 0))
    full = lambda a: pl.BlockSpec(a.shape, lambda i, c: (0,) * a.ndim)
    grp = lambda a: pl.BlockSpec((EXP_PER_GROUP,) + a.shape[1:], lambda i, c: (g, 0, 0),
                                 pipeline_mode=pl.Buffered(1))
    grid_spec = pltpu.PrefetchScalarGridSpec(
        num_scalar_prefetch=1,
        grid=(t // tm,),
        in_specs=[row(D_MODEL), pl.BlockSpec((tm, LANES), lambda i, c: (i, g)),
                  pl.BlockSpec((1, BF16_ROWS, tm), lambda i, c: (i, 0, 0)),
                  grp(w['w_gate']), grp(w['w_up']), grp(w['w_down']), row(D_MODEL),
                  full(w['ln2_g']), full(w['ln2_b'])],
        out_specs=row(D_MODEL))
    return pl.pallas_call(
        functools.partial(_moe_kernel, g=g, alpha=alpha),
        grid_spec=grid_spec,
        out_shape=jax.ShapeDtypeStruct((t, D_MODEL), F32),
        compiler_params=_cparams(("arbitrary",)),
        name=f"moe_group{g}",
    )(cnt, hb, slab, rank_r, w['w_gate'], w['w_up'], w['w_down'], y, w['ln2_g'], w['ln2_b'])


def _moe(h, hb, slab, rank_r, cnt, w, *, tm, alpha):
    y = h
    for g in range(N_GROUPS):
        y = _moe_group(g, y, hb, slab, rank_r, cnt, w, tm=tm, alpha=alpha)
    return y


def _rope_tables(pos):
    pos = pos.astype(F32)[:, None]
    lane = jnp.arange(LANES)
    inv_m = ROPE_THETA ** (-jnp.arange(0, MLA_ROPE, 2, dtype=F32) / MLA_ROPE)
    ang_m = pos * inv_m
    inv_d = ROPE_THETA ** (-jnp.arange(0, DIFF_ROT, 2, dtype=F32) / DIFF_ROT)
    ang_d = pos * inv_d
    narrow = lax.optimization_barrier((jnp.cos(ang_m), jnp.sin(ang_m), jnp.cos(ang_d), jnp.sin(ang_d)))
    tabs = tuple(a[:, lane % a.shape[1]] for a in narrow)
    tabs_t = tuple(a.T for a in narrow)
    return tabs, tabs_t


def _prep_weights(l, w_in, mla_q_norm, mla_w_uq, mla_kv_norm, mla_w_ukv, diff_subln, w_out, ln1_g, ln1_b,
                  ln2_g, ln2_b, w_route_group, b_route_group, w_route_expert, b_route_expert,
                  w_exp_gate, w_exp_up, w_exp_down, lam_rows):
    wi = w_in[l]
    s0, s1, s2, s3, s4 = 256, 512, 544, 1056, 1568
    w_in_r = jnp.concatenate([wi[:, :s0], wi[:, s0:s1], wi[:, s3:s4], wi[:, s4:],
                              wi[:, s1:s2], jnp.zeros((D_MODEL, LANES - MLA_ROPE), F32)], axis=1)
    w_uq = jnp.pad(mla_w_uq[l].reshape(MLA_Q_RANK, MLA_HEADS, MLA_NOPE + MLA_ROPE),
                   ((0, 0), (0, 0), (0, HEAD_PAD - MLA_NOPE - MLA_ROPE))).reshape(MLA_Q_RANK, -1)
    ukv = mla_w_ukv[l].reshape(MLA_KV_RANK, MLA_HEADS, MLA_NOPE + MLA_V)
    w_uk = jnp.pad(ukv[..., :MLA_NOPE], ((0, 0), (0, 0), (0, HEAD_PAD - MLA_NOPE))).reshape(MLA_KV_RANK, -1)
    w_uv = ukv[..., MLA_NOPE:].reshape(MLA_KV_RANK, -1)
    rows = jnp.arange(LANES)[:, None]
    cols = jnp.arange(MLA_HEADS * HEAD_PAD)[None, :]
    e_place = ((rows < MLA_ROPE) & (cols % HEAD_PAD == rows + MLA_NOPE)).astype(BF16)
    w_route = jnp.concatenate([w_route_expert[l], w_route_group[l],
                               jnp.zeros((D_MODEL, LANES - N_EXPERTS - N_GROUPS), F32)], axis=1)
    b_route = jnp.concatenate([b_route_expert[l], b_route_group[l],
                               jnp.zeros((LANES - N_EXPERTS - N_GROUPS,), F32)])[None, :]
    return dict(
        w_in=w_in_r.astype(BF16), w_dqt=wi[:, s2:s3].T.astype(BF16),
        q_norm=mla_q_norm[l][None, :], w_uqt=w_uq.T.astype(BF16),
        kv_norm=mla_kv_norm[l][None, :], w_ukv=jnp.concatenate([w_uk, w_uv], axis=1).astype(BF16),
        w_uvt=w_uv.T.astype(BF16),
        e_place=e_place, subln=diff_subln[l][None, :], lam_rows=lam_rows,
        w_out_a=w_out[l][:MLA_HEADS * MLA_V].astype(BF16), w_out_b=w_out[l][MLA_HEADS * MLA_V:].astype(BF16),
        ln1_g=ln1_g[l][None, :], ln1_b=ln1_b[l][None, :], ln2_g=ln2_g[l][None, :], ln2_b=ln2_b[l][None, :],
        w_route=w_route.astype(BF16), b_route=b_route,
        w_gate=w_exp_gate[l].astype(BF16), w_up=w_exp_up[l].astype(BF16), w_down=w_exp_down[l].astype(BF16))


def _block_output(x, mla_o, diff_o, w, alpha):
    t = x.shape[0]
    tm = min(MOE_TILE, t)
    h, hb, slab, rank_r, cnt = _post_attn(x, mla_o.reshape(t, -1), diff_o.reshape(t, -1), w, tm=tm, alpha=alpha)
    cnt = cnt[:, 0, :N_GROUPS].astype(jnp.int32)
    return _moe(h, hb, slab, rank_r, cnt, w, tm=tm, alpha=alpha)


def kernel(x_prompt, x_sample, cache_mla_ckv, cache_mla_kpe, cache_diff_k, cache_diff_v, w_in, mla_q_norm, mla_w_uq, mla_kv_norm, mla_w_ukv, diff_lambda_q1, diff_lambda_k1, diff_lambda_q2, diff_lambda_k2, diff_subln, w_out, ln1_g, ln1_b, ln2_g, ln2_b, w_route_group, b_route_group, w_route_expert, b_route_expert, w_exp_gate, w_exp_up, w_exp_down):
    depth = w_in.shape[0]
    bp, sp, _ = x_prompt.shape
    bs, ss, _ = x_sample.shape
    past = cache_mla_ckv.shape[2]
    alpha = (2.0 * depth) ** 0.25
    tabs_p, tabs_pt = _rope_tables(jnp.arange(sp))
    tabs_s, tabs_st = _rope_tables(past + jnp.arange(ss))
    hp = x_prompt.reshape(bp * sp, D_MODEL)
    hs = x_sample.reshape(bs * ss, D_MODEL)
    outs = [[] for _ in range(8)]
    for l in range(depth):
        lam_init = 0.8 - 0.6 * math.exp(-0.3 * l)
        lam_rows = jnp.stack([diff_lambda_q1[l], diff_lambda_k1[l], diff_lambda_q2[l], diff_lambda_k2[l]])
        w = _prep_weights(l, w_in, mla_q_norm, mla_w_uq, mla_kv_norm, mla_w_ukv, diff_subln, w_out,
                          ln1_g, ln1_b, ln2_g, ln2_b, w_route_group, b_route_group, w_route_expert,
                          b_route_expert, w_exp_gate, w_exp_up, w_exp_down, lam_rows)

        ckv, kpe, dk, dv, qt, km, _, vt, dqt, dkb, _, dvt = _proj(hp, tabs_p, tabs_pt, w, tm=ATTN_TILE)
        b3 = lambda a: a.reshape(bp, sp, -1)
        mla_o = _mla_flash(qt, b3(km), vt, t=ATTN_TILE, heads=4)
        diff_o = _diff_flash(w['lam_rows'], w['subln'], dqt, b3(dkb), dvt, t=ATTN_TILE, heads=2,
                             lam_init=lam_init)
        outs[0].append(ckv.reshape(bp, sp, MLA_KV_RANK))
        outs[1].append(kpe.reshape(bp, sp, MLA_ROPE))
        outs[2].append(dk.reshape(bp, sp, DIFF_HEADS, 2 * DIFF_D))
        outs[3].append(dv.reshape(bp, sp, DIFF_HEADS, DIFF_V))
        hp = _block_output(hp, mla_o, diff_o, w, alpha)

        ckv_s, kpe_s, dk_s, dv_s, qt_s, km_s, vm_s, _, dqt_s, dkb_s, dvb_s, _ = _proj(hs, tabs_s, tabs_st, w, tm=ss)
        km_past, vm_past = _expand(cache_mla_ckv[l].reshape(bs * past, MLA_KV_RANK),
                                   cache_mla_kpe[l].reshape(bs * past, MLA_ROPE), w, tm=512)
        s3 = lambda a: a.reshape(bs, ss, -1)
        p3 = lambda a: a.reshape(bs, past, -1)
        mla_o_s = _mla_step(qt_s, p3(km_past), p3(vm_past), s3(km_s), s3(vm_s))
        diff_o_s = _diff_step(w['lam_rows'], w['subln'], dqt_s, cache_diff_k[l], cache_diff_v[l],
                              s3(dkb_s), s3(dvb_s), lam_init=lam_init)
        outs[4].append(ckv_s.reshape(bs, ss, MLA_KV_RANK))
        outs[5].append(kpe_s.reshape(bs, ss, MLA_ROPE))
        outs[6].append(dk_s.reshape(bs, ss, DIFF_HEADS, 2 * DIFF_D))
        outs[7].append(dv_s.reshape(bs, ss, DIFF_HEADS, DIFF_V))
        hs = _block_output(hs, mla_o_s, diff_o_s, w, alpha)

    return (hp.reshape(bp, sp, D_MODEL), hs.reshape(bs, ss, D_MODEL)) + tuple(jnp.stack(o) for o in outs)
```

```python
import functools
import math

import jax
import jax.numpy as jnp
from jax import lax
from jax.experimental import pallas as pl
from jax.experimental.pallas import tpu as pltpu

F32 = jnp.float32
BF16 = jnp.bfloat16

D_MODEL = 1024
CHUNK = 64
ROPE_THETA = 500000.0
MLA_HEADS = 8
MLA_NOPE = 64
MLA_ROPE = 32
MLA_V = 64
MLA_Q_RANK = 256
MLA_KV_RANK = 256
MLA_SCALE = (MLA_NOPE + MLA_ROPE) ** -0.5
DIFF_HEADS = 4
DIFF_D = 64
DIFF_V = 2 * DIFF_D
DIFF_ROT = DIFF_D // 4
DIFF_SCALE = DIFF_D ** -0.5
N_GROUPS = 4
EXP_PER_GROUP = 8
N_EXPERTS = N_GROUPS * EXP_PER_GROUP
D_EXPERT = 256
EPS_LN = 1e-5
EPS_RMS = 1e-6
LOG2E = math.log2(math.e)

LANES = 128
HEAD_PAD = LANES
ATTN_TILE = 256
BF16_ROWS = 16
MLA_VA = MLA_V + BF16_ROWS
DIFF_VA = DIFF_V + BF16_ROWS
VMEM_LIMIT = 56 * 1024 * 1024

_CQ, _CKV, _DK, _DV, _KR, _IN_COLS_R = 0, 256, 512, 1024, 1536, 1664


def _cparams(sem):
    return pltpu.CompilerParams(dimension_semantics=sem, vmem_limit_bytes=VMEM_LIMIT)


def _rms(x, g):
    return x * lax.rsqrt(jnp.mean(x * x, axis=-1, keepdims=True) + EPS_RMS) * g


def _layer_norm(x, g, b):
    mu = jnp.mean(x, axis=-1, keepdims=True)
    xc = x - mu
    var = jnp.mean(xc * xc, axis=-1, keepdims=True)
    return xc * lax.rsqrt(var + EPS_LN) * g + b


def _nt_dot(a, b):
    return lax.dot_general(a, b, (((1,), (1,)), ((), ())), preferred_element_type=F32)


def _rope_coeffs(cos_t, sin_t, lo, half, period):
    lane = lax.broadcasted_iota(jnp.int32, cos_t.shape, 1) & (period - 1)
    is1 = (lane >= lo) & (lane < lo + half)
    is2 = (lane >= lo + half) & (lane < lo + 2 * half)
    c = jnp.where(is1 | is2, cos_t, 1.0)
    a = jnp.where(is1, -sin_t, 0.0)
    b = jnp.where(is2, sin_t, 0.0)
    return c, a, b


def _rope_apply(x, coeffs, half):
    c, a, b = coeffs
    return x * c + pltpu.roll(x, LANES - half, 1) * a + pltpu.roll(x, half, 1) * b


def _rope_rows(x, cos_t, sin_t):
    half = cos_t.shape[0]
    x1, x2 = x[:half], x[half:]
    return x1 * cos_t - x2 * sin_t, x1 * sin_t + x2 * cos_t


def _proj_kernel(x_ref, w_in_ref, w_dqt_ref, qn_ref, w_uqt_ref, kvn_ref, w_ukv_ref, w_uvt_ref, e_ref,
                 cm_ref, sm_ref, cd_ref, sd_ref, cmt_ref, smt_ref, cdt_ref, sdt_ref,
                 ckv_ref, kpe_ref, dk_ref, dv_ref, qt_ref, km_ref, vm_ref, vt_ref,
                 dqt_ref, dkb_ref, dvb_ref, dvt_ref):
    xb = x_ref[...].astype(BF16)
    tm = xb.shape[0]
    z = jnp.dot(xb, w_in_ref[...], preferred_element_type=F32)
    half_m = MLA_ROPE // 2
    half_d = DIFF_ROT // 2
    coef_k = _rope_coeffs(cm_ref[...], sm_ref[...], 0, half_m, LANES)
    coef_d = _rope_coeffs(cd_ref[...], sd_ref[...], 0, half_d, DIFF_D)

    cq = _rms(z[:, _CQ:_CQ + MLA_Q_RANK], qn_ref[...])
    qt = _nt_dot(w_uqt_ref[...], cq.astype(BF16)) * (MLA_SCALE * LOG2E)
    cos_mt, sin_mt = cmt_ref[...], smt_ref[...]
    for h in range(MLA_HEADS):
        r0 = h * HEAD_PAD
        qt_ref[0, r0:r0 + MLA_NOPE, :] = qt[r0:r0 + MLA_NOPE].astype(BF16)
        o1, o2 = _rope_rows(qt[r0 + MLA_NOPE:r0 + MLA_NOPE + MLA_ROPE], cos_mt, sin_mt)
        qt_ref[0, r0 + MLA_NOPE:r0 + MLA_NOPE + half_m, :] = o1.astype(BF16)
        qt_ref[0, r0 + MLA_NOPE + half_m:r0 + MLA_NOPE + MLA_ROPE, :] = o2.astype(BF16)
        qt_ref[0, r0 + MLA_NOPE + MLA_ROPE:r0 + HEAD_PAD, :] = jnp.zeros((HEAD_PAD - MLA_NOPE - MLA_ROPE, tm), BF16)

    ckv = _rms(z[:, _CKV:_CKV + MLA_KV_RANK], kvn_ref[...])
    ckv_ref[...] = ckv
    ckv_b = ckv.astype(BF16)
    kslab = _rope_apply(z[:, _KR:_KR + LANES], coef_k, half_m)
    kpe_ref[...] = kslab[:, :MLA_ROPE]
    kv = jnp.dot(ckv_b, w_ukv_ref[...], preferred_element_type=F32)
    k_full = kv[:, :MLA_HEADS * HEAD_PAD] + jnp.dot(kslab.astype(BF16), e_ref[...], preferred_element_type=F32)
    km_ref[...] = k_full.astype(BF16)
    vm_ref[...] = kv[:, MLA_HEADS * HEAD_PAD:].astype(BF16)
    ones_rows = jnp.where(lax.broadcasted_iota(jnp.int32, (BF16_ROWS, tm), 0) == 0, 1.0, 0.0).astype(BF16)
    vt = _nt_dot(w_uvt_ref[...], ckv_b)
    for h in range(MLA_HEADS):
        vt_ref[0, h * MLA_VA:h * MLA_VA + MLA_V, :] = vt[h * MLA_V:(h + 1) * MLA_V].astype(BF16)
        vt_ref[0, h * MLA_VA + MLA_V:(h + 1) * MLA_VA, :] = ones_rows

    dqt = _nt_dot(w_dqt_ref[...], xb) * (DIFF_SCALE * LOG2E)
    cos_dt, sin_dt = cdt_ref[...], sdt_ref[...]
    zeros = jnp.zeros((DIFF_D, tm), BF16)
    for n in range(2 * DIFF_HEADS):
        r0 = n * DIFF_D
        o1, o2 = _rope_rows(dqt[r0:r0 + DIFF_ROT], cos_dt, sin_dt)
        qn = jnp.concatenate([o1, o2, dqt[r0 + DIFF_ROT:r0 + DIFF_D]], axis=0).astype(BF16)
        lo, hi = (qn, zeros) if n % 2 == 0 else (zeros, qn)
        dqt_ref[0, n * LANES:n * LANES + DIFF_D, :] = lo
        dqt_ref[0, n * LANES + DIFF_D:(n + 1) * LANES, :] = hi

    dv = z[:, _DV:_DV + DIFF_HEADS * DIFF_V]
    dvb_ref[...] = dv.astype(BF16)
    dvt = dv.T
    for h in range(DIFF_HEADS):
        sl = slice(h * LANES, (h + 1) * LANES)
        dk = _rope_apply(z[:, _DK + h * LANES:_DK + (h + 1) * LANES], coef_d, half_d)
        dk_ref[:, h, :] = dk
        dkb_ref[:, sl] = dk.astype(BF16)
        dv_ref[:, h, :] = dv[:, sl]
        dvt_ref[0, h * DIFF_VA:h * DIFF_VA + DIFF_V, :] = dvt[sl].astype(BF16)
        dvt_ref[0, h * DIFF_VA + DIFF_V:(h + 1) * DIFF_VA, :] = ones_rows


def _proj(x, tabs, tabs_t, w, *, tm):
    t = x.shape[0]
    n_pos_blocks = tabs[0].shape[0] // tm
    row = lambda n: pl.BlockSpec((tm, n), lambda i: (i, 0))
    full = lambda a: pl.BlockSpec(a.shape, lambda i: (0,) * a.ndim)
    tab = pl.BlockSpec((tm, LANES), lambda i: (i % n_pos_blocks, 0))
    tab_t = lambda a: pl.BlockSpec((a.shape[0], tm), lambda i: (0, i % n_pos_blocks))
    tr = lambda n: pl.BlockSpec((1, n, tm), lambda i: (i, 0, 0))
    hd = lambda n: pl.BlockSpec((tm, DIFF_HEADS, n), lambda i: (i, 0, 0))
    weights = (w['w_in'], w['w_dqt'], w['q_norm'], w['w_uqt'], w['kv_norm'], w['w_ukv'], w['w_uvt'], w['e_place'])
    outs = ((row, MLA_KV_RANK, F32), (row, MLA_ROPE, F32), (hd, 2 * DIFF_D, F32), (hd, DIFF_V, F32),
            (tr, 1024, BF16), (row, 1024, BF16), (row, 512, BF16), (tr, MLA_HEADS * MLA_VA, BF16),
            (tr, 1024, BF16), (row, 512, BF16), (row, 512, BF16), (tr, DIFF_HEADS * DIFF_VA, BF16))
    shape = lambda kind, n: {tr: (t // tm, n, tm), hd: (t, DIFF_HEADS, n), row: (t, n)}[kind]
    return pl.pallas_call(
        _proj_kernel,
        grid=(t // tm,),
        in_specs=[row(D_MODEL)] + [full(a) for a in weights] + [tab] * 4 + [tab_t(a) for a in tabs_t],
        out_specs=[kind(n) for kind, n, _ in outs],
        out_shape=[jax.ShapeDtypeStruct(shape(kind, n), d) for kind, n, d in outs],
        compiler_params=_cparams(("parallel",)),
        name="proj",
    )(x, *weights, *tabs, *tabs_t)


def _expand_kernel(ckv_ref, kpe_ref, w_ukv_ref, e_ref, km_ref, vm_ref):
    kv = jnp.dot(ckv_ref[...].astype(BF16), w_ukv_ref[...], preferred_element_type=F32)
    k_full = kv[:, :MLA_HEADS * HEAD_PAD] + jnp.dot(kpe_ref[...].astype(BF16), e_ref[:MLA_ROPE, :],
                                                   preferred_element_type=F32)
    km_ref[...] = k_full.astype(BF16)
    vm_ref[...] = kv[:, MLA_HEADS * HEAD_PAD:].astype(BF16)


def _expand(ckv, kpe, w, *, tm):
    r = ckv.shape[0]
    row = lambda n: pl.BlockSpec((tm, n), lambda i: (i, 0))
    full = lambda a: pl.BlockSpec(a.shape, lambda i: (0,) * a.ndim)
    return pl.pallas_call(
        _expand_kernel,
        grid=(r // tm,),
        in_specs=[row(MLA_KV_RANK), row(MLA_ROPE), full(w['w_ukv']), full(w['e_place'])],
        out_specs=[row(1024), row(512)],
        out_shape=[jax.ShapeDtypeStruct((r, 1024), BF16), jax.ShapeDtypeStruct((r, 512), BF16)],
        compiler_params=_cparams(("parallel",)),
        name="expand",
    )(ckv, kpe, w['w_ukv'], w['e_place'])


def _chunk_mask_t(t):
    shift = CHUNK.bit_length() - 1
    kc = lax.broadcasted_iota(jnp.int32, (t, t), 0) >> shift
    qc = lax.broadcasted_iota(jnp.int32, (t, t), 1) >> shift
    return kc <= qc


def _flash_scratch(t, dv, n):
    return [pltpu.VMEM((n, t, t), F32), pltpu.VMEM((n, t, t), F32),
            pltpu.VMEM((n, 1, t), F32), pltpu.VMEM((n, dv, t), F32)]


def _flash_loop(i, score, value, n, s_a, s_b, m_ref, acc_ref, ahead):
    t = s_a.shape[-1]
    mask = _chunk_mask_t(t)
    m_ref[...] = jnp.full(m_ref.shape, -jnp.inf, F32)
    acc_ref[...] = jnp.zeros(acc_ref.shape, F32)
    first, second = tuple(range(n)), tuple(range(n, 2 * n))
    both = first + second

    def step(j, s_cur, s_nxt, cur, masked, nxt):
        pending = list(nxt)

        def issue_scores(count):
            for k in pending[:count]:
                s_nxt[k] = score(j + 1, k // n, k % n)
            del pending[:count]

        issue_scores(ahead)
        for k in cur:
            issue_scores(1)
            s = s_cur[k]
            if k in masked:
                s = jnp.where(mask, s, -jnp.inf)
            m = m_ref[k]
            m_new = jnp.maximum(m, jnp.max(s, axis=0, keepdims=True))
            alpha = jnp.exp2(m - m_new)
            p = jnp.exp2(s - m_new)
            m_ref[k] = m_new
            acc_ref[k] = alpha * acc_ref[k] + jnp.dot(value(j, k % n), p.astype(BF16), preferred_element_type=F32)
        issue_scores(len(pending))

    for k in both:
        s_a[k] = score(0, k // n, k % n)

    def pair(jj, carry):
        step(2 * jj, s_a, s_b, both, (), both)
        step(2 * jj + 1, s_b, s_a, both, (), both)
        return carry

    lax.fori_loop(0, i, pair, 0)
    step(2 * i, s_a, s_b, both, first, second)
    step(2 * i + 1, s_b, None, second, second, ())


def _normalised(acc_ref, k, dv):
    return acc_ref[k, :dv, :] / acc_ref[k, dv:dv + 1, :]


def _mla_flash_kernel(qt_ref, k_ref, vt_ref, o_ref, s_a, s_b, m_ref, acc_ref, *, t, heads):
    def score(j, half, h):
        rows = pl.ds(pl.multiple_of(j * t, t), t)
        return jnp.dot(k_ref[0, rows, h * HEAD_PAD:(h + 1) * HEAD_PAD],
                       qt_ref[half, h * HEAD_PAD:(h + 1) * HEAD_PAD, :], preferred_element_type=F32)

    value = lambda j, h: vt_ref[j, h * MLA_VA:(h + 1) * MLA_VA, :]
    _flash_loop(pl.program_id(2), score, value, heads, s_a, s_b, m_ref, acc_ref, ahead=2)
    for half in range(2):
        for h in range(0, heads, 2):
            k = half * heads + h
            pair = jnp.concatenate([_normalised(acc_ref, k, MLA_V), _normalised(acc_ref, k + 1, MLA_V)], axis=0)
            o_ref[0, half * t:(half + 1) * t, h * MLA_V:(h + 2) * MLA_V] = pair.T.astype(BF16)


def _mla_flash(qt, km, vt, *, t, heads):
    b, s, _ = km.shape
    nk = s // t
    nq2 = nk // 2
    return pl.pallas_call(
        functools.partial(_mla_flash_kernel, t=t, heads=heads),
        grid=(b, MLA_HEADS // heads, nq2),
        in_specs=[pl.BlockSpec((2, heads * HEAD_PAD, t), lambda bi, hi, i: (bi * nq2 + i, hi, 0)),
                  pl.BlockSpec((1, s, heads * HEAD_PAD), lambda bi, hi, i: (bi, 0, hi)),
                  pl.BlockSpec((nk, heads * MLA_VA, t), lambda bi, hi, i: (bi, hi, 0))],
        out_specs=pl.BlockSpec((1, 2 * t, heads * MLA_V), lambda bi, hi, i: (bi, i, hi)),
        out_shape=jax.ShapeDtypeStruct((b, s, MLA_HEADS * MLA_V), BF16),
        scratch_shapes=_flash_scratch(t, MLA_VA, 2 * heads),
        compiler_params=_cparams(("parallel", "parallel", "arbitrary")),
        name="mla_flash",
    )(qt, km, vt)


def _diff_lambda(lam_ref, lam_init):
    lq1, lk1, lq2, lk2 = (lam_ref[r:r + 1, :] for r in range(4))
    return (jnp.exp(jnp.sum(lq1 * lk1, axis=-1, keepdims=True))
            - jnp.exp(jnp.sum(lq2 * lk2, axis=-1, keepdims=True)) + lam_init)


def _diff_finish(o1, o2, lam, subln, lam_init):
    o = o1 - lam * o2
    return _rms(o, subln) * (1.0 - lam_init)


def _diff_flash_kernel(lam_ref, subln_ref, qt_ref, k_ref, vt_ref, o_ref, s_a, s_b, m_ref, acc_ref,
                       *, t, heads, lam_init):
    def score(j, half, n):
        rows = pl.ds(pl.multiple_of(j * t, t), t)
        return jnp.dot(k_ref[0, rows, (n // 2) * LANES:(n // 2 + 1) * LANES],
                       qt_ref[half, n * LANES:(n + 1) * LANES, :], preferred_element_type=F32)

    value = lambda j, n: vt_ref[j, (n // 2) * DIFF_VA:(n // 2 + 1) * DIFF_VA, :]
    _flash_loop(pl.program_id(2), score, value, 2 * heads, s_a, s_b, m_ref, acc_ref, ahead=1)
    lam = _diff_lambda(lam_ref, lam_init)
    for half in range(2):
        for h in range(heads):
            k = half * 2 * heads + 2 * h
            o1 = _normalised(acc_ref, k, DIFF_V).T
            o2 = _normalised(acc_ref, k + 1, DIFF_V).T
            o = _diff_finish(o1, o2, lam, subln_ref[...], lam_init)
            o_ref[0, half * t:(half + 1) * t, h * DIFF_V:(h + 1) * DIFF_V] = o.astype(BF16)


def _diff_flash(lam_rows, subln, dqt, dkb, dvt, *, t, heads, lam_init):
    b, s, _ = dkb.shape
    nk = s // t
    nq2 = nk // 2
    full = lambda a: pl.BlockSpec(a.shape, lambda bi, hi, i: (0,) * a.ndim)
    return pl.pallas_call(
        functools.partial(_diff_flash_kernel, t=t, heads=heads, lam_init=lam_init),
        grid=(b, DIFF_HEADS // heads, nq2),
        in_specs=[full(lam_rows), full(subln),
                  pl.BlockSpec((2, heads * 2 * LANES, t), lambda bi, hi, i: (bi * nq2 + i, hi, 0)),
                  pl.BlockSpec((1, s, heads * LANES), lambda bi, hi, i: (bi, 0, hi)),
                  pl.BlockSpec((nk, heads * DIFF_VA, t), lambda bi, hi, i: (bi, hi, 0))],
        out_specs=pl.BlockSpec((1, 2 * t, heads * DIFF_V), lambda bi, hi, i: (bi, i, hi)),
        out_shape=jax.ShapeDtypeStruct((b, s, DIFF_HEADS * DIFF_V), BF16),
        scratch_shapes=_flash_scratch(t, DIFF_VA, 4 * heads),
        compiler_params=_cparams(("parallel", "parallel", "arbitrary")),
        name="diff_flash",
    )(lam_rows, subln, dqt, dkb, dvt)


def _two_part_softmax_pv(s_past, s_new, v_past, v_new):
    m = jnp.maximum(jnp.max(s_past, axis=-1, keepdims=True), jnp.max(s_new, axis=-1, keepdims=True))
    p_past = jnp.exp2(s_past - m)
    p_new = jnp.exp2(s_new - m)
    l = jnp.sum(p_past, axis=-1, keepdims=True) + jnp.sum(p_new, axis=-1, keepdims=True)
    acc = (jnp.dot(p_past.astype(BF16), v_past, preferred_element_type=F32)
           + jnp.dot(p_new.astype(BF16), v_new, preferred_element_type=F32))
    return acc / l


def _token_major(qt_ref):
    return qt_ref[0].astype(F32).T.astype(BF16)


def _mla_step_kernel(qt_ref, kp_ref, vp_ref, kn_ref, vn_ref, o_ref):
    q_all = _token_major(qt_ref)
    for h in range(MLA_HEADS):
        ks = slice(h * HEAD_PAD, (h + 1) * HEAD_PAD)
        vs = slice(h * MLA_V, (h + 1) * MLA_V)
        q = q_all[:, ks]
        o = _two_part_softmax_pv(_nt_dot(q, kp_ref[0, :, ks]), _nt_dot(q, kn_ref[0, :, ks]),
                                 vp_ref[0, :, vs], vn_ref[0, :, vs])
        o_ref[0, :, vs] = o.astype(BF16)


def _mla_step(qt, km_past, vm_past, km_new, vm_new):
    b, _, n = qt.shape
    blk = lambda a: pl.BlockSpec((1,) + a.shape[1:], lambda bi: (bi, 0, 0))
    args = (qt, km_past, vm_past, km_new, vm_new)
    return pl.pallas_call(
        _mla_step_kernel,
        grid=(b,),
        in_specs=[blk(a) for a in args],
        out_specs=pl.BlockSpec((1, n, MLA_HEADS * MLA_V), lambda bi: (bi, 0, 0)),
        out_shape=jax.ShapeDtypeStruct((b, n, MLA_HEADS * MLA_V), BF16),
        compiler_params=_cparams(("parallel",)),
        name="mla_step",
    )(*args)


def _diff_step_kernel(lam_ref, subln_ref, qt_ref, kp_ref, vp_ref, kn_ref, vn_ref, o_ref, *, lam_init):
    lam = _diff_lambda(lam_ref, lam_init)
    q = _token_major(qt_ref)
    for h in range(DIFF_HEADS):
        sl = slice(h * LANES, (h + 1) * LANES)
        q1 = q[:, 2 * h * LANES:(2 * h + 1) * LANES]
        q2 = q[:, (2 * h + 1) * LANES:(2 * h + 2) * LANES]
        kp = kp_ref[0, :, h, :].astype(BF16)
        vp = vp_ref[0, :, h, :].astype(BF16)
        kn = kn_ref[0, :, sl]
        vn = vn_ref[0, :, sl]
        o1 = _two_part_softmax_pv(_nt_dot(q1, kp), _nt_dot(q1, kn), vp, vn)
        o2 = _two_part_softmax_pv(_nt_dot(q2, kp), _nt_dot(q2, kn), vp, vn)
        o_ref[0, :, sl] = _diff_finish(o1, o2, lam, subln_ref[...], lam_init).astype(BF16)


def _diff_step(lam_rows, subln, dqt, k_past, v_past, dkb, dvb, *, lam_init):
    b, _, n = dqt.shape
    full = lambda a: pl.BlockSpec(a.shape, lambda bi: (0,) * a.ndim)
    blk = lambda a: pl.BlockSpec((1,) + a.shape[1:], lambda bi: (bi,) + (0,) * (a.ndim - 1))
    args = (dqt, k_past, v_past, dkb, dvb)
    return pl.pallas_call(
        functools.partial(_diff_step_kernel, lam_init=lam_init),
        grid=(b,),
        in_specs=[full(lam_rows), full(subln)] + [blk(a) for a in args],
        out_specs=pl.BlockSpec((1, n, DIFF_HEADS * DIFF_V), lambda bi: (bi, 0, 0)),
        out_shape=jax.ShapeDtypeStruct((b, n, DIFF_HEADS * DIFF_V), BF16),
        compiler_params=_cparams(("parallel",)),
        name="diff_step",
    )(lam_rows, subln, *args)


def _first_index_of_max(vals, lane):
    m = jnp.max(vals, axis=-1, keepdims=True)
    idx = jnp.min(jnp.where(vals == m, lane, float(LANES)), axis=-1, keepdims=True)
    return m, idx


_SLAB_MEMBER = EXP_PER_GROUP
_SLAB_RANK = EXP_PER_GROUP + 1


def _route(hb, wr_ref, br_ref):
    logit = jnp.dot(hb, wr_ref[...], preferred_element_type=F32) + br_ref[...]
    lane_i = lax.broadcasted_iota(jnp.int32, logit.shape, 1)
    lane = lane_i.astype(F32)
    group_of_lane = (lane_i >> (EXP_PER_GROUP.bit_length() - 1)).astype(F32)
    neg = -jnp.inf
    g_logit = jnp.where((lane_i >= N_EXPERTS) & (lane_i < N_EXPERTS + N_GROUPS), logit, neg)
    g_max, g_lane = _first_index_of_max(g_logit, lane)
    g_w = 1.0 / jnp.sum(jnp.exp(g_logit - g_max), axis=-1, keepdims=True)
    g_idx = g_lane - float(N_EXPERTS)
    e_logit = jnp.where(group_of_lane == g_idx, logit, neg)
    m1, i1 = _first_index_of_max(e_logit, lane)
    m2, i2 = _first_index_of_max(jnp.where(lane == i1, neg, e_logit), lane)
    r = jnp.exp(m2 - m1)
    p1 = 1.0 / (1.0 + r)
    p2 = r / (1.0 + r)
    gate = g_w * (jnp.where(lane == i1, p1, 0.0) + jnp.where(lane == i2, p2, 0.0))
    return gate, g_idx, jnp.where(lane == g_idx, 1.0, 0.0)


def _post_attn_kernel(x_ref, mla_ref, diff_ref, wo_a_ref, wo_b_ref, g_ref, b_ref, wr_ref, br_ref, ltri_ref, utri_ref,
                      h_ref, hb_ref, slab_ref, rank_ref, cnt_ref, *, alpha):
    a = (jnp.dot(mla_ref[...], wo_a_ref[...], preferred_element_type=F32)
         + jnp.dot(diff_ref[...], wo_b_ref[...], preferred_element_type=F32))
    h = _layer_norm(alpha * x_ref[...] + a, g_ref[...], b_ref[...])
    h_ref[...] = h
    hb = h.astype(BF16)
    hb_ref[...] = hb
    gate, g_idx, onehot = _route(hb, wr_ref, br_ref)
    lane_i = lax.broadcasted_iota(jnp.int32, gate.shape, 1)

    rank_c = jnp.dot(ltri_ref[...], onehot.astype(BF16), preferred_element_type=F32)
    onehot_t = onehot.T[:BF16_ROWS]
    rank_r = jnp.dot(onehot_t.astype(BF16), utri_ref[...], preferred_element_type=F32)
    rank_ref[0] = jnp.where(onehot_t > 0.0, rank_r, -1.0)
    cnt_ref[0] = jnp.broadcast_to(jnp.sum(onehot, axis=0, keepdims=True), cnt_ref.shape[1:])

    for g in range(N_GROUPS):
        member = g_idx == float(g)
        rank_g = jnp.sum(jnp.where(lane_i == g, rank_c, 0.0), axis=-1, keepdims=True)
        rolled = gate if g == 0 else pltpu.roll(gate, LANES - EXP_PER_GROUP * g, 1)
        extra = jnp.where(lane_i == _SLAB_MEMBER, jnp.where(member, 1.0, 0.0),
                          jnp.where(lane_i == _SLAB_RANK, jnp.where(member, rank_g, -1.0), 0.0))
        slab_ref[:, g * LANES:(g + 1) * LANES] = jnp.where(lane_i < EXP_PER_GROUP, rolled, extra)


def _post_attn(x, mla_o, diff_o, w, *, tm, alpha):
    t = x.shape[0]
    nt = t // tm
    row = lambda n: pl.BlockSpec((tm, n), lambda i: (i, 0))
    full = lambda a: pl.BlockSpec(a.shape, lambda i: (0,) * a.ndim)
    per_tile = lambda r, c: pl.BlockSpec((1, r, c), lambda i: (i, 0, 0))
    idx = jnp.arange(tm)
    ltri = (idx[None, :] < idx[:, None]).astype(BF16)
    weights = (w['w_out_a'], w['w_out_b'], w['ln1_g'], w['ln1_b'], w['w_route'], w['b_route'], ltri, ltri.T)
    return pl.pallas_call(
        functools.partial(_post_attn_kernel, alpha=alpha),
        grid=(nt,),
        in_specs=[row(D_MODEL), row(512), row(512)] + [full(a) for a in weights],
        out_specs=[row(D_MODEL), row(D_MODEL), row(N_GROUPS * LANES), per_tile(BF16_ROWS, tm), per_tile(8, LANES)],
        out_shape=[jax.ShapeDtypeStruct((t, D_MODEL), F32), jax.ShapeDtypeStruct((t, D_MODEL), BF16),
                   jax.ShapeDtypeStruct((t, N_GROUPS * LANES), F32),
                   jax.ShapeDtypeStruct((nt, BF16_ROWS, tm), F32), jax.ShapeDtypeStruct((nt, 8, LANES), F32)],
        compiler_params=_cparams(("parallel",)),
        name="post_attn",
    )(x, mla_o, diff_o, *weights)


MOE_TILE = 1024
MOE_SLACK_ROWS = 32
MOE_EXTRA_ROWS = 128


def _split3(x):
    hi = x.astype(BF16)
    r1 = x - hi.astype(F32)
    mid = r1.astype(BF16)
    lo = (r1 - mid.astype(F32)).astype(BF16)
    return hi, mid, lo


def _moe_kernel(cnt_ref, hb_ref, slab_ref, rank_ref, wg_ref, wu_ref, wd_ref, yin_ref, g_ref, b_ref, y_ref,
                *, g, alpha):
    n_tok = cnt_ref[pl.program_id(0), g]
    slab = slab_ref[...]
    rank_col = slab[:, _SLAB_RANK:_SLAB_RANK + 1]
    rank_row = rank_ref[0, g:g + 1, :]
    gate3 = _split3(slab)

    y_ref[...] = alpha * yin_ref[...] if g == 0 else yin_ref[...]

    def one_pass(base, rows):
        sub = lax.broadcasted_iota(jnp.int32, (rows, 1), 0).astype(F32)
        lan = lax.broadcasted_iota(jnp.int32, (1, rows), 1).astype(F32)
        sel = jnp.where(rank_row - base == sub, 1.0, 0.0).astype(BF16)
        sel_t = jnp.where(rank_col - base == lan, 1.0, 0.0).astype(BF16)
        x = jnp.dot(sel, hb_ref[...], preferred_element_type=F32).astype(BF16)
        gate_c = sum(jnp.dot(sel, part, preferred_element_type=F32) for part in gate3)
        ys = jnp.zeros((rows, D_MODEL), F32)
        for e in range(EXP_PER_GROUP):
            hid = (jax.nn.silu(jnp.dot(x, wg_ref[e], preferred_element_type=F32))
                   * jnp.dot(x, wu_ref[e], preferred_element_type=F32))
            ys = ys + jnp.dot((hid * gate_c[:, e:e + 1]).astype(BF16), wd_ref[e], preferred_element_type=F32)
        ys_hi = ys.astype(BF16)
        ys_lo = (ys - ys_hi.astype(F32)).astype(BF16)
        y_ref[...] += (jnp.dot(sel_t, ys_hi, preferred_element_type=F32)
                       + jnp.dot(sel_t, ys_lo, preferred_element_type=F32))

    first_rows = slab.shape[0] // N_GROUPS + MOE_SLACK_ROWS
    one_pass(0.0, first_rows)

    def extra_pass(c, carry):
        one_pass((first_rows + c * MOE_EXTRA_ROWS).astype(F32), MOE_EXTRA_ROWS)
        return carry

    lax.fori_loop(0, (jnp.maximum(n_tok - first_rows, 0) + MOE_EXTRA_ROWS - 1) // MOE_EXTRA_ROWS, extra_pass, 0)

    if g == N_GROUPS - 1:
        y_ref[...] = _layer_norm(y_ref[...], g_ref[...], b_ref[...])


def _moe_group(g, y, hb, slab, rank_r, cnt, w, *, tm, alpha):
    t = y.shape[0]
    row = lambda n: pl.BlockSpec((tm, n), lambda i, c: (i, 0))
    full = lambda a: pl.BlockSpec(a.shape, lambda i, c: (0,) * a.ndim)
    grp = lambda a: pl.BlockSpec((EXP_PER_GROUP,) + a.shape[1:], lambda i, c: (g, 0, 0),
                                 pipeline_mode=pl.Buffered(1))
    grid_spec = pltpu.PrefetchScalarGridSpec(
        num_scalar_prefetch=1,
        grid=(t // tm,),
        in_specs=[row(D_MODEL), pl.BlockSpec((tm, LANES), lambda i, c: (i, g)),
                  pl.BlockSpec((1, BF16_ROWS, tm), lambda i, c: (i, 0, 0)),
                  grp(w['w_gate']), grp(w['w_up']), grp(w['w_down']), row(D_MODEL),
                  full(w['ln2_g']), full(w['ln2_b'])],
        out_specs=row(D_MODEL))
    return pl.pallas_call(
        functools.partial(_moe_kernel, g=g, alpha=alpha),
        grid_spec=grid_spec,
        out_shape=jax.ShapeDtypeStruct((t, D_MODEL), F32),
        compiler_params=_cparams(("arbitrary",)),
        name=f"moe_group{g}",
    )(cnt, hb, slab, rank_r, w['w_gate'], w['w_up'], w['w_down'], y, w['ln2_g'], w['ln2_b'])


def _moe(h, hb, slab, rank_r, cnt, w, *, tm, alpha):
    y = h
    for g in range(N_GROUPS):
        y = _moe_group(g, y, hb, slab, rank_r, cnt, w, tm=tm, alpha=alpha)
    return y


def _rope_tables(pos):
    pos = pos.astype(F32)[:, None]
    lane = jnp.arange(LANES)
    inv_m = ROPE_THETA ** (-jnp.arange(0, MLA_ROPE, 2, dtype=F32) / MLA_ROPE)
    ang_m = pos * inv_m
    inv_d = ROPE_THETA ** (-jnp.arange(0, DIFF_ROT, 2, dtype=F32) / DIFF_ROT)
    ang_d = pos * inv_d
    narrow = lax.optimization_barrier((jnp.cos(ang_m), jnp.sin(ang_m), jnp.cos(ang_d), jnp.sin(ang_d)))
    tabs = tuple(a[:, lane % a.shape[1]] for a in narrow)
    tabs_t = tuple(a.T for a in narrow)
    return tabs, tabs_t


def _prep_weights(l, w_in, mla_q_norm, mla_w_uq, mla_kv_norm, mla_w_ukv, diff_subln, w_out, ln1_g, ln1_b,
                  ln2_g, ln2_b, w_route_group, b_route_group, w_route_expert, b_route_expert,
                  w_exp_gate, w_exp_up, w_exp_down, lam_rows):
    wi = w_in[l]
    s0, s1, s2, s3, s4 = 256, 512, 544, 1056, 1568
    w_in_r = jnp.concatenate([wi[:, :s0], wi[:, s0:s1], wi[:, s3:s4], wi[:, s4:],
                              wi[:, s1:s2], jnp.zeros((D_MODEL, LANES - MLA_ROPE), F32)], axis=1)
    w_uq = jnp.pad(mla_w_uq[l].reshape(MLA_Q_RANK, MLA_HEADS, MLA_NOPE + MLA_ROPE),
                   ((0, 0), (0, 0), (0, HEAD_PAD - MLA_NOPE - MLA_ROPE))).reshape(MLA_Q_RANK, -1)
    ukv = mla_w_ukv[l].reshape(MLA_KV_RANK, MLA_HEADS, MLA_NOPE + MLA_V)
    w_uk = jnp.pad(ukv[..., :MLA_NOPE], ((0, 0), (0, 0), (0, HEAD_PAD - MLA_NOPE))).reshape(MLA_KV_RANK, -1)
    w_uv = ukv[..., MLA_NOPE:].reshape(MLA_KV_RANK, -1)
    rows = jnp.arange(LANES)[:, None]
    cols = jnp.arange(MLA_HEADS * HEAD_PAD)[None, :]
    e_place = ((rows < MLA_ROPE) & (cols % HEAD_PAD == rows + MLA_NOPE)).astype(BF16)
    w_route = jnp.concatenate([w_route_expert[l], w_route_group[l],
                               jnp.zeros((D_MODEL, LANES - N_EXPERTS - N_GROUPS), F32)], axis=1)
    b_route = jnp.concatenate([b_route_expert[l], b_route_group[l],
                               jnp.zeros((LANES - N_EXPERTS - N_GROUPS,), F32)])[None, :]
    return dict(
        w_in=w_in_r.astype(BF16), w_dqt=wi[:, s2:s3].T.astype(BF16),
        q_norm=mla_q_norm[l][None, :], w_uqt=w_uq.T.astype(BF16),
        kv_norm=mla_kv_norm[l][None, :], w_ukv=jnp.concatenate([w_uk, w_uv], axis=1).astype(BF16),
        w_uvt=w_uv.T.astype(BF16),
        e_place=e_place, subln=diff_subln[l][None, :], lam_rows=lam_rows,
        w_out_a=w_out[l][:MLA_HEADS * MLA_V].astype(BF16), w_out_b=w_out[l][MLA_HEADS * MLA_V:].astype(BF16),
        ln1_g=ln1_g[l][None, :], ln1_b=ln1_b[l][None, :], ln2_g=ln2_g[l][None, :], ln2_b=ln2_b[l][None, :],
        w_route=w_route.astype(BF16), b_route=b_route,
        w_gate=w_exp_gate[l].astype(BF16), w_up=w_exp_up[l].astype(BF16), w_down=w_exp_down[l].astype(BF16))


def _block_output(x, mla_o, diff_o, w, alpha):
    t = x.shape[0]
    tm = min(MOE_TILE, t)
    h, hb, slab, rank_r, cnt = _post_attn(x, mla_o.reshape(t, -1), diff_o.reshape(t, -1), w, tm=tm, alpha=alpha)
    cnt = cnt[:, 0, :N_GROUPS].astype(jnp.int32)
    return _moe(h, hb, slab, rank_r, cnt, w, tm=tm, alpha=alpha)


def kernel(x_prompt, x_sample, cache_mla_ckv, cache_mla_kpe, cache_diff_k, cache_diff_v, w_in, mla_q_norm, mla_w_uq, mla_kv_norm, mla_w_ukv, diff_lambda_q1, diff_lambda_k1, diff_lambda_q2, diff_lambda_k2, diff_subln, w_out, ln1_g, ln1_b, ln2_g, ln2_b, w_route_group, b_route_group, w_route_expert, b_route_expert, w_exp_gate, w_exp_up, w_exp_down):
    depth = w_in.shape[0]
    bp, sp, _ = x_prompt.shape
    bs, ss, _ = x_sample.shape
    past = cache_mla_ckv.shape[2]
    alpha = (2.0 * depth) ** 0.25
    tabs_p, tabs_pt = _rope_tables(jnp.arange(sp))
    tabs_s, tabs_st = _rope_tables(past + jnp.arange(ss))
    hp = x_prompt.reshape(bp * sp, D_MODEL)
    hs = x_sample.reshape(bs * ss, D_MODEL)
    outs = [[] for _ in range(8)]
    for l in range(depth):
        lam_init = 0.8 - 0.6 * math.exp(-0.3 * l)
        lam_rows = jnp.stack([diff_lambda_q1[l], diff_lambda_k1[l], diff_lambda_q2[l], diff_lambda_k2[l]])
        w = _prep_weights(l, w_in, mla_q_norm, mla_w_uq, mla_kv_norm, mla_w_ukv, diff_subln, w_out,
                          ln1_g, ln1_b, ln2_g, ln2_b, w_route_group, b_route_group, w_route_expert,
                          b_route_expert, w_exp_gate, w_exp_up, w_exp_down, lam_rows)

        ckv, kpe, dk, dv, qt, km, _, vt, dqt, dkb, _, dvt = _proj(hp, tabs_p, tabs_pt, w, tm=ATTN_TILE)
        b3 = lambda a: a.reshape(bp, sp, -1)
        mla_o = _mla_flash(qt, b3(km), vt, t=ATTN_TILE, heads=4)
        diff_o = _diff_flash(w['lam_rows'], w['subln'], dqt, b3(dkb), dvt, t=ATTN_TILE, heads=2,
                             lam_init=lam_init)
        outs[0].append(ckv.reshape(bp, sp, MLA_KV_RANK))
        outs[1].append(kpe.reshape(bp, sp, MLA_ROPE))
        outs[2].append(dk.reshape(bp, sp, DIFF_HEADS, 2 * DIFF_D))
        outs[3].append(dv.reshape(bp, sp, DIFF_HEADS, DIFF_V))
        hp = _block_output(hp, mla_o, diff_o, w, alpha)

        ckv_s, kpe_s, dk_s, dv_s, qt_s, km_s, vm_s, _, dqt_s, dkb_s, dvb_s, _ = _proj(hs, tabs_s, tabs_st, w, tm=ss)
        km_past, vm_past = _expand(cache_mla_ckv[l].reshape(bs * past, MLA_KV_RANK),
                                   cache_mla_kpe[l].reshape(bs * past, MLA_ROPE), w, tm=512)
        s3 = lambda a: a.reshape(bs, ss, -1)
        p3 = lambda a: a.reshape(bs, past, -1)
        mla_o_s = _mla_step(qt_s, p3(km_past), p3(vm_past), s3(km_s), s3(vm_s))
        diff_o_s = _diff_step(w['lam_rows'], w['subln'], dqt_s, cache_diff_k[l], cache_diff_v[l],
                              s3(dkb_s), s3(dvb_s), lam_init=lam_init)
        outs[4].append(ckv_s.reshape(bs, ss, MLA_KV_RANK))
        outs[5].append(kpe_s.reshape(bs, ss, MLA_ROPE))
        outs[6].append(dk_s.reshape(bs, ss, DIFF_HEADS, 2 * DIFF_D))
        outs[7].append(dv_s.reshape(bs, ss, DIFF_HEADS, DIFF_V))
        hs = _block_output(hs, mla_o_s, diff_o_s, w, alpha)

    return (hp.reshape(bp, sp, D_MODEL), hs.reshape(bs, ss, D_MODEL)) + tuple(jnp.stack(o) for o in outs)
```

```python
import functools
import math

import jax
import jax.numpy as jnp
from jax import lax
from jax.experimental import pallas as pl
from jax.experimental.pallas import tpu as pltpu

F32 = jnp.float32
BF16 = jnp.bfloat16

D_MODEL = 1024
CHUNK = 64
ROPE_THETA = 500000.0
MLA_HEADS = 8
MLA_NOPE = 64
MLA_ROPE = 32
MLA_V = 64
MLA_Q_RANK = 256
MLA_KV_RANK = 256
MLA_SCALE = (MLA_NOPE + MLA_ROPE) ** -0.5
DIFF_HEADS = 4
DIFF_D = 64
DIFF_V = 2 * DIFF_D
DIFF_ROT = DIFF_D // 4
DIFF_SCALE = DIFF_D ** -0.5
N_GROUPS = 4
EXP_PER_GROUP = 8
N_EXPERTS = N_GROUPS * EXP_PER_GROUP
D_EXPERT = 256
EPS_LN = 1e-5
EPS_RMS = 1e-6
LOG2E = math.log2(math.e)

LANES = 128
HEAD_PAD = LANES
ATTN_TILE = 256
ATTN_QTILES = 4
BF16_ROWS = 16
MLA_VA = MLA_V + BF16_ROWS
DIFF_VA = DIFF_V + BF16_ROWS
VMEM_LIMIT = 56 * 1024 * 1024

_CQ, _CKV, _DK, _DV, _KR, _IN_COLS_R = 0, 256, 512, 1024, 1536, 1664


def _cparams(sem):
    return pltpu.CompilerParams(dimension_semantics=sem, vmem_limit_bytes=VMEM_LIMIT)


def _rms(x, g):
    return x * lax.rsqrt(jnp.mean(x * x, axis=-1, keepdims=True) + EPS_RMS) * g


def _layer_norm(x, g, b):
    mu = jnp.mean(x, axis=-1, keepdims=True)
    xc = x - mu
    var = jnp.mean(xc * xc, axis=-1, keepdims=True)
    return xc * lax.rsqrt(var + EPS_LN) * g + b


def _nt_dot(a, b):
    return lax.dot_general(a, b, (((1,), (1,)), ((), ())), preferred_element_type=F32)


def _rope_coeffs(cos_t, sin_t, lo, half, period):
    lane = lax.broadcasted_iota(jnp.int32, cos_t.shape, 1) & (period - 1)
    is1 = (lane >= lo) & (lane < lo + half)
    is2 = (lane >= lo + half) & (lane < lo + 2 * half)
    c = jnp.where(is1 | is2, cos_t, 1.0)
    a = jnp.where(is1, -sin_t, 0.0)
    b = jnp.where(is2, sin_t, 0.0)
    return c, a, b


def _rope_apply(x, coeffs, half):
    c, a, b = coeffs
    return x * c + pltpu.roll(x, LANES - half, 1) * a + pltpu.roll(x, half, 1) * b


def _rope_rows(x, cos_t, sin_t):
    half = cos_t.shape[0]
    x1, x2 = x[:half], x[half:]
    return x1 * cos_t - x2 * sin_t, x1 * sin_t + x2 * cos_t


def _proj_kernel(x_ref, w_in_ref, w_dqt_ref, qn_ref, w_uqt_ref, kvn_ref, w_ukv_ref, w_uvt_ref, e_ref,
                 cm_ref, sm_ref, cd_ref, sd_ref, cmt_ref, smt_ref, cdt_ref, sdt_ref,
                 ckv_ref, kpe_ref, dk_ref, dv_ref, qt_ref, km_ref, vm_ref, vt_ref,
                 dqt_ref, dkb_ref, dvb_ref, dvt_ref):
    xb = x_ref[...].astype(BF16)
    tm = xb.shape[0]
    z = jnp.dot(xb, w_in_ref[...], preferred_element_type=F32)
    half_m = MLA_ROPE // 2
    half_d = DIFF_ROT // 2
    coef_k = _rope_coeffs(cm_ref[...], sm_ref[...], 0, half_m, LANES)
    coef_d = _rope_coeffs(cd_ref[...], sd_ref[...], 0, half_d, DIFF_D)

    cq = _rms(z[:, _CQ:_CQ + MLA_Q_RANK], qn_ref[...])
    qt = _nt_dot(w_uqt_ref[...], cq.astype(BF16)) * (MLA_SCALE * LOG2E)
    cos_mt, sin_mt = cmt_ref[...], smt_ref[...]
    for h in range(MLA_HEADS):
        r0 = h * HEAD_PAD
        qt_ref[0, r0:r0 + MLA_NOPE, :] = qt[r0:r0 + MLA_NOPE].astype(BF16)
        o1, o2 = _rope_rows(qt[r0 + MLA_NOPE:r0 + MLA_NOPE + MLA_ROPE], cos_mt, sin_mt)
        qt_ref[0, r0 + MLA_NOPE:r0 + MLA_NOPE + half_m, :] = o1.astype(BF16)
        qt_ref[0, r0 + MLA_NOPE + half_m:r0 + MLA_NOPE + MLA_ROPE, :] = o2.astype(BF16)
        qt_ref[0, r0 + MLA_NOPE + MLA_ROPE:r0 + HEAD_PAD, :] = jnp.zeros((HEAD_PAD - MLA_NOPE - MLA_ROPE, tm), BF16)

    ckv = _rms(z[:, _CKV:_CKV + MLA_KV_RANK], kvn_ref[...])
    ckv_ref[...] = ckv
    ckv_b = ckv.astype(BF16)
    kslab = _rope_apply(z[:, _KR:_KR + LANES], coef_k, half_m)
    kpe_ref[...] = kslab[:, :MLA_ROPE]
    kv = jnp.dot(ckv_b, w_ukv_ref[...], preferred_element_type=F32)
    k_full = kv[:, :MLA_HEADS * HEAD_PAD] + jnp.dot(kslab.astype(BF16), e_ref[...], preferred_element_type=F32)
    km_ref[...] = k_full.astype(BF16)
    vm_ref[...] = kv[:, MLA_HEADS * HEAD_PAD:].astype(BF16)
    ones_rows = jnp.where(lax.broadcasted_iota(jnp.int32, (BF16_ROWS, tm), 0) == 0, 1.0, 0.0).astype(BF16)
    vt = _nt_dot(w_uvt_ref[...], ckv_b)
    for h in range(MLA_HEADS):
        vt_ref[0, h * MLA_VA:h * MLA_VA + MLA_V, :] = vt[h * MLA_V:(h + 1) * MLA_V].astype(BF16)
        vt_ref[0, h * MLA_VA + MLA_V:(h + 1) * MLA_VA, :] = ones_rows

    dqt = _nt_dot(w_dqt_ref[...], xb) * (DIFF_SCALE * LOG2E)
    cos_dt, sin_dt = cdt_ref[...], sdt_ref[...]
    zeros = jnp.zeros((DIFF_D, tm), BF16)
    for n in range(2 * DIFF_HEADS):
        r0 = n * DIFF_D
        o1, o2 = _rope_rows(dqt[r0:r0 + DIFF_ROT], cos_dt, sin_dt)
        qn = jnp.concatenate([o1, o2, dqt[r0 + DIFF_ROT:r0 + DIFF_D]], axis=0).astype(BF16)
        lo, hi = (qn, zeros) if n % 2 == 0 else (zeros, qn)
        dqt_ref[0, n * LANES:n * LANES + DIFF_D, :] = lo
        dqt_ref[0, n * LANES + DIFF_D:(n + 1) * LANES, :] = hi

    dv = z[:, _DV:_DV + DIFF_HEADS * DIFF_V]
    dvb_ref[...] = dv.astype(BF16)
    dvt = dv.T
    for h in range(DIFF_HEADS):
        sl = slice(h * LANES, (h + 1) * LANES)
        dk = _rope_apply(z[:, _DK + h * LANES:_DK + (h + 1) * LANES], coef_d, half_d)
        dk_ref[:, h, :] = dk
        dkb_ref[:, sl] = dk.astype(BF16)
        dv_ref[:, h, :] = dv[:, sl]
        dvt_ref[0, h * DIFF_VA:h * DIFF_VA + DIFF_V, :] = dvt[sl].astype(BF16)
        dvt_ref[0, h * DIFF_VA + DIFF_V:(h + 1) * DIFF_VA, :] = ones_rows


def _proj(x, tabs, tabs_t, w, *, tm):
    t = x.shape[0]
    n_pos_blocks = tabs[0].shape[0] // tm
    row = lambda n: pl.BlockSpec((tm, n), lambda i: (i, 0))
    full = lambda a: pl.BlockSpec(a.shape, lambda i: (0,) * a.ndim)
    tab = pl.BlockSpec((tm, LANES), lambda i: (i % n_pos_blocks, 0))
    tab_t = lambda a: pl.BlockSpec((a.shape[0], tm), lambda i: (0, i % n_pos_blocks))
    tr = lambda n: pl.BlockSpec((1, n, tm), lambda i: (i, 0, 0))
    hd = lambda n: pl.BlockSpec((tm, DIFF_HEADS, n), lambda i: (i, 0, 0))
    weights = (w['w_in'], w['w_dqt'], w['q_norm'], w['w_uqt'], w['kv_norm'], w['w_ukv'], w['w_uvt'], w['e_place'])
    outs = ((row, MLA_KV_RANK, F32), (row, MLA_ROPE, F32), (hd, 2 * DIFF_D, F32), (hd, DIFF_V, F32),
            (tr, 1024, BF16), (row, 1024, BF16), (row, 512, BF16), (tr, MLA_HEADS * MLA_VA, BF16),
            (tr, 1024, BF16), (row, 512, BF16), (row, 512, BF16), (tr, DIFF_HEADS * DIFF_VA, BF16))
    shape = lambda kind, n: {tr: (t // tm, n, tm), hd: (t, DIFF_HEADS, n), row: (t, n)}[kind]
    return pl.pallas_call(
        _proj_kernel,
        grid=(t // tm,),
        in_specs=[row(D_MODEL)] + [full(a) for a in weights] + [tab] * 4 + [tab_t(a) for a in tabs_t],
        out_specs=[kind(n) for kind, n, _ in outs],
        out_shape=[jax.ShapeDtypeStruct(shape(kind, n), d) for kind, n, d in outs],
        compiler_params=_cparams(("parallel",)),
        name="proj",
    )(x, *weights, *tabs, *tabs_t)


def _expand_kernel(ckv_ref, kpe_ref, w_ukv_ref, e_ref, km_ref, vm_ref):
    kv = jnp.dot(ckv_ref[...].astype(BF16), w_ukv_ref[...], preferred_element_type=F32)
    k_full = kv[:, :MLA_HEADS * HEAD_PAD] + jnp.dot(kpe_ref[...].astype(BF16), e_ref[:MLA_ROPE, :],
                                                   preferred_element_type=F32)
    km_ref[...] = k_full.astype(BF16)
    vm_ref[...] = kv[:, MLA_HEADS * HEAD_PAD:].astype(BF16)


def _expand(ckv, kpe, w, *, tm):
    r = ckv.shape[0]
    row = lambda n: pl.BlockSpec((tm, n), lambda i: (i, 0))
    full = lambda a: pl.BlockSpec(a.shape, lambda i: (0,) * a.ndim)
    return pl.pallas_call(
        _expand_kernel,
        grid=(r // tm,),
        in_specs=[row(MLA_KV_RANK), row(MLA_ROPE), full(w['w_ukv']), full(w['e_place'])],
        out_specs=[row(1024), row(512)],
        out_shape=[jax.ShapeDtypeStruct((r, 1024), BF16), jax.ShapeDtypeStruct((r, 512), BF16)],
        compiler_params=_cparams(("parallel",)),
        name="expand",
    )(ckv, kpe, w['w_ukv'], w['e_place'])


def _chunk_mask_t(t):
    shift = CHUNK.bit_length() - 1
    kc = lax.broadcasted_iota(jnp.int32, (t, t), 0) >> shift
    qc = lax.broadcasted_iota(jnp.int32, (t, t), 1) >> shift
    return kc <= qc


def _flash_scratch(t, dv, n):
    return [pltpu.VMEM((n, t, t), F32), pltpu.VMEM((n, t, t), F32),
            pltpu.VMEM((n, 1, t), F32), pltpu.VMEM((n, dv, t), F32)]


def _flash_loop(i, score, value, n, nq, s_a, s_b, m_ref, acc_ref, ahead):
    t = s_a.shape[-1]
    mask = _chunk_mask_t(t)
    m_ref[...] = jnp.full(m_ref.shape, -jnp.inf, F32)
    acc_ref[...] = jnp.zeros(acc_ref.shape, F32)
    chains_from = lambda q: tuple(range(q * n, nq * n))
    everything = chains_from(0)

    def step(j, s_cur, s_nxt, cur, masked, nxt):
        pending = list(nxt)

        def issue_scores(count):
            for k in pending[:count]:
                s_nxt[k] = score(j + 1, k // n, k % n)
            del pending[:count]

        issue_scores(ahead)
        for k in cur:
            issue_scores(1)
            s = s_cur[k]
            if k in masked:
                s = jnp.where(mask, s, -jnp.inf)
            m = m_ref[k]
            m_new = jnp.maximum(m, jnp.max(s, axis=0, keepdims=True))
            alpha = jnp.exp2(m - m_new)
            p = jnp.exp2(s - m_new)
            m_ref[k] = m_new
            acc_ref[k] = alpha * acc_ref[k] + jnp.dot(value(j, k % n), p.astype(BF16), preferred_element_type=F32)
        issue_scores(len(pending))

    for k in everything:
        s_a[k] = score(0, k // n, k % n)

    def pair(jj, carry):
        step(2 * jj, s_a, s_b, everything, (), everything)
        step(2 * jj + 1, s_b, s_a, everything, (), everything)
        return carry

    lax.fori_loop(0, (nq // 2) * i, pair, 0)
    bufs = (s_a, s_b)
    for e in range(nq):
        step(nq * i + e, bufs[e % 2], bufs[(e + 1) % 2], chains_from(e), chains_from(e)[:n], chains_from(e + 1))


def _normalised(acc_ref, k, dv):
    return acc_ref[k, :dv, :] / acc_ref[k, dv:dv + 1, :]


def _mla_flash_kernel(qt_ref, k_ref, vt_ref, o_ref, s_a, s_b, m_ref, acc_ref, *, t, heads, nq):
    def score(j, q, h):
        rows = pl.ds(pl.multiple_of(j * t, t), t)
        return jnp.dot(k_ref[0, rows, h * HEAD_PAD:(h + 1) * HEAD_PAD],
                       qt_ref[q, h * HEAD_PAD:(h + 1) * HEAD_PAD, :], preferred_element_type=F32)

    value = lambda j, h: vt_ref[j, h * MLA_VA:(h + 1) * MLA_VA, :]
    _flash_loop(pl.program_id(2), score, value, heads, nq, s_a, s_b, m_ref, acc_ref, ahead=2)
    for q in range(nq):
        for h in range(0, heads, 2):
            k = q * heads + h
            pair = jnp.concatenate([_normalised(acc_ref, k, MLA_V), _normalised(acc_ref, k + 1, MLA_V)], axis=0)
            o_ref[0, q * t:(q + 1) * t, h * MLA_V:(h + 2) * MLA_V] = pair.T.astype(BF16)


def _mla_flash(qt, km, vt, *, t, heads, nq):
    b, s, _ = km.shape
    nk = s // t
    steps = nk // nq
    return pl.pallas_call(
        functools.partial(_mla_flash_kernel, t=t, heads=heads, nq=nq),
        grid=(b, MLA_HEADS // heads, steps),
        in_specs=[pl.BlockSpec((nq, heads * HEAD_PAD, t), lambda bi, hi, i: (bi * steps + i, hi, 0)),
                  pl.BlockSpec((1, s, heads * HEAD_PAD), lambda bi, hi, i: (bi, 0, hi)),
                  pl.BlockSpec((nk, heads * MLA_VA, t), lambda bi, hi, i: (bi, hi, 0))],
        out_specs=pl.BlockSpec((1, nq * t, heads * MLA_V), lambda bi, hi, i: (bi, i, hi)),
        out_shape=jax.ShapeDtypeStruct((b, s, MLA_HEADS * MLA_V), BF16),
        scratch_shapes=_flash_scratch(t, MLA_VA, nq * heads),
        compiler_params=_cparams(("parallel", "parallel", "arbitrary")),
        name="mla_flash",
    )(qt, km, vt)


def _diff_lambda(lam_ref, lam_init):
    lq1, lk1, lq2, lk2 = (lam_ref[r:r + 1, :] for r in range(4))
    return (jnp.exp(jnp.sum(lq1 * lk1, axis=-1, keepdims=True))
            - jnp.exp(jnp.sum(lq2 * lk2, axis=-1, keepdims=True)) + lam_init)


def _diff_finish(o1, o2, lam, subln, lam_init):
    o = o1 - lam * o2
    return _rms(o, subln) * (1.0 - lam_init)


def _diff_flash_kernel(lam_ref, subln_ref, qt_ref, k_ref, vt_ref, o_ref, s_a, s_b, m_ref, acc_ref,
                       *, t, heads, nq, lam_init):
    def score(j, q, n):
        rows = pl.ds(pl.multiple_of(j * t, t), t)
        return jnp.dot(k_ref[0, rows, (n // 2) * LANES:(n // 2 + 1) * LANES],
                       qt_ref[q, n * LANES:(n + 1) * LANES, :], preferred_element_type=F32)

    value = lambda j, n: vt_ref[j, (n // 2) * DIFF_VA:(n // 2 + 1) * DIFF_VA, :]
    _flash_loop(pl.program_id(2), score, value, 2 * heads, nq, s_a, s_b, m_ref, acc_ref, ahead=1)
    lam = _diff_lambda(lam_ref, lam_init)
    for q in range(nq):
        for h in range(heads):
            k = q * 2 * heads + 2 * h
            o1 = _normalised(acc_ref, k, DIFF_V).T
            o2 = _normalised(acc_ref, k + 1, DIFF_V).T
            o = _diff_finish(o1, o2, lam, subln_ref[...], lam_init)
            o_ref[0, q * t:(q + 1) * t, h * DIFF_V:(h + 1) * DIFF_V] = o.astype(BF16)


def _diff_flash(lam_rows, subln, dqt, dkb, dvt, *, t, heads, nq, lam_init):
    b, s, _ = dkb.shape
    nk = s // t
    steps = nk // nq
    full = lambda a: pl.BlockSpec(a.shape, lambda bi, hi, i: (0,) * a.ndim)
    return pl.pallas_call(
        functools.partial(_diff_flash_kernel, t=t, heads=heads, nq=nq, lam_init=lam_init),
        grid=(b, DIFF_HEADS // heads, steps),
        in_specs=[full(lam_rows), full(subln),
                  pl.BlockSpec((nq, heads * 2 * LANES, t), lambda bi, hi, i: (bi * steps + i, hi, 0)),
                  pl.BlockSpec((1, s, heads * LANES), lambda bi, hi, i: (bi, 0, hi)),
                  pl.BlockSpec((nk, heads * DIFF_VA, t), lambda bi, hi, i: (bi, hi, 0))],
        out_specs=pl.BlockSpec((1, nq * t, heads * DIFF_V), lambda bi, hi, i: (bi, i, hi)),
        out_shape=jax.ShapeDtypeStruct((b, s, DIFF_HEADS * DIFF_V), BF16),
        scratch_shapes=_flash_scratch(t, DIFF_VA, nq * 2 * heads),
        compiler_params=_cparams(("parallel", "parallel", "arbitrary")),
        name="diff_flash",
    )(lam_rows, subln, dqt, dkb, dvt)


def _two_part_softmax_pv(s_past, s_new, v_past, v_new):
    m = jnp.maximum(jnp.max(s_past, axis=-1, keepdims=True), jnp.max(s_new, axis=-1, keepdims=True))
    p_past = jnp.exp2(s_past - m)
    p_new = jnp.exp2(s_new - m)
    l = jnp.sum(p_past, axis=-1, keepdims=True) + jnp.sum(p_new, axis=-1, keepdims=True)
    acc = (jnp.dot(p_past.astype(BF16), v_past, preferred_element_type=F32)
           + jnp.dot(p_new.astype(BF16), v_new, preferred_element_type=F32))
    return acc / l


def _token_major(qt_ref):
    return qt_ref[0].astype(F32).T.astype(BF16)


def _mla_step_kernel(qt_ref, kp_ref, vp_ref, kn_ref, vn_ref, o_ref):
    q_all = _token_major(qt_ref)
    for h in range(MLA_HEADS):
        ks = slice(h * HEAD_PAD, (h + 1) * HEAD_PAD)
        vs = slice(h * MLA_V, (h + 1) * MLA_V)
        q = q_all[:, ks]
        o = _two_part_softmax_pv(_nt_dot(q, kp_ref[0, :, ks]), _nt_dot(q, kn_ref[0, :, ks]),
                                 vp_ref[0, :, vs], vn_ref[0, :, vs])
        o_ref[0, :, vs] = o.astype(BF16)


def _mla_step(qt, km_past, vm_past, km_new, vm_new):
    b, _, n = qt.shape
    blk = lambda a: pl.BlockSpec((1,) + a.shape[1:], lambda bi: (bi, 0, 0))
    args = (qt, km_past, vm_past, km_new, vm_new)
    return pl.pallas_call(
        _mla_step_kernel,
        grid=(b,),
        in_specs=[blk(a) for a in args],
        out_specs=pl.BlockSpec((1, n, MLA_HEADS * MLA_V), lambda bi: (bi, 0, 0)),
        out_shape=jax.ShapeDtypeStruct((b, n, MLA_HEADS * MLA_V), BF16),
        compiler_params=_cparams(("parallel",)),
        name="mla_step",
    )(*args)


def _diff_step_kernel(lam_ref, subln_ref, qt_ref, kp_ref, vp_ref, kn_ref, vn_ref, o_ref, *, lam_init):
    lam = _diff_lambda(lam_ref, lam_init)
    q = _token_major(qt_ref)
    for h in range(DIFF_HEADS):
        sl = slice(h * LANES, (h + 1) * LANES)
        q1 = q[:, 2 * h * LANES:(2 * h + 1) * LANES]
        q2 = q[:, (2 * h + 1) * LANES:(2 * h + 2) * LANES]
        kp = kp_ref[0, :, h, :].astype(BF16)
        vp = vp_ref[0, :, h, :].astype(BF16)
        kn = kn_ref[0, :, sl]
        vn = vn_ref[0, :, sl]
        o1 = _two_part_softmax_pv(_nt_dot(q1, kp), _nt_dot(q1, kn), vp, vn)
        o2 = _two_part_softmax_pv(_nt_dot(q2, kp), _nt_dot(q2, kn), vp, vn)
        o_ref[0, :, sl] = _diff_finish(o1, o2, lam, subln_ref[...], lam_init).astype(BF16)


def _diff_step(lam_rows, subln, dqt, k_past, v_past, dkb, dvb, *, lam_init):
    b, _, n = dqt.shape
    full = lambda a: pl.BlockSpec(a.shape, lambda bi: (0,) * a.ndim)
    blk = lambda a: pl.BlockSpec((1,) + a.shape[1:], lambda bi: (bi,) + (0,) * (a.ndim - 1))
    args = (dqt, k_past, v_past, dkb, dvb)
    return pl.pallas_call(
        functools.partial(_diff_step_kernel, lam_init=lam_init),
        grid=(b,),
        in_specs=[full(lam_rows), full(subln)] + [blk(a) for a in args],
        out_specs=pl.BlockSpec((1, n, DIFF_HEADS * DIFF_V), lambda bi: (bi, 0, 0)),
        out_shape=jax.ShapeDtypeStruct((b, n, DIFF_HEADS * DIFF_V), BF16),
        compiler_params=_cparams(("parallel",)),
        name="diff_step",
    )(lam_rows, subln, *args)


def _first_index_of_max(vals, lane):
    m = jnp.max(vals, axis=-1, keepdims=True)
    idx = jnp.min(jnp.where(vals == m, lane, float(LANES)), axis=-1, keepdims=True)
    return m, idx


_SLAB_MEMBER = EXP_PER_GROUP
_SLAB_RANK = EXP_PER_GROUP + 1


def _route(hb, wr_ref, br_ref):
    logit = jnp.dot(hb, wr_ref[...], preferred_element_type=F32) + br_ref[...]
    lane_i = lax.broadcasted_iota(jnp.int32, logit.shape, 1)
    lane = lane_i.astype(F32)
    group_of_lane = (lane_i >> (EXP_PER_GROUP.bit_length() - 1)).astype(F32)
    neg = -jnp.inf
    g_logit = jnp.where((lane_i >= N_EXPERTS) & (lane_i < N_EXPERTS + N_GROUPS), logit, neg)
    g_max, g_lane = _first_index_of_max(g_logit, lane)
    g_w = 1.0 / jnp.sum(jnp.exp(g_logit - g_max), axis=-1, keepdims=True)
    g_idx = g_lane - float(N_EXPERTS)
    e_logit = jnp.where(group_of_lane == g_idx, logit, neg)
    m1, i1 = _first_index_of_max(e_logit, lane)
    m2, i2 = _first_index_of_max(jnp.where(lane == i1, neg, e_logit), lane)
    r = jnp.exp(m2 - m1)
    p1 = 1.0 / (1.0 + r)
    p2 = r / (1.0 + r)
    gate = g_w * (jnp.where(lane == i1, p1, 0.0) + jnp.where(lane == i2, p2, 0.0))
    return gate, g_idx, jnp.where(lane == g_idx, 1.0, 0.0)


def _post_attn_kernel(x_ref, mla_ref, diff_ref, wo_a_ref, wo_b_ref, g_ref, b_ref, wr_ref, br_ref, ltri_ref, utri_ref,
                      h_ref, hb_ref, slab_ref, rank_ref, cnt_ref, *, alpha):
    a = (jnp.dot(mla_ref[...], wo_a_ref[...], preferred_element_type=F32)
         + jnp.dot(diff_ref[...], wo_b_ref[...], preferred_element_type=F32))
    h = _layer_norm(alpha * x_ref[...] + a, g_ref[...], b_ref[...])
    h_ref[...] = h
    hb = h.astype(BF16)
    hb_ref[...] = hb
    gate, g_idx, onehot = _route(hb, wr_ref, br_ref)
    lane_i = lax.broadcasted_iota(jnp.int32, gate.shape, 1)

    rank_c = jnp.dot(ltri_ref[...], onehot.astype(BF16), preferred_element_type=F32)
    onehot_t = onehot.T[:BF16_ROWS]
    rank_r = jnp.dot(onehot_t.astype(BF16), utri_ref[...], preferred_element_type=F32)
    rank_ref[0] = jnp.where(onehot_t > 0.0, rank_r, -1.0)
    cnt_ref[0] = jnp.broadcast_to(jnp.sum(onehot, axis=0, keepdims=True), cnt_ref.shape[1:])

    for g in range(N_GROUPS):
        member = g_idx == float(g)
        rank_g = jnp.sum(jnp.where(lane_i == g, rank_c, 0.0), axis=-1, keepdims=True)
        rolled = gate if g == 0 else pltpu.roll(gate, LANES - EXP_PER_GROUP * g, 1)
        extra = jnp.where(lane_i == _SLAB_MEMBER, jnp.where(member, 1.0, 0.0),
                          jnp.where(lane_i == _SLAB_RANK, jnp.where(member, rank_g, -1.0), 0.0))
        slab_ref[:, g * LANES:(g + 1) * LANES] = jnp.where(lane_i < EXP_PER_GROUP, rolled, extra)


def _post_attn(x, mla_o, diff_o, w, *, tm, alpha):
    t = x.shape[0]
    nt = t // tm
    row = lambda n: pl.BlockSpec((tm, n), lambda i: (i, 0))
    full = lambda a: pl.BlockSpec(a.shape, lambda i: (0,) * a.ndim)
    per_tile = lambda r, c: pl.BlockSpec((1, r, c), lambda i: (i, 0, 0))
    idx = jnp.arange(tm)
    ltri = (idx[None, :] < idx[:, None]).astype(BF16)
    weights = (w['w_out_a'], w['w_out_b'], w['ln1_g'], w['ln1_b'], w['w_route'], w['b_route'], ltri, ltri.T)
    return pl.pallas_call(
        functools.partial(_post_attn_kernel, alpha=alpha),
        grid=(nt,),
        in_specs=[row(D_MODEL), row(512), row(512)] + [full(a) for a in weights],
        out_specs=[row(D_MODEL), row(D_MODEL), row(N_GROUPS * LANES), per_tile(BF16_ROWS, tm), per_tile(8, LANES)],
        out_shape=[jax.ShapeDtypeStruct((t, D_MODEL), F32), jax.ShapeDtypeStruct((t, D_MODEL), BF16),
                   jax.ShapeDtypeStruct((t, N_GROUPS * LANES), F32),
                   jax.ShapeDtypeStruct((nt, BF16_ROWS, tm), F32), jax.ShapeDtypeStruct((nt, 8, LANES), F32)],
        compiler_params=_cparams(("parallel",)),
        name="post_attn",
    )(x, mla_o, diff_o, *weights)


MOE_TILE = 1024
MOE_SLACK_ROWS = 32
MOE_EXTRA_ROWS = 128


def _split3(x):
    hi = x.astype(BF16)
    r1 = x - hi.astype(F32)
    mid = r1.astype(BF16)
    lo = (r1 - mid.astype(F32)).astype(BF16)
    return hi, mid, lo


def _moe_kernel(cnt_ref, hb_ref, slab_ref, rank_ref, wg_ref, wu_ref, wd_ref, yin_ref, g_ref, b_ref, y_ref,
                *, g, alpha):
    n_tok = cnt_ref[pl.program_id(0), g]
    slab = slab_ref[...]
    rank_col = slab[:, _SLAB_RANK:_SLAB_RANK + 1]
    rank_row = rank_ref[0, g:g + 1, :]
    gate3 = _split3(slab)

    y_ref[...] = alpha * yin_ref[...] if g == 0 else yin_ref[...]

    def one_pass(base, rows):
        sub = lax.broadcasted_iota(jnp.int32, (rows, 1), 0).astype(F32)
        lan = lax.broadcasted_iota(jnp.int32, (1, rows), 1).astype(F32)
        sel = jnp.where(rank_row - base == sub, 1.0, 0.0).astype(BF16)
        sel_t = jnp.where(rank_col - base == lan, 1.0, 0.0).astype(BF16)
        x = jnp.dot(sel, hb_ref[...], preferred_element_type=F32).astype(BF16)
        gate_c = sum(jnp.dot(sel, part, preferred_element_type=F32) for part in gate3)
        ys = jnp.zeros((rows, D_MODEL), F32)
        for e in range(EXP_PER_GROUP):
            hid = (jax.nn.silu(jnp.dot(x, wg_ref[e], preferred_element_type=F32))
                   * jnp.dot(x, wu_ref[e], preferred_element_type=F32))
            ys = ys + jnp.dot((hid * gate_c[:, e:e + 1]).astype(BF16), wd_ref[e], preferred_element_type=F32)
        ys_hi = ys.astype(BF16)
        ys_lo = (ys - ys_hi.astype(F32)).astype(BF16)
        y_ref[...] += (jnp.dot(sel_t, ys_hi, preferred_element_type=F32)
                       + jnp.dot(sel_t, ys_lo, preferred_element_type=F32))

    first_rows = slab.shape[0] // N_GROUPS + MOE_SLACK_ROWS
    one_pass(0.0, first_rows)

    def extra_pass(c, carry):
        one_pass((first_rows + c * MOE_EXTRA_ROWS).astype(F32), MOE_EXTRA_ROWS)
        return carry

    lax.fori_loop(0, (jnp.maximum(n_tok - first_rows, 0) + MOE_EXTRA_ROWS - 1) // MOE_EXTRA_ROWS, extra_pass, 0)

    if g == N_GROUPS - 1:
        y_ref[...] = _layer_norm(y_ref[...], g_ref[...], b_ref[...])


def _moe_group(g, y, hb, slab, rank_r, cnt, w, *, tm, alpha):
    t = y.shape[0]
    row = lambda n: pl.BlockSpec((tm, n), lambda i, c: (i, 0))
    full = lambda a: pl.BlockSpec(a.shape, lambda i, c: (0,) * a.ndim)
    grp = lambda a: pl.BlockSpec((EXP_PER_GROUP,) + a.shape[1:], lambda i, c: (g, 0, 0),
                                 pipeline_mode=pl.Buffered(1))
    grid_spec = pltpu.PrefetchScalarGridSpec(
        num_scalar_prefetch=1,
        grid=(t // tm,),
        in_specs=[row(D_MODEL), pl.BlockSpec((tm, LANES), lambda i, c: (i, g)),
                  pl.BlockSpec((1, BF16_ROWS, tm), lambda i, c: (i, 0, 0)),
                  grp(w['w_gate']), grp(w['w_up']), grp(w['w_down']), row(D_MODEL),
                  full(w['ln2_g']), full(w['ln2_b'])],
        out_specs=row(D_MODEL))
    return pl.pallas_call(
        functools.partial(_moe_kernel, g=g, alpha=alpha),
        grid_spec=grid_spec,
        out_shape=jax.ShapeDtypeStruct((t, D_MODEL), F32),
        compiler_params=_cparams(("arbitrary",)),
        name=f"moe_group{g}",
    )(cnt, hb, slab, rank_r, w['w_gate'], w['w_up'], w['w_down'], y, w['ln2_g'], w['ln2_b'])


def _moe(h, hb, slab, rank_r, cnt, w, *, tm, alpha):
    y = h
    for g in range(N_GROUPS):
        y = _moe_group(g, y, hb, slab, rank_r, cnt, w, tm=tm, alpha=alpha)
    return y


def _rope_tables(pos):
    pos = pos.astype(F32)[:, None]
    lane = jnp.arange(LANES)
    inv_m = ROPE_THETA ** (-jnp.arange(0, MLA_ROPE, 2, dtype=F32) / MLA_ROPE)
    ang_m = pos * inv_m
    inv_d = ROPE_THETA ** (-jnp.arange(0, DIFF_ROT, 2, dtype=F32) / DIFF_ROT)
    ang_d = pos * inv_d
    narrow = lax.optimization_barrier((jnp.cos(ang_m), jnp.sin(ang_m), jnp.cos(ang_d), jnp.sin(ang_d)))
    tabs = tuple(a[:, lane % a.shape[1]] for a in narrow)
    tabs_t = tuple(a.T for a in narrow)
    return tabs, tabs_t


def _prep_weights(l, w_in, mla_q_norm, mla_w_uq, mla_kv_norm, mla_w_ukv, diff_subln, w_out, ln1_g, ln1_b,
                  ln2_g, ln2_b, w_route_group, b_route_group, w_route_expert, b_route_expert,
                  w_exp_gate, w_exp_up, w_exp_down, lam_rows):
    wi = w_in[l]
    s0, s1, s2, s3, s4 = 256, 512, 544, 1056, 1568
    w_in_r = jnp.concatenate([wi[:, :s0], wi[:, s0:s1], wi[:, s3:s4], wi[:, s4:],
                              wi[:, s1:s2], jnp.zeros((D_MODEL, LANES - MLA_ROPE), F32)], axis=1)
    w_uq = jnp.pad(mla_w_uq[l].reshape(MLA_Q_RANK, MLA_HEADS, MLA_NOPE + MLA_ROPE),
                   ((0, 0), (0, 0), (0, HEAD_PAD - MLA_NOPE - MLA_ROPE))).reshape(MLA_Q_RANK, -1)
    ukv = mla_w_ukv[l].reshape(MLA_KV_RANK, MLA_HEADS, MLA_NOPE + MLA_V)
    w_uk = jnp.pad(ukv[..., :MLA_NOPE], ((0, 0), (0, 0), (0, HEAD_PAD - MLA_NOPE))).reshape(MLA_KV_RANK, -1)
    w_uv = ukv[..., MLA_NOPE:].reshape(MLA_KV_RANK, -1)
    rows = jnp.arange(LANES)[:, None]
    cols = jnp.arange(MLA_HEADS * HEAD_PAD)[None, :]
    e_place = ((rows < MLA_ROPE) & (cols % HEAD_PAD == rows + MLA_NOPE)).astype(BF16)
    w_route = jnp.concatenate([w_route_expert[l], w_route_group[l],
                               jnp.zeros((D_MODEL, LANES - N_EXPERTS - N_GROUPS), F32)], axis=1)
    b_route = jnp.concatenate([b_route_expert[l], b_route_group[l],
                               jnp.zeros((LANES - N_EXPERTS - N_GROUPS,), F32)])[None, :]
    return dict(
        w_in=w_in_r.astype(BF16), w_dqt=wi[:, s2:s3].T.astype(BF16),
        q_norm=mla_q_norm[l][None, :], w_uqt=w_uq.T.astype(BF16),
        kv_norm=mla_kv_norm[l][None, :], w_ukv=jnp.concatenate([w_uk, w_uv], axis=1).astype(BF16),
        w_uvt=w_uv.T.astype(BF16),
        e_place=e_place, subln=diff_subln[l][None, :], lam_rows=lam_rows,
        w_out_a=w_out[l][:MLA_HEADS * MLA_V].astype(BF16), w_out_b=w_out[l][MLA_HEADS * MLA_V:].astype(BF16),
        ln1_g=ln1_g[l][None, :], ln1_b=ln1_b[l][None, :], ln2_g=ln2_g[l][None, :], ln2_b=ln2_b[l][None, :],
        w_route=w_route.astype(BF16), b_route=b_route,
        w_gate=w_exp_gate[l].astype(BF16), w_up=w_exp_up[l].astype(BF16), w_down=w_exp_down[l].astype(BF16))


def _block_output(x, mla_o, diff_o, w, alpha):
    t = x.shape[0]
    tm = min(MOE_TILE, t)
    h, hb, slab, rank_r, cnt = _post_attn(x, mla_o.reshape(t, -1), diff_o.reshape(t, -1), w, tm=tm, alpha=alpha)
    cnt = cnt[:, 0, :N_GROUPS].astype(jnp.int32)
    return _moe(h, hb, slab, rank_r, cnt, w, tm=tm, alpha=alpha)


def kernel(x_prompt, x_sample, cache_mla_ckv, cache_mla_kpe, cache_diff_k, cache_diff_v, w_in, mla_q_norm, mla_w_uq, mla_kv_norm, mla_w_ukv, diff_lambda_q1, diff_lambda_k1, diff_lambda_q2, diff_lambda_k2, diff_subln, w_out, ln1_g, ln1_b, ln2_g, ln2_b, w_route_group, b_route_group, w_route_expert, b_route_expert, w_exp_gate, w_exp_up, w_exp_down):
    depth = w_in.shape[0]
    bp, sp, _ = x_prompt.shape
    bs, ss, _ = x_sample.shape
    past = cache_mla_ckv.shape[2]
    alpha = (2.0 * depth) ** 0.25
    tabs_p, tabs_pt = _rope_tables(jnp.arange(sp))
    tabs_s, tabs_st = _rope_tables(past + jnp.arange(ss))
    hp = x_prompt.reshape(bp * sp, D_MODEL)
    hs = x_sample.reshape(bs * ss, D_MODEL)
    outs = [[] for _ in range(8)]
    for l in range(depth):
        lam_init = 0.8 - 0.6 * math.exp(-0.3 * l)
        lam_rows = jnp.stack([diff_lambda_q1[l], diff_lambda_k1[l], diff_lambda_q2[l], diff_lambda_k2[l]])
        w = _prep_weights(l, w_in, mla_q_norm, mla_w_uq, mla_kv_norm, mla_w_ukv, diff_subln, w_out,
                          ln1_g, ln1_b, ln2_g, ln2_b, w_route_group, b_route_group, w_route_expert,
                          b_route_expert, w_exp_gate, w_exp_up, w_exp_down, lam_rows)

        ckv, kpe, dk, dv, qt, km, _, vt, dqt, dkb, _, dvt = _proj(hp, tabs_p, tabs_pt, w, tm=ATTN_TILE)
        b3 = lambda a: a.reshape(bp, sp, -1)
        mla_o = _mla_flash(qt, b3(km), vt, t=ATTN_TILE, heads=4, nq=ATTN_QTILES)
        diff_o = _diff_flash(w['lam_rows'], w['subln'], dqt, b3(dkb), dvt, t=ATTN_TILE, heads=2,
                             nq=ATTN_QTILES, lam_init=lam_init)
        outs[0].append(ckv.reshape(bp, sp, MLA_KV_RANK))
        outs[1].append(kpe.reshape(bp, sp, MLA_ROPE))
        outs[2].append(dk.reshape(bp, sp, DIFF_HEADS, 2 * DIFF_D))
        outs[3].append(dv.reshape(bp, sp, DIFF_HEADS, DIFF_V))
        hp = _block_output(hp, mla_o, diff_o, w, alpha)

        ckv_s, kpe_s, dk_s, dv_s, qt_s, km_s, vm_s, _, dqt_s, dkb_s, dvb_s, _ = _proj(hs, tabs_s, tabs_st, w, tm=ss)
        km_past, vm_past = _expand(cache_mla_ckv[l].reshape(bs * past, MLA_KV_RANK),
                                   cache_mla_kpe[l].reshape(bs * past, MLA_ROPE), w, tm=512)
        s3 = lambda a: a.reshape(bs, ss, -1)
        p3 = lambda a: a.reshape(bs, past, -1)
        mla_o_s = _mla_step(qt_s, p3(km_past), p3(vm_past), s3(km_s), s3(vm_s))
        diff_o_s = _diff_step(w['lam_rows'], w['subln'], dqt_s, cache_diff_k[l], cache_diff_v[l],
                              s3(dkb_s), s3(dvb_s), lam_init=lam_init)
        outs[4].append(ckv_s.reshape(bs, ss, MLA_KV_RANK))
        outs[5].append(kpe_s.reshape(bs, ss, MLA_ROPE))
        outs[6].append(dk_s.reshape(bs, ss, DIFF_HEADS, 2 * DIFF_D))
        outs[7].append(dv_s.reshape(bs, ss, DIFF_HEADS, DIFF_V))
        hs = _block_output(hs, mla_o_s, diff_o_s, w, alpha)

    return (hp.reshape(bp, sp, D_MODEL), hs.reshape(bs, ss, D_MODEL)) + tuple(jnp.stack(o) for o in outs)
```

```python
import functools
import math

import jax
import jax.numpy as jnp
from jax import lax
from jax.experimental import pallas as pl
from jax.experimental.pallas import tpu as pltpu

F32 = jnp.float32
BF16 = jnp.bfloat16

D_MODEL = 1024
CHUNK = 64
ROPE_THETA = 500000.0
MLA_HEADS = 8
MLA_NOPE = 64
MLA_ROPE = 32
MLA_V = 64
MLA_Q_RANK = 256
MLA_KV_RANK = 256
MLA_SCALE = (MLA_NOPE + MLA_ROPE) ** -0.5
DIFF_HEADS = 4
DIFF_D = 64
DIFF_V = 2 * DIFF_D
DIFF_ROT = DIFF_D // 4
DIFF_SCALE = DIFF_D ** -0.5
N_GROUPS = 4
EXP_PER_GROUP = 8
N_EXPERTS = N_GROUPS * EXP_PER_GROUP
D_EXPERT = 256
EPS_LN = 1e-5
EPS_RMS = 1e-6
LOG2E = math.log2(math.e)

LANES = 128
HEAD_PAD = LANES
ATTN_TILE = 256
ATTN_QTILES = 4
BF16_ROWS = 16
MLA_VA = MLA_V + BF16_ROWS
DIFF_VA = DIFF_V + BF16_ROWS
VMEM_LIMIT = 56 * 1024 * 1024

_CQ, _CKV, _DK, _DV, _KR, _IN_COLS_R = 0, 256, 512, 1024, 1536, 1664


def _cparams(sem):
    return pltpu.CompilerParams(dimension_semantics=sem, vmem_limit_bytes=VMEM_LIMIT)


def _rms(x, g):
    return x * lax.rsqrt(jnp.mean(x * x, axis=-1, keepdims=True) + EPS_RMS) * g


def _layer_norm(x, g, b):
    mu = jnp.mean(x, axis=-1, keepdims=True)
    xc = x - mu
    var = jnp.mean(xc * xc, axis=-1, keepdims=True)
    return xc * lax.rsqrt(var + EPS_LN) * g + b


def _nt_dot(a, b):
    return lax.dot_general(a, b, (((1,), (1,)), ((), ())), preferred_element_type=F32)


def _rope_coeffs(cos_t, sin_t, lo, half, period):
    lane = lax.broadcasted_iota(jnp.int32, cos_t.shape, 1) & (period - 1)
    is1 = (lane >= lo) & (lane < lo + half)
    is2 = (lane >= lo + half) & (lane < lo + 2 * half)
    c = jnp.where(is1 | is2, cos_t, 1.0)
    a = jnp.where(is1, -sin_t, 0.0)
    b = jnp.where(is2, sin_t, 0.0)
    return c, a, b


def _rope_apply(x, coeffs, half):
    c, a, b = coeffs
    return x * c + pltpu.roll(x, LANES - half, 1) * a + pltpu.roll(x, half, 1) * b


def _rope_rows(x, cos_t, sin_t):
    half = cos_t.shape[0]
    x1, x2 = x[:half], x[half:]
    return x1 * cos_t - x2 * sin_t, x1 * sin_t + x2 * cos_t


def _proj_kernel(x_ref, w_in_ref, w_dqt_ref, qn_ref, w_uqt_ref, kvn_ref, w_ukv_ref, w_uvt_ref, e_ref,
                 cm_ref, sm_ref, cd_ref, sd_ref, cmt_ref, smt_ref, cdt_ref, sdt_ref,
                 ckv_ref, kpe_ref, dk_ref, dv_ref, qt_ref, km_ref, vm_ref, vt_ref,
                 dqt_ref, dkb_ref, dvb_ref, dvt_ref):
    xb = x_ref[...].astype(BF16)
    tm = xb.shape[0]
    z = jnp.dot(xb, w_in_ref[...], preferred_element_type=F32)
    half_m = MLA_ROPE // 2
    half_d = DIFF_ROT // 2
    coef_k = _rope_coeffs(cm_ref[...], sm_ref[...], 0, half_m, LANES)
    coef_d = _rope_coeffs(cd_ref[...], sd_ref[...], 0, half_d, DIFF_D)

    cq = _rms(z[:, _CQ:_CQ + MLA_Q_RANK], qn_ref[...])
    qt = _nt_dot(w_uqt_ref[...], cq.astype(BF16)) * (MLA_SCALE * LOG2E)
    cos_mt, sin_mt = cmt_ref[...], smt_ref[...]
    for h in range(MLA_HEADS):
        r0 = h * HEAD_PAD
        qt_ref[0, r0:r0 + MLA_NOPE, :] = qt[r0:r0 + MLA_NOPE].astype(BF16)
        o1, o2 = _rope_rows(qt[r0 + MLA_NOPE:r0 + MLA_NOPE + MLA_ROPE], cos_mt, sin_mt)
        qt_ref[0, r0 + MLA_NOPE:r0 + MLA_NOPE + half_m, :] = o1.astype(BF16)
        qt_ref[0, r0 + MLA_NOPE + half_m:r0 + MLA_NOPE + MLA_ROPE, :] = o2.astype(BF16)
        qt_ref[0, r0 + MLA_NOPE + MLA_ROPE:r0 + HEAD_PAD, :] = jnp.zeros((HEAD_PAD - MLA_NOPE - MLA_ROPE, tm), BF16)

    ckv = _rms(z[:, _CKV:_CKV + MLA_KV_RANK], kvn_ref[...])
    ckv_ref[...] = ckv
    ckv_b = ckv.astype(BF16)
    kslab = _rope_apply(z[:, _KR:_KR + LANES], coef_k, half_m)
    kpe_ref[...] = kslab[:, :MLA_ROPE]
    kv = jnp.dot(ckv_b, w_ukv_ref[...], preferred_element_type=F32)
    k_full = kv[:, :MLA_HEADS * HEAD_PAD] + jnp.dot(kslab.astype(BF16), e_ref[...], preferred_element_type=F32)
    km_ref[...] = k_full.astype(BF16)
    vm_ref[...] = kv[:, MLA_HEADS * HEAD_PAD:].astype(BF16)
    ones_rows = jnp.where(lax.broadcasted_iota(jnp.int32, (BF16_ROWS, tm), 0) == 0, 1.0, 0.0).astype(BF16)
    vt = _nt_dot(w_uvt_ref[...], ckv_b)
    for h in range(MLA_HEADS):
        vt_ref[0, h * MLA_VA:h * MLA_VA + MLA_V, :] = vt[h * MLA_V:(h + 1) * MLA_V].astype(BF16)
        vt_ref[0, h * MLA_VA + MLA_V:(h + 1) * MLA_VA, :] = ones_rows

    dqt = _nt_dot(w_dqt_ref[...], xb) * (DIFF_SCALE * LOG2E)
    cos_dt, sin_dt = cdt_ref[...], sdt_ref[...]
    zeros = jnp.zeros((DIFF_D, tm), BF16)
    for n in range(2 * DIFF_HEADS):
        r0 = n * DIFF_D
        o1, o2 = _rope_rows(dqt[r0:r0 + DIFF_ROT], cos_dt, sin_dt)
        qn = jnp.concatenate([o1, o2, dqt[r0 + DIFF_ROT:r0 + DIFF_D]], axis=0).astype(BF16)
        lo, hi = (qn, zeros) if n % 2 == 0 else (zeros, qn)
        dqt_ref[0, n * LANES:n * LANES + DIFF_D, :] = lo
        dqt_ref[0, n * LANES + DIFF_D:(n + 1) * LANES, :] = hi

    dv = z[:, _DV:_DV + DIFF_HEADS * DIFF_V]
    dvb_ref[...] = dv.astype(BF16)
    dvt = dv.T
    for h in range(DIFF_HEADS):
        sl = slice(h * LANES, (h + 1) * LANES)
        dk = _rope_apply(z[:, _DK + h * LANES:_DK + (h + 1) * LANES], coef_d, half_d)
        dk_ref[:, h, :] = dk
        dkb_ref[:, sl] = dk.astype(BF16)
        dv_ref[:, h, :] = dv[:, sl]
        dvt_ref[0, h * DIFF_VA:h * DIFF_VA + DIFF_V, :] = dvt[sl].astype(BF16)
        dvt_ref[0, h * DIFF_VA + DIFF_V:(h + 1) * DIFF_VA, :] = ones_rows


def _proj(x, tabs, tabs_t, w, *, tm):
    t = x.shape[0]
    n_pos_blocks = tabs[0].shape[0] // tm
    row = lambda n: pl.BlockSpec((tm, n), lambda i: (i, 0))
    full = lambda a: pl.BlockSpec(a.shape, lambda i: (0,) * a.ndim)
    tab = pl.BlockSpec((tm, LANES), lambda i: (i % n_pos_blocks, 0))
    tab_t = lambda a: pl.BlockSpec((a.shape[0], tm), lambda i: (0, i % n_pos_blocks))
    tr = lambda n: pl.BlockSpec((1, n, tm), lambda i: (i, 0, 0))
    hd = lambda n: pl.BlockSpec((tm, DIFF_HEADS, n), lambda i: (i, 0, 0))
    weights = (w['w_in'], w['w_dqt'], w['q_norm'], w['w_uqt'], w['kv_norm'], w['w_ukv'], w['w_uvt'], w['e_place'])
    outs = ((row, MLA_KV_RANK, F32), (row, MLA_ROPE, F32), (hd, 2 * DIFF_D, F32), (hd, DIFF_V, F32),
            (tr, 1024, BF16), (row, 1024, BF16), (row, 512, BF16), (tr, MLA_HEADS * MLA_VA, BF16),
            (tr, 1024, BF16), (row, 512, BF16), (row, 512, BF16), (tr, DIFF_HEADS * DIFF_VA, BF16))
    shape = lambda kind, n: {tr: (t // tm, n, tm), hd: (t, DIFF_HEADS, n), row: (t, n)}[kind]
    return pl.pallas_call(
        _proj_kernel,
        grid=(t // tm,),
        in_specs=[row(D_MODEL)] + [full(a) for a in weights] + [tab] * 4 + [tab_t(a) for a in tabs_t],
        out_specs=[kind(n) for kind, n, _ in outs],
        out_shape=[jax.ShapeDtypeStruct(shape(kind, n), d) for kind, n, d in outs],
        compiler_params=_cparams(("parallel",)),
        name="proj",
    )(x, *weights, *tabs, *tabs_t)


def _expand_kernel(ckv_ref, kpe_ref, w_ukv_ref, e_ref, km_ref, vm_ref):
    kv = jnp.dot(ckv_ref[...].astype(BF16), w_ukv_ref[...], preferred_element_type=F32)
    k_full = kv[:, :MLA_HEADS * HEAD_PAD] + jnp.dot(kpe_ref[...].astype(BF16), e_ref[:MLA_ROPE, :],
                                                   preferred_element_type=F32)
    km_ref[...] = k_full.astype(BF16)
    vm_ref[...] = kv[:, MLA_HEADS * HEAD_PAD:].astype(BF16)


def _expand(ckv, kpe, w, *, tm):
    r = ckv.shape[0]
    row = lambda n: pl.BlockSpec((tm, n), lambda i: (i, 0))
    full = lambda a: pl.BlockSpec(a.shape, lambda i: (0,) * a.ndim)
    return pl.pallas_call(
        _expand_kernel,
        grid=(r // tm,),
        in_specs=[row(MLA_KV_RANK), row(MLA_ROPE), full(w['w_ukv']), full(w['e_place'])],
        out_specs=[row(1024), row(512)],
        out_shape=[jax.ShapeDtypeStruct((r, 1024), BF16), jax.ShapeDtypeStruct((r, 512), BF16)],
        compiler_params=_cparams(("parallel",)),
        name="expand",
    )(ckv, kpe, w['w_ukv'], w['e_place'])


def _chunk_mask_t(t):
    shift = CHUNK.bit_length() - 1
    kc = lax.broadcasted_iota(jnp.int32, (t, t), 0) >> shift
    qc = lax.broadcasted_iota(jnp.int32, (t, t), 1) >> shift
    return kc <= qc


def _flash_scratch(t, dv, n):
    return [pltpu.VMEM((n, t, t), F32), pltpu.VMEM((n, t, t), F32),
            pltpu.VMEM((n, 1, t), F32), pltpu.VMEM((n, dv, t), F32)]


def _flash_loop(i, score, value, n, nq, s_a, s_b, m_ref, acc_ref, ahead):
    t = s_a.shape[-1]
    mask = _chunk_mask_t(t)
    m_ref[...] = jnp.full(m_ref.shape, -jnp.inf, F32)
    acc_ref[...] = jnp.zeros(acc_ref.shape, F32)
    chains_from = lambda q: tuple(range(q * n, nq * n))
    everything = chains_from(0)

    def step(j, s_cur, s_nxt, cur, masked, nxt):
        pending = list(nxt)

        def issue_scores(count):
            for k in pending[:count]:
                s_nxt[k] = score(j + 1, k // n, k % n)
            del pending[:count]

        issue_scores(ahead)
        for k in cur:
            issue_scores(1)
            s = s_cur[k]
            if k in masked:
                s = jnp.where(mask, s, -jnp.inf)
            m = m_ref[k]
            m_new = jnp.maximum(m, jnp.max(s, axis=0, keepdims=True))
            alpha = jnp.exp2(m - m_new)
            p = jnp.exp2(s - m_new)
            m_ref[k] = m_new
            acc_ref[k] = alpha * acc_ref[k] + jnp.dot(value(j, k % n), p.astype(BF16), preferred_element_type=F32)
        issue_scores(len(pending))

    for k in everything:
        s_a[k] = score(0, k // n, k % n)

    def pair(jj, carry):
        step(2 * jj, s_a, s_b, everything, (), everything)
        step(2 * jj + 1, s_b, s_a, everything, (), everything)
        return carry

    lax.fori_loop(0, (nq // 2) * i, pair, 0)
    bufs = (s_a, s_b)
    for e in range(nq):
        step(nq * i + e, bufs[e % 2], bufs[(e + 1) % 2], chains_from(e), chains_from(e)[:n], chains_from(e + 1))


def _normalised(acc_ref, k, dv):
    return acc_ref[k, :dv, :] / acc_ref[k, dv:dv + 1, :]


def _mla_flash_kernel(qt_ref, k_ref, vt_ref, o_ref, s_a, s_b, m_ref, acc_ref, *, t, heads, nq):
    def score(j, q, h):
        rows = pl.ds(pl.multiple_of(j * t, t), t)
        return jnp.dot(k_ref[0, rows, h * HEAD_PAD:(h + 1) * HEAD_PAD],
                       qt_ref[q, h * HEAD_PAD:(h + 1) * HEAD_PAD, :], preferred_element_type=F32)

    value = lambda j, h: vt_ref[j, h * MLA_VA:(h + 1) * MLA_VA, :]
    _flash_loop(pl.program_id(2), score, value, heads, nq, s_a, s_b, m_ref, acc_ref, ahead=2)
    for q in range(nq):
        for h in range(0, heads, 2):
            k = q * heads + h
            pair = jnp.concatenate([_normalised(acc_ref, k, MLA_V), _normalised(acc_ref, k + 1, MLA_V)], axis=0)
            o_ref[0, q * t:(q + 1) * t, h * MLA_V:(h + 2) * MLA_V] = pair.T.astype(BF16)


def _mla_flash(qt, km, vt, *, t, heads, nq):
    b, s, _ = km.shape
    nk = s // t
    steps = nk // nq
    return pl.pallas_call(
        functools.partial(_mla_flash_kernel, t=t, heads=heads, nq=nq),
        grid=(b, MLA_HEADS // heads, steps),
        in_specs=[pl.BlockSpec((nq, heads * HEAD_PAD, t), lambda bi, hi, i: (bi * steps + i, hi, 0)),
                  pl.BlockSpec((1, s, heads * HEAD_PAD), lambda bi, hi, i: (bi, 0, hi)),
                  pl.BlockSpec((nk, heads * MLA_VA, t), lambda bi, hi, i: (bi, hi, 0))],
        out_specs=pl.BlockSpec((1, nq * t, heads * MLA_V), lambda bi, hi, i: (bi, i, hi)),
        out_shape=jax.ShapeDtypeStruct((b, s, MLA_HEADS * MLA_V), BF16),
        scratch_shapes=_flash_scratch(t, MLA_VA, nq * heads),
        compiler_params=_cparams(("parallel", "parallel", "arbitrary")),
        name="mla_flash",
    )(qt, km, vt)


def _diff_lambda(lam_ref, lam_init):
    lq1, lk1, lq2, lk2 = (lam_ref[r:r + 1, :] for r in range(4))
    return (jnp.exp(jnp.sum(lq1 * lk1, axis=-1, keepdims=True))
            - jnp.exp(jnp.sum(lq2 * lk2, axis=-1, keepdims=True)) + lam_init)


def _diff_finish(o1, o2, lam, subln, lam_init):
    o = o1 - lam * o2
    return _rms(o, subln) * (1.0 - lam_init)


def _diff_flash_kernel(lam_ref, subln_ref, qt_ref, k_ref, vt_ref, o_ref, s_a, s_b, m_ref, acc_ref,
                       *, t, heads, nq, lam_init):
    def score(j, q, n):
        rows = pl.ds(pl.multiple_of(j * t, t), t)
        return jnp.dot(k_ref[0, rows, (n // 2) * LANES:(n // 2 + 1) * LANES],
                       qt_ref[q, n * LANES:(n + 1) * LANES, :], preferred_element_type=F32)

    value = lambda j, n: vt_ref[j, (n // 2) * DIFF_VA:(n // 2 + 1) * DIFF_VA, :]
    _flash_loop(pl.program_id(2), score, value, 2 * heads, nq, s_a, s_b, m_ref, acc_ref, ahead=1)
    lam = _diff_lambda(lam_ref, lam_init)
    for q in range(nq):
        for h in range(heads):
            k = q * 2 * heads + 2 * h
            o1 = _normalised(acc_ref, k, DIFF_V).T
            o2 = _normalised(acc_ref, k + 1, DIFF_V).T
            o = _diff_finish(o1, o2, lam, subln_ref[...], lam_init)
            o_ref[0, q * t:(q + 1) * t, h * DIFF_V:(h + 1) * DIFF_V] = o.astype(BF16)


def _diff_flash(lam_rows, subln, dqt, dkb, dvt, *, t, heads, nq, lam_init):
    b, s, _ = dkb.shape
    nk = s // t
    steps = nk // nq
    full = lambda a: pl.BlockSpec(a.shape, lambda bi, hi, i: (0,) * a.ndim)
    return pl.pallas_call(
        functools.partial(_diff_flash_kernel, t=t, heads=heads, nq=nq, lam_init=lam_init),
        grid=(b, DIFF_HEADS // heads, steps),
        in_specs=[full(lam_rows), full(subln),
                  pl.BlockSpec((nq, heads * 2 * LANES, t), lambda bi, hi, i: (bi * steps + i, hi, 0)),
                  pl.BlockSpec((1, s, heads * LANES), lambda bi, hi, i: (bi, 0, hi)),
                  pl.BlockSpec((nk, heads * DIFF_VA, t), lambda bi, hi, i: (bi, hi, 0))],
        out_specs=pl.BlockSpec((1, nq * t, heads * DIFF_V), lambda bi, hi, i: (bi, i, hi)),
        out_shape=jax.ShapeDtypeStruct((b, s, DIFF_HEADS * DIFF_V), BF16),
        scratch_shapes=_flash_scratch(t, DIFF_VA, nq * 2 * heads),
        compiler_params=_cparams(("parallel", "parallel", "arbitrary")),
        name="diff_flash",
    )(lam_rows, subln, dqt, dkb, dvt)


def _two_part_softmax_pv(s_past, s_new, v_past, v_new):
    m = jnp.maximum(jnp.max(s_past, axis=-1, keepdims=True), jnp.max(s_new, axis=-1, keepdims=True))
    p_past = jnp.exp2(s_past - m)
    p_new = jnp.exp2(s_new - m)
    l = jnp.sum(p_past, axis=-1, keepdims=True) + jnp.sum(p_new, axis=-1, keepdims=True)
    acc = (jnp.dot(p_past.astype(BF16), v_past, preferred_element_type=F32)
           + jnp.dot(p_new.astype(BF16), v_new, preferred_element_type=F32))
    return acc / l


def _token_major(qt_ref):
    return qt_ref[0].astype(F32).T.astype(BF16)


def _mla_step_kernel(qt_ref, kp_ref, vp_ref, kn_ref, vn_ref, o_ref):
    q_all = _token_major(qt_ref)
    for h in range(MLA_HEADS):
        ks = slice(h * HEAD_PAD, (h + 1) * HEAD_PAD)
        vs = slice(h * MLA_V, (h + 1) * MLA_V)
        q = q_all[:, ks]
        o = _two_part_softmax_pv(_nt_dot(q, kp_ref[0, :, ks]), _nt_dot(q, kn_ref[0, :, ks]),
                                 vp_ref[0, :, vs], vn_ref[0, :, vs])
        o_ref[0, :, vs] = o.astype(BF16)


def _mla_step(qt, km_past, vm_past, km_new, vm_new):
    b, _, n = qt.shape
    blk = lambda a: pl.BlockSpec((1,) + a.shape[1:], lambda bi: (bi, 0, 0))
    args = (qt, km_past, vm_past, km_new, vm_new)
    return pl.pallas_call(
        _mla_step_kernel,
        grid=(b,),
        in_specs=[blk(a) for a in args],
        out_specs=pl.BlockSpec((1, n, MLA_HEADS * MLA_V), lambda bi: (bi, 0, 0)),
        out_shape=jax.ShapeDtypeStruct((b, n, MLA_HEADS * MLA_V), BF16),
        compiler_params=_cparams(("parallel",)),
        name="mla_step",
    )(*args)


def _diff_step_kernel(lam_ref, subln_ref, qt_ref, kp_ref, vp_ref, kn_ref, vn_ref, o_ref, *, lam_init):
    lam = _diff_lambda(lam_ref, lam_init)
    q = _token_major(qt_ref)
    for h in range(DIFF_HEADS):
        sl = slice(h * LANES, (h + 1) * LANES)
        q1 = q[:, 2 * h * LANES:(2 * h + 1) * LANES]
        q2 = q[:, (2 * h + 1) * LANES:(2 * h + 2) * LANES]
        kp = kp_ref[0, :, h, :].astype(BF16)
        vp = vp_ref[0, :, h, :].astype(BF16)
        kn = kn_ref[0, :, sl]
        vn = vn_ref[0, :, sl]
        o1 = _two_part_softmax_pv(_nt_dot(q1, kp), _nt_dot(q1, kn), vp, vn)
        o2 = _two_part_softmax_pv(_nt_dot(q2, kp), _nt_dot(q2, kn), vp, vn)
        o_ref[0, :, sl] = _diff_finish(o1, o2, lam, subln_ref[...], lam_init).astype(BF16)


def _diff_step(lam_rows, subln, dqt, k_past, v_past, dkb, dvb, *, lam_init):
    b, _, n = dqt.shape
    full = lambda a: pl.BlockSpec(a.shape, lambda bi: (0,) * a.ndim)
    blk = lambda a: pl.BlockSpec((1,) + a.shape[1:], lambda bi: (bi,) + (0,) * (a.ndim - 1))
    args = (dqt, k_past, v_past, dkb, dvb)
    return pl.pallas_call(
        functools.partial(_diff_step_kernel, lam_init=lam_init),
        grid=(b,),
        in_specs=[full(lam_rows), full(subln)] + [blk(a) for a in args],
        out_specs=pl.BlockSpec((1, n, DIFF_HEADS * DIFF_V), lambda bi: (bi, 0, 0)),
        out_shape=jax.ShapeDtypeStruct((b, n, DIFF_HEADS * DIFF_V), BF16),
        compiler_params=_cparams(("parallel",)),
        name="diff_step",
    )(lam_rows, subln, *args)


def _first_index_of_max(vals, lane):
    m = jnp.max(vals, axis=-1, keepdims=True)
    idx = jnp.min(jnp.where(vals == m, lane, float(LANES)), axis=-1, keepdims=True)
    return m, idx


_SLAB_MEMBER = EXP_PER_GROUP
_SLAB_RANK = EXP_PER_GROUP + 1


def _route(hb, wr_ref, br_ref):
    logit = jnp.dot(hb, wr_ref[...], preferred_element_type=F32) + br_ref[...]
    lane_i = lax.broadcasted_iota(jnp.int32, logit.shape, 1)
    lane = lane_i.astype(F32)
    group_of_lane = (lane_i >> (EXP_PER_GROUP.bit_length() - 1)).astype(F32)
    neg = -jnp.inf
    g_logit = jnp.where((lane_i >= N_EXPERTS) & (lane_i < N_EXPERTS + N_GROUPS), logit, neg)
    g_max, g_lane = _first_index_of_max(g_logit, lane)
    g_w = 1.0 / jnp.sum(jnp.exp(g_logit - g_max), axis=-1, keepdims=True)
    g_idx = g_lane - float(N_EXPERTS)
    e_logit = jnp.where(group_of_lane == g_idx, logit, neg)
    m1, i1 = _first_index_of_max(e_logit, lane)
    m2, i2 = _first_index_of_max(jnp.where(lane == i1, neg, e_logit), lane)
    r = jnp.exp(m2 - m1)
    p1 = 1.0 / (1.0 + r)
    p2 = r / (1.0 + r)
    gate = g_w * (jnp.where(lane == i1, p1, 0.0) + jnp.where(lane == i2, p2, 0.0))
    return gate, g_idx, jnp.where(lane == g_idx, 1.0, 0.0)


def _post_attn_kernel(x_ref, mla_ref, diff_ref, wo_a_ref, wo_b_ref, g_ref, b_ref, wr_ref, br_ref, ltri_ref, utri_ref,
                      h_ref, hb_ref, slab_ref, rank_ref, cnt_ref, *, alpha):
    a = (jnp.dot(mla_ref[...], wo_a_ref[...], preferred_element_type=F32)
         + jnp.dot(diff_ref[...], wo_b_ref[...], preferred_element_type=F32))
    h = _layer_norm(alpha * x_ref[...] + a, g_ref[...], b_ref[...])
    h_ref[...] = h
    hb = h.astype(BF16)
    hb_ref[...] = hb
    gate, g_idx, onehot = _route(hb, wr_ref, br_ref)
    lane_i = lax.broadcasted_iota(jnp.int32, gate.shape, 1)

    rank_c = jnp.dot(ltri_ref[...], onehot.astype(BF16), preferred_element_type=F32)
    onehot_t = onehot.T[:BF16_ROWS]
    rank_r = jnp.dot(onehot_t.astype(BF16), utri_ref[...], preferred_element_type=F32)
    rank_ref[0] = jnp.where(onehot_t > 0.0, rank_r, -1.0)
    cnt_ref[0] = jnp.broadcast_to(jnp.sum(onehot, axis=0, keepdims=True), cnt_ref.shape[1:])

    for g in range(N_GROUPS):
        member = g_idx == float(g)
        rank_g = jnp.sum(jnp.where(lane_i == g, rank_c, 0.0), axis=-1, keepdims=True)
        rolled = gate if g == 0 else pltpu.roll(gate, LANES - EXP_PER_GROUP * g, 1)
        extra = jnp.where(lane_i == _SLAB_MEMBER, jnp.where(member, 1.0, 0.0),
                          jnp.where(lane_i == _SLAB_RANK, jnp.where(member, rank_g, -1.0), 0.0))
        slab_ref[:, g * LANES:(g + 1) * LANES] = jnp.where(lane_i < EXP_PER_GROUP, rolled, extra)


def _post_attn(x, mla_o, diff_o, w, *, tm, alpha):
    t = x.shape[0]
    nt = t // tm
    row = lambda n: pl.BlockSpec((tm, n), lambda i: (i, 0))
    full = lambda a: pl.BlockSpec(a.shape, lambda i: (0,) * a.ndim)
    per_tile = lambda r, c: pl.BlockSpec((1, r, c), lambda i: (i, 0, 0))
    idx = jnp.arange(tm)
    ltri = (idx[None, :] < idx[:, None]).astype(BF16)
    weights = (w['w_out_a'], w['w_out_b'], w['ln1_g'], w['ln1_b'], w['w_route'], w['b_route'], ltri, ltri.T)
    return pl.pallas_call(
        functools.partial(_post_attn_kernel, alpha=alpha),
        grid=(nt,),
        in_specs=[row(D_MODEL), row(512), row(512)] + [full(a) for a in weights],
        out_specs=[row(D_MODEL), row(D_MODEL), row(N_GROUPS * LANES), per_tile(BF16_ROWS, tm), per_tile(8, LANES)],
        out_shape=[jax.ShapeDtypeStruct((t, D_MODEL), F32), jax.ShapeDtypeStruct((t, D_MODEL), BF16),
                   jax.ShapeDtypeStruct((t, N_GROUPS * LANES), F32),
                   jax.ShapeDtypeStruct((nt, BF16_ROWS, tm), F32), jax.ShapeDtypeStruct((nt, 8, LANES), F32)],
        compiler_params=_cparams(("parallel",)),
        name="post_attn",
    )(x, mla_o, diff_o, *weights)


MOE_TILE = 1024
MOE_SLACK_ROWS = 32
MOE_EXTRA_ROWS = 128


def _pack_gate_terms(slab):
    lane = lax.broadcasted_iota(jnp.int32, slab.shape, 1)
    gates = jnp.where(lane < EXP_PER_GROUP, slab, 0.0)
    hi = gates.astype(BF16).astype(F32)
    r1 = gates - hi
    mid = r1.astype(BF16).astype(F32)
    lo = r1 - mid
    return (hi + pltpu.roll(mid, EXP_PER_GROUP, 1) + pltpu.roll(lo, 2 * EXP_PER_GROUP, 1)).astype(BF16)


def _unpack_gate_terms(packed):
    return (packed + pltpu.roll(packed, LANES - EXP_PER_GROUP, 1)
            + pltpu.roll(packed, LANES - 2 * EXP_PER_GROUP, 1))


def _moe_kernel(cnt_ref, hb_ref, slab_ref, rank_ref, wg_ref, wu_ref, wd_ref, yin_ref, g_ref, b_ref, y_ref,
                *, g, alpha):
    n_tok = cnt_ref[pl.program_id(0), g]
    slab = slab_ref[...]
    rank_col = slab[:, _SLAB_RANK:_SLAB_RANK + 1]
    rank_row = rank_ref[0, g:g + 1, :]
    gate_terms = _pack_gate_terms(slab)

    y_ref[...] = alpha * yin_ref[...] if g == 0 else yin_ref[...]

    def one_pass(base, rows):
        sub = lax.broadcasted_iota(jnp.int32, (rows, 1), 0).astype(F32)
        lan2 = lax.broadcasted_iota(jnp.int32, (1, 2 * rows), 1)
        row_of_lane = jnp.where(lan2 < rows, lan2, lan2 - rows).astype(F32)
        sel = jnp.where(rank_row - base == sub, 1.0, 0.0).astype(BF16)
        sel_t2 = jnp.where(rank_col - base == row_of_lane, 1.0, 0.0).astype(BF16)
        x = jnp.dot(sel, hb_ref[...], preferred_element_type=F32).astype(BF16)
        gate_c = _unpack_gate_terms(jnp.dot(sel, gate_terms, preferred_element_type=F32))
        ys = jnp.zeros((rows, D_MODEL), F32)
        for e in range(EXP_PER_GROUP):
            hid = (jax.nn.silu(jnp.dot(x, wg_ref[e], preferred_element_type=F32))
                   * jnp.dot(x, wu_ref[e], preferred_element_type=F32))
            ys = ys + jnp.dot((hid * gate_c[:, e:e + 1]).astype(BF16), wd_ref[e], preferred_element_type=F32)
        ys_hi = ys.astype(BF16)
        ys_lo = (ys - ys_hi.astype(F32)).astype(BF16)
        y_ref[...] += jnp.dot(sel_t2, jnp.concatenate([ys_hi, ys_lo], axis=0), preferred_element_type=F32)

    first_rows = slab.shape[0] // N_GROUPS + MOE_SLACK_ROWS
    one_pass(0.0, first_rows)

    def extra_pass(c, carry):
        one_pass((first_rows + c * MOE_EXTRA_ROWS).astype(F32), MOE_EXTRA_ROWS)
        return carry

    lax.fori_loop(0, (jnp.maximum(n_tok - first_rows, 0) + MOE_EXTRA_ROWS - 1) // MOE_EXTRA_ROWS, extra_pass, 0)

    if g == N_GROUPS - 1:
        y_ref[...] = _layer_norm(y_ref[...], g_ref[...], b_ref[...])


def _moe_group(g, y, hb, slab, rank_r, cnt, w, *, tm, alpha):
    t = y.shape[0]
    row = lambda n: pl.BlockSpec((tm, n), lambda i, c: (i, 0))
    full = lambda a: pl.BlockSpec(a.shape, lambda i, c: (0,) * a.ndim)
    grp = lambda a: pl.BlockSpec((EXP_PER_GROUP,) + a.shape[1:], lambda i, c: (g, 0, 0),
                                 pipeline_mode=pl.Buffered(1))
    grid_spec = pltpu.PrefetchScalarGridSpec(
        num_scalar_prefetch=1,
        grid=(t // tm,),
        in_specs=[row(D_MODEL), pl.BlockSpec((tm, LANES), lambda i, c: (i, g)),
                  pl.BlockSpec((1, BF16_ROWS, tm), lambda i, c: (i, 0, 0)),
                  grp(w['w_gate']), grp(w['w_up']), grp(w['w_down']), row(D_MODEL),
                  full(w['ln2_g']), full(w['ln2_b'])],
        out_specs=row(D_MODEL))
    return pl.pallas_call(
        functools.partial(_moe_kernel, g=g, alpha=alpha),
        grid_spec=grid_spec,
        out_shape=jax.ShapeDtypeStruct((t, D_MODEL), F32),
        compiler_params=_cparams(("arbitrary",)),
        name=f"moe_group{g}",
    )(cnt, hb, slab, rank_r, w['w_gate'], w['w_up'], w['w_down'], y, w['ln2_g'], w['ln2_b'])


def _moe(h, hb, slab, rank_r, cnt, w, *, tm, alpha):
    y = h
    for g in range(N_GROUPS):
        y = _moe_group(g, y, hb, slab, rank_r, cnt, w, tm=tm, alpha=alpha)
    return y


def _rope_tables(pos):
    pos = pos.astype(F32)[:, None]
    lane = jnp.arange(LANES)
    inv_m = ROPE_THETA ** (-jnp.arange(0, MLA_ROPE, 2, dtype=F32) / MLA_ROPE)
    ang_m = pos * inv_m
    inv_d = ROPE_THETA ** (-jnp.arange(0, DIFF_ROT, 2, dtype=F32) / DIFF_ROT)
    ang_d = pos * inv_d
    narrow = lax.optimization_barrier((jnp.cos(ang_m), jnp.sin(ang_m), jnp.cos(ang_d), jnp.sin(ang_d)))
    tabs = tuple(a[:, lane % a.shape[1]] for a in narrow)
    tabs_t = tuple(a.T for a in narrow)
    return tabs, tabs_t


def _prep_weights(l, w_in, mla_q_norm, mla_w_uq, mla_kv_norm, mla_w_ukv, diff_subln, w_out, ln1_g, ln1_b,
                  ln2_g, ln2_b, w_route_group, b_route_group, w_route_expert, b_route_expert,
                  w_exp_gate, w_exp_up, w_exp_down, lam_rows):
    wi = w_in[l]
    s0, s1, s2, s3, s4 = 256, 512, 544, 1056, 1568
    w_in_r = jnp.concatenate([wi[:, :s0], wi[:, s0:s1], wi[:, s3:s4], wi[:, s4:],
                              wi[:, s1:s2], jnp.zeros((D_MODEL, LANES - MLA_ROPE), F32)], axis=1)
    w_uq = jnp.pad(mla_w_uq[l].reshape(MLA_Q_RANK, MLA_HEADS, MLA_NOPE + MLA_ROPE),
                   ((0, 0), (0, 0), (0, HEAD_PAD - MLA_NOPE - MLA_ROPE))).reshape(MLA_Q_RANK, -1)
    ukv = mla_w_ukv[l].reshape(MLA_KV_RANK, MLA_HEADS, MLA_NOPE + MLA_V)
    w_uk = jnp.pad(ukv[..., :MLA_NOPE], ((0, 0), (0, 0), (0, HEAD_PAD - MLA_NOPE))).reshape(MLA_KV_RANK, -1)
    w_uv = ukv[..., MLA_NOPE:].reshape(MLA_KV_RANK, -1)
    rows = jnp.arange(LANES)[:, None]
    cols = jnp.arange(MLA_HEADS * HEAD_PAD)[None, :]
    e_place = ((rows < MLA_ROPE) & (cols % HEAD_PAD == rows + MLA_NOPE)).astype(BF16)
    w_route = jnp.concatenate([w_route_expert[l], w_route_group[l],
                               jnp.zeros((D_MODEL, LANES - N_EXPERTS - N_GROUPS), F32)], axis=1)
    b_route = jnp.concatenate([b_route_expert[l], b_route_group[l],
                               jnp.zeros((LANES - N_EXPERTS - N_GROUPS,), F32)])[None, :]
    return dict(
        w_in=w_in_r.astype(BF16), w_dqt=wi[:, s2:s3].T.astype(BF16),
        q_norm=mla_q_norm[l][None, :], w_uqt=w_uq.T.astype(BF16),
        kv_norm=mla_kv_norm[l][None, :], w_ukv=jnp.concatenate([w_uk, w_uv], axis=1).astype(BF16),
        w_uvt=w_uv.T.astype(BF16),
        e_place=e_place, subln=diff_subln[l][None, :], lam_rows=lam_rows,
        w_out_a=w_out[l][:MLA_HEADS * MLA_V].astype(BF16), w_out_b=w_out[l][MLA_HEADS * MLA_V:].astype(BF16),
        ln1_g=ln1_g[l][None, :], ln1_b=ln1_b[l][None, :], ln2_g=ln2_g[l][None, :], ln2_b=ln2_b[l][None, :],
        w_route=w_route.astype(BF16), b_route=b_route,
        w_gate=w_exp_gate[l].astype(BF16), w_up=w_exp_up[l].astype(BF16), w_down=w_exp_down[l].astype(BF16))


def _block_output(x, mla_o, diff_o, w, alpha):
    t = x.shape[0]
    tm = min(MOE_TILE, t)
    h, hb, slab, rank_r, cnt = _post_attn(x, mla_o.reshape(t, -1), diff_o.reshape(t, -1), w, tm=tm, alpha=alpha)
    cnt = cnt[:, 0, :N_GROUPS].astype(jnp.int32)
    return _moe(h, hb, slab, rank_r, cnt, w, tm=tm, alpha=alpha)


def kernel(x_prompt, x_sample, cache_mla_ckv, cache_mla_kpe, cache_diff_k, cache_diff_v, w_in, mla_q_norm, mla_w_uq, mla_kv_norm, mla_w_ukv, diff_lambda_q1, diff_lambda_k1, diff_lambda_q2, diff_lambda_k2, diff_subln, w_out, ln1_g, ln1_b, ln2_g, ln2_b, w_route_group, b_route_group, w_route_expert, b_route_expert, w_exp_gate, w_exp_up, w_exp_down):
    depth = w_in.shape[0]
    bp, sp, _ = x_prompt.shape
    bs, ss, _ = x_sample.shape
    past = cache_mla_ckv.shape[2]
    alpha = (2.0 * depth) ** 0.25
    tabs_p, tabs_pt = _rope_tables(jnp.arange(sp))
    tabs_s, tabs_st = _rope_tables(past + jnp.arange(ss))
    hp = x_prompt.reshape(bp * sp, D_MODEL)
    hs = x_sample.reshape(bs * ss, D_MODEL)
    outs = [[] for _ in range(8)]
    for l in range(depth):
        lam_init = 0.8 - 0.6 * math.exp(-0.3 * l)
        lam_rows = jnp.stack([diff_lambda_q1[l], diff_lambda_k1[l], diff_lambda_q2[l], diff_lambda_k2[l]])
        w = _prep_weights(l, w_in, mla_q_norm, mla_w_uq, mla_kv_norm, mla_w_ukv, diff_subln, w_out,
                          ln1_g, ln1_b, ln2_g, ln2_b, w_route_group, b_route_group, w_route_expert,
                          b_route_expert, w_exp_gate, w_exp_up, w_exp_down, lam_rows)

        ckv, kpe, dk, dv, qt, km, _, vt, dqt, dkb, _, dvt = _proj(hp, tabs_p, tabs_pt, w, tm=ATTN_TILE)
        b3 = lambda a: a.reshape(bp, sp, -1)
        mla_o = _mla_flash(qt, b3(km), vt, t=ATTN_TILE, heads=4, nq=ATTN_QTILES)
        diff_o = _diff_flash(w['lam_rows'], w['subln'], dqt, b3(dkb), dvt, t=ATTN_TILE, heads=2,
                             nq=ATTN_QTILES, lam_init=lam_init)
        outs[0].append(ckv.reshape(bp, sp, MLA_KV_RANK))
        outs[1].append(kpe.reshape(bp, sp, MLA_ROPE))
        outs[2].append(dk.reshape(bp, sp, DIFF_HEADS, 2 * DIFF_D))
        outs[3].append(dv.reshape(bp, sp, DIFF_HEADS, DIFF_V))
        hp = _block_output(hp, mla_o, diff_o, w, alpha)

        ckv_s, kpe_s, dk_s, dv_s, qt_s, km_s, vm_s, _, dqt_s, dkb_s, dvb_s, _ = _proj(hs, tabs_s, tabs_st, w, tm=ss)
        km_past, vm_past = _expand(cache_mla_ckv[l].reshape(bs * past, MLA_KV_RANK),
                                   cache_mla_kpe[l].reshape(bs * past, MLA_ROPE), w, tm=512)
        s3 = lambda a: a.reshape(bs, ss, -1)
        p3 = lambda a: a.reshape(bs, past, -1)
        mla_o_s = _mla_step(qt_s, p3(km_past), p3(vm_past), s3(km_s), s3(vm_s))
        diff_o_s = _diff_step(w['lam_rows'], w['subln'], dqt_s, cache_diff_k[l], cache_diff_v[l],
                              s3(dkb_s), s3(dvb_s), lam_init=lam_init)
        outs[4].append(ckv_s.reshape(bs, ss, MLA_KV_RANK))
        outs[5].append(kpe_s.reshape(bs, ss, MLA_ROPE))
        outs[6].append(dk_s.reshape(bs, ss, DIFF_HEADS, 2 * DIFF_D))
        outs[7].append(dv_s.reshape(bs, ss, DIFF_HEADS, DIFF_V))
        hs = _block_output(hs, mla_o_s, diff_o_s, w, alpha)

    return (hp.reshape(bp, sp, D_MODEL), hs.reshape(bs, ss, D_MODEL)) + tuple(jnp.stack(o) for o in outs)
```

```python
import functools
import math

import jax
import jax.numpy as jnp
from jax import lax
from jax.experimental import pallas as pl
from jax.experimental.pallas import tpu as pltpu

F32 = jnp.float32
BF16 = jnp.bfloat16

D_MODEL = 1024
CHUNK = 64
ROPE_THETA = 500000.0
MLA_HEADS = 8
MLA_NOPE = 64
MLA_ROPE = 32
MLA_V = 64
MLA_Q_RANK = 256
MLA_KV_RANK = 256
MLA_SCALE = (MLA_NOPE + MLA_ROPE) ** -0.5
DIFF_HEADS = 4
DIFF_D = 64
DIFF_V = 2 * DIFF_D
DIFF_ROT = DIFF_D // 4
DIFF_SCALE = DIFF_D ** -0.5
N_GROUPS = 4
EXP_PER_GROUP = 8
N_EXPERTS = N_GROUPS * EXP_PER_GROUP
D_EXPERT = 256
EPS_LN = 1e-5
EPS_RMS = 1e-6
LOG2E = math.log2(math.e)

LANES = 128
HEAD_PAD = LANES
ATTN_TILE = 256
PROJ_TILE = ATTN_TILE
ATTN_QTILES = 4
BF16_ROWS = 16
MLA_VA = MLA_V + BF16_ROWS
DIFF_VA = DIFF_V + BF16_ROWS
VMEM_LIMIT = 56 * 1024 * 1024

_CQ, _CKV, _DK, _DV, _KR, _IN_COLS_R = 0, 256, 512, 1024, 1536, 1664


def _cparams(sem):
    return pltpu.CompilerParams(dimension_semantics=sem, vmem_limit_bytes=VMEM_LIMIT)


def _rms(x, g):
    return x * lax.rsqrt(jnp.mean(x * x, axis=-1, keepdims=True) + EPS_RMS) * g


def _layer_norm(x, g, b):
    mu = jnp.mean(x, axis=-1, keepdims=True)
    xc = x - mu
    var = jnp.mean(xc * xc, axis=-1, keepdims=True)
    return xc * lax.rsqrt(var + EPS_LN) * g + b


def _nt_dot(a, b):
    return lax.dot_general(a, b, (((1,), (1,)), ((), ())), preferred_element_type=F32)


def _rope_coeffs(cos_t, sin_t, lo, half, period):
    lane = lax.broadcasted_iota(jnp.int32, cos_t.shape, 1) & (period - 1)
    is1 = (lane >= lo) & (lane < lo + half)
    is2 = (lane >= lo + half) & (lane < lo + 2 * half)
    c = jnp.where(is1 | is2, cos_t, 1.0)
    a = jnp.where(is1, -sin_t, 0.0)
    b = jnp.where(is2, sin_t, 0.0)
    return c, a, b


def _rope_apply(x, coeffs, half):
    c, a, b = coeffs
    return x * c + pltpu.roll(x, LANES - half, 1) * a + pltpu.roll(x, half, 1) * b


def _rope_rows(x, cos_t, sin_t):
    half = cos_t.shape[0]
    x1, x2 = x[:half], x[half:]
    return x1 * cos_t - x2 * sin_t, x1 * sin_t + x2 * cos_t


def _store_tiles(ref, r0, r1, val):
    tile = ref.shape[-1]
    for c in range(val.shape[-1] // tile):
        ref[c, r0:r1, :] = val[:, c * tile:(c + 1) * tile]


def _proj_kernel(x_ref, w_in_ref, w_dqt_ref, qn_ref, w_uqt_ref, kvn_ref, w_ukv_ref, w_uvt_ref, e_ref,
                 cm_ref, sm_ref, cd_ref, sd_ref, cmt_ref, smt_ref, cdt_ref, sdt_ref,
                 ckv_ref, kpe_ref, dk_ref, dv_ref, qt_ref, km_ref, vm_ref, vt_ref,
                 dqt_ref, dkb_ref, dvb_ref, dvt_ref):
    xb = x_ref[...].astype(BF16)
    tm = xb.shape[0]
    z = jnp.dot(xb, w_in_ref[...], preferred_element_type=F32)
    half_m = MLA_ROPE // 2
    half_d = DIFF_ROT // 2
    coef_k = _rope_coeffs(cm_ref[...], sm_ref[...], 0, half_m, LANES)
    coef_d = _rope_coeffs(cd_ref[...], sd_ref[...], 0, half_d, DIFF_D)

    cq = _rms(z[:, _CQ:_CQ + MLA_Q_RANK], qn_ref[...])
    qt = _nt_dot(w_uqt_ref[...], cq.astype(BF16)) * (MLA_SCALE * LOG2E)
    cos_mt, sin_mt = cmt_ref[...], smt_ref[...]
    for h in range(MLA_HEADS):
        r0 = h * HEAD_PAD
        _store_tiles(qt_ref, r0, r0 + MLA_NOPE, qt[r0:r0 + MLA_NOPE].astype(BF16))
        o1, o2 = _rope_rows(qt[r0 + MLA_NOPE:r0 + MLA_NOPE + MLA_ROPE], cos_mt, sin_mt)
        _store_tiles(qt_ref, r0 + MLA_NOPE, r0 + MLA_NOPE + half_m, o1.astype(BF16))
        _store_tiles(qt_ref, r0 + MLA_NOPE + half_m, r0 + MLA_NOPE + MLA_ROPE, o2.astype(BF16))
        _store_tiles(qt_ref, r0 + MLA_NOPE + MLA_ROPE, r0 + HEAD_PAD,
                     jnp.zeros((HEAD_PAD - MLA_NOPE - MLA_ROPE, tm), BF16))

    ckv = _rms(z[:, _CKV:_CKV + MLA_KV_RANK], kvn_ref[...])
    ckv_ref[...] = ckv
    ckv_b = ckv.astype(BF16)
    kslab = _rope_apply(z[:, _KR:_KR + LANES], coef_k, half_m)
    kpe_ref[...] = kslab[:, :MLA_ROPE]
    kv = jnp.dot(ckv_b, w_ukv_ref[...], preferred_element_type=F32)
    k_full = kv[:, :MLA_HEADS * HEAD_PAD] + jnp.dot(kslab.astype(BF16), e_ref[...], preferred_element_type=F32)
    km_ref[...] = k_full.astype(BF16)
    vm_ref[...] = kv[:, MLA_HEADS * HEAD_PAD:].astype(BF16)
    ones_rows = jnp.where(lax.broadcasted_iota(jnp.int32, (BF16_ROWS, tm), 0) == 0, 1.0, 0.0).astype(BF16)
    vt = _nt_dot(w_uvt_ref[...], ckv_b)
    for h in range(MLA_HEADS):
        _store_tiles(vt_ref, h * MLA_VA, h * MLA_VA + MLA_V, vt[h * MLA_V:(h + 1) * MLA_V].astype(BF16))
        _store_tiles(vt_ref, h * MLA_VA + MLA_V, (h + 1) * MLA_VA, ones_rows)

    dqt = _nt_dot(w_dqt_ref[...], xb) * (DIFF_SCALE * LOG2E)
    cos_dt, sin_dt = cdt_ref[...], sdt_ref[...]
    zeros = jnp.zeros((DIFF_D, tm), BF16)
    for n in range(2 * DIFF_HEADS):
        r0 = n * DIFF_D
        o1, o2 = _rope_rows(dqt[r0:r0 + DIFF_ROT], cos_dt, sin_dt)
        qn = jnp.concatenate([o1, o2, dqt[r0 + DIFF_ROT:r0 + DIFF_D]], axis=0).astype(BF16)
        lo, hi = (qn, zeros) if n % 2 == 0 else (zeros, qn)
        _store_tiles(dqt_ref, n * LANES, n * LANES + DIFF_D, lo)
        _store_tiles(dqt_ref, n * LANES + DIFF_D, (n + 1) * LANES, hi)

    dv = z[:, _DV:_DV + DIFF_HEADS * DIFF_V]
    dvb_ref[...] = dv.astype(BF16)
    dvt = dv.T
    for h in range(DIFF_HEADS):
        sl = slice(h * LANES, (h + 1) * LANES)
        dk = _rope_apply(z[:, _DK + h * LANES:_DK + (h + 1) * LANES], coef_d, half_d)
        dk_ref[:, h, :] = dk
        dkb_ref[:, sl] = dk.astype(BF16)
        dv_ref[:, h, :] = dv[:, sl]
        _store_tiles(dvt_ref, h * DIFF_VA, h * DIFF_VA + DIFF_V, dvt[sl].astype(BF16))
        _store_tiles(dvt_ref, h * DIFF_VA + DIFF_V, (h + 1) * DIFF_VA, ones_rows)


def _proj(x, tabs, tabs_t, w, *, tm):
    t = x.shape[0]
    n_pos_blocks = tabs[0].shape[0] // tm
    row = lambda n: pl.BlockSpec((tm, n), lambda i: (i, 0))
    full = lambda a: pl.BlockSpec(a.shape, lambda i: (0,) * a.ndim)
    tab = pl.BlockSpec((tm, LANES), lambda i: (i % n_pos_blocks, 0))
    tab_t = lambda a: pl.BlockSpec((a.shape[0], tm), lambda i: (0, i % n_pos_blocks))
    tile = min(tm, ATTN_TILE)
    tr = lambda n: pl.BlockSpec((tm // tile, n, tile), lambda i: (i, 0, 0))
    hd = lambda n: pl.BlockSpec((tm, DIFF_HEADS, n), lambda i: (i, 0, 0))
    weights = (w['w_in'], w['w_dqt'], w['q_norm'], w['w_uqt'], w['kv_norm'], w['w_ukv'], w['w_uvt'], w['e_place'])
    outs = ((row, MLA_KV_RANK, F32), (row, MLA_ROPE, F32), (hd, 2 * DIFF_D, F32), (hd, DIFF_V, F32),
            (tr, 1024, BF16), (row, 1024, BF16), (row, 512, BF16), (tr, MLA_HEADS * MLA_VA, BF16),
            (tr, 1024, BF16), (row, 512, BF16), (row, 512, BF16), (tr, DIFF_HEADS * DIFF_VA, BF16))
    shape = lambda kind, n: {tr: (t // tile, n, tile), hd: (t, DIFF_HEADS, n), row: (t, n)}[kind]
    return pl.pallas_call(
        _proj_kernel,
        grid=(t // tm,),
        in_specs=[row(D_MODEL)] + [full(a) for a in weights] + [tab] * 4 + [tab_t(a) for a in tabs_t],
        out_specs=[kind(n) for kind, n, _ in outs],
        out_shape=[jax.ShapeDtypeStruct(shape(kind, n), d) for kind, n, d in outs],
        compiler_params=_cparams(("parallel",)),
        name="proj",
    )(x, *weights, *tabs, *tabs_t)


def _chunk_mask_t(t):
    shift = CHUNK.bit_length() - 1
    kc = lax.broadcasted_iota(jnp.int32, (t, t), 0) >> shift
    qc = lax.broadcasted_iota(jnp.int32, (t, t), 1) >> shift
    return kc <= qc


def _flash_scratch(t, dv, n):
    return [pltpu.VMEM((n, t, t), F32), pltpu.VMEM((n, t, t), F32),
            pltpu.VMEM((n, 1, t), F32), pltpu.VMEM((n, dv, t), F32)]


def _flash_loop(i, score, value, n, nq, s_a, s_b, m_ref, acc_ref, ahead):
    t = s_a.shape[-1]
    mask = _chunk_mask_t(t)
    m_ref[...] = jnp.full(m_ref.shape, -jnp.inf, F32)
    acc_ref[...] = jnp.zeros(acc_ref.shape, F32)
    chains_from = lambda q: tuple(range(q * n, nq * n))
    everything = chains_from(0)

    def step(j, s_cur, s_nxt, cur, masked, nxt):
        pending = list(nxt)

        def issue_scores(count):
            for k in pending[:count]:
                s_nxt[k] = score(j + 1, k // n, k % n)
            del pending[:count]

        issue_scores(ahead)
        for k in cur:
            issue_scores(1)
            s = s_cur[k]
            if k in masked:
                s = jnp.where(mask, s, -jnp.inf)
            m = m_ref[k]
            m_new = jnp.maximum(m, jnp.max(s, axis=0, keepdims=True))
            alpha = jnp.exp2(m - m_new)
            p = jnp.exp2(s - m_new)
            m_ref[k] = m_new
            acc_ref[k] = alpha * acc_ref[k] + jnp.dot(value(j, k % n), p.astype(BF16), preferred_element_type=F32)
        issue_scores(len(pending))

    for k in everything:
        s_a[k] = score(0, k // n, k % n)

    def pair(jj, carry):
        step(2 * jj, s_a, s_b, everything, (), everything)
        step(2 * jj + 1, s_b, s_a, everything, (), everything)
        return carry

    lax.fori_loop(0, (nq // 2) * i, pair, 0)
    bufs = (s_a, s_b)
    for e in range(nq):
        step(nq * i + e, bufs[e % 2], bufs[(e + 1) % 2], chains_from(e), chains_from(e)[:n], chains_from(e + 1))


def _normalised(acc_ref, k, dv):
    return acc_ref[k, :dv, :] / acc_ref[k, dv:dv + 1, :]


def _mla_flash_kernel(qt_ref, k_ref, vt_ref, o_ref, s_a, s_b, m_ref, acc_ref, *, t, heads, nq):
    def score(j, q, h):
        rows = pl.ds(pl.multiple_of(j * t, t), t)
        return jnp.dot(k_ref[0, rows, h * HEAD_PAD:(h + 1) * HEAD_PAD],
                       qt_ref[q, h * HEAD_PAD:(h + 1) * HEAD_PAD, :], preferred_element_type=F32)

    value = lambda j, h: vt_ref[j, h * MLA_VA:(h + 1) * MLA_VA, :]
    _flash_loop(pl.program_id(2), score, value, heads, nq, s_a, s_b, m_ref, acc_ref, ahead=2)
    for q in range(nq):
        for h in range(0, heads, 2):
            k = q * heads + h
            pair = jnp.concatenate([_normalised(acc_ref, k, MLA_V), _normalised(acc_ref, k + 1, MLA_V)], axis=0)
            o_ref[0, q * t:(q + 1) * t, h * MLA_V:(h + 2) * MLA_V] = pair.T.astype(BF16)


def _mla_flash(qt, km, vt, *, t, heads, nq):
    b, s, _ = km.shape
    nk = s // t
    steps = nk // nq
    return pl.pallas_call(
        functools.partial(_mla_flash_kernel, t=t, heads=heads, nq=nq),
        grid=(b, MLA_HEADS // heads, steps),
        in_specs=[pl.BlockSpec((nq, heads * HEAD_PAD, t), lambda bi, hi, i: (bi * steps + i, hi, 0)),
                  pl.BlockSpec((1, s, heads * HEAD_PAD), lambda bi, hi, i: (bi, 0, hi)),
                  pl.BlockSpec((nk, heads * MLA_VA, t), lambda bi, hi, i: (bi, hi, 0))],
        out_specs=pl.BlockSpec((1, nq * t, heads * MLA_V), lambda bi, hi, i: (bi, i, hi)),
        out_shape=jax.ShapeDtypeStruct((b, s, MLA_HEADS * MLA_V), BF16),
        scratch_shapes=_flash_scratch(t, MLA_VA, nq * heads),
        compiler_params=_cparams(("parallel", "parallel", "arbitrary")),
        name="mla_flash",
    )(qt, km, vt)


def _diff_lambda(lam_ref, lam_init):
    lq1, lk1, lq2, lk2 = (lam_ref[r:r + 1, :] for r in range(4))
    return (jnp.exp(jnp.sum(lq1 * lk1, axis=-1, keepdims=True))
            - jnp.exp(jnp.sum(lq2 * lk2, axis=-1, keepdims=True)) + lam_init)


def _diff_finish(o1, o2, lam, subln, lam_init):
    o = o1 - lam * o2
    return _rms(o, subln) * (1.0 - lam_init)


def _diff_flash_kernel(lam_ref, subln_ref, qt_ref, k_ref, vt_ref, o_ref, s_a, s_b, m_ref, acc_ref,
                       *, t, heads, nq, lam_init):
    def score(j, q, n):
        rows = pl.ds(pl.multiple_of(j * t, t), t)
        return jnp.dot(k_ref[0, rows, (n // 2) * LANES:(n // 2 + 1) * LANES],
                       qt_ref[q, n * LANES:(n + 1) * LANES, :], preferred_element_type=F32)

    value = lambda j, n: vt_ref[j, (n // 2) * DIFF_VA:(n // 2 + 1) * DIFF_VA, :]
    _flash_loop(pl.program_id(2), score, value, 2 * heads, nq, s_a, s_b, m_ref, acc_ref, ahead=1)
    lam = _diff_lambda(lam_ref, lam_init)
    for q in range(nq):
        for h in range(heads):
            k = q * 2 * heads + 2 * h
            o1 = _normalised(acc_ref, k, DIFF_V).T
            o2 = _normalised(acc_ref, k + 1, DIFF_V).T
            o = _diff_finish(o1, o2, lam, subln_ref[...], lam_init)
            o_ref[0, q * t:(q + 1) * t, h * DIFF_V:(h + 1) * DIFF_V] = o.astype(BF16)


def _diff_flash(lam_rows, subln, dqt, dkb, dvt, *, t, heads, nq, lam_init):
    b, s, _ = dkb.shape
    nk = s // t
    steps = nk // nq
    full = lambda a: pl.BlockSpec(a.shape, lambda bi, hi, i: (0,) * a.ndim)
    return pl.pallas_call(
        functools.partial(_diff_flash_kernel, t=t, heads=heads, nq=nq, lam_init=lam_init),
        grid=(b, DIFF_HEADS // heads, steps),
        in_specs=[full(lam_rows), full(subln),
                  pl.BlockSpec((nq, heads * 2 * LANES, t), lambda bi, hi, i: (bi * steps + i, hi, 0)),
                  pl.BlockSpec((1, s, heads * LANES), lambda bi, hi, i: (bi, 0, hi)),
                  pl.BlockSpec((nk, heads * DIFF_VA, t), lambda bi, hi, i: (bi, hi, 0))],
        out_specs=pl.BlockSpec((1, nq * t, heads * DIFF_V), lambda bi, hi, i: (bi, i, hi)),
        out_shape=jax.ShapeDtypeStruct((b, s, DIFF_HEADS * DIFF_V), BF16),
        scratch_shapes=_flash_scratch(t, DIFF_VA, nq * 2 * heads),
        compiler_params=_cparams(("parallel", "parallel", "arbitrary")),
        name="diff_flash",
    )(lam_rows, subln, dqt, dkb, dvt)


def _two_part_softmax_pv(s_past, s_new, v_past, v_new):
    m = jnp.maximum(jnp.max(s_past, axis=-1, keepdims=True), jnp.max(s_new, axis=-1, keepdims=True))
    p_past = jnp.exp2(s_past - m)
    p_new = jnp.exp2(s_new - m)
    l = jnp.sum(p_past, axis=-1, keepdims=True) + jnp.sum(p_new, axis=-1, keepdims=True)
    acc = (jnp.dot(p_past.astype(BF16), v_past, preferred_element_type=F32)
           + jnp.dot(p_new.astype(BF16), v_new, preferred_element_type=F32))
    return acc / l


def _token_major(qt_ref):
    return qt_ref[0].astype(F32).T.astype(BF16)


def _mla_step_kernel(qt_ref, ckv_ref, kpe_ref, w_ukv_ref, e_ref, kn_ref, vn_ref, o_ref):
    q_all = _token_major(qt_ref)
    kv = jnp.dot(ckv_ref[0].astype(BF16), w_ukv_ref[...], preferred_element_type=F32)
    k_past = (kv[:, :MLA_HEADS * HEAD_PAD]
              + jnp.dot(kpe_ref[0].astype(BF16), e_ref[:MLA_ROPE, :], preferred_element_type=F32)).astype(BF16)
    v_past = kv[:, MLA_HEADS * HEAD_PAD:].astype(BF16)
    for h in range(MLA_HEADS):
        ks = slice(h * HEAD_PAD, (h + 1) * HEAD_PAD)
        vs = slice(h * MLA_V, (h + 1) * MLA_V)
        q = q_all[:, ks]
        o = _two_part_softmax_pv(_nt_dot(q, k_past[:, ks]), _nt_dot(q, kn_ref[0, :, ks]),
                                 v_past[:, vs], vn_ref[0, :, vs])
        o_ref[0, :, vs] = o.astype(BF16)


def _mla_step(qt, ckv_past, kpe_past, km_new, vm_new, w):
    b, _, n = qt.shape
    blk = lambda a: pl.BlockSpec((1,) + a.shape[1:], lambda bi: (bi, 0, 0))
    full = lambda a: pl.BlockSpec(a.shape, lambda bi: (0,) * a.ndim)
    return pl.pallas_call(
        _mla_step_kernel,
        grid=(b,),
        in_specs=[blk(qt), blk(ckv_past), blk(kpe_past), full(w['w_ukv']), full(w['e_place']),
                  blk(km_new), blk(vm_new)],
        out_specs=pl.BlockSpec((1, n, MLA_HEADS * MLA_V), lambda bi: (bi, 0, 0)),
        out_shape=jax.ShapeDtypeStruct((b, n, MLA_HEADS * MLA_V), BF16),
        compiler_params=_cparams(("parallel",)),
        name="mla_step",
    )(qt, ckv_past, kpe_past, w['w_ukv'], w['e_place'], km_new, vm_new)


def _diff_step_kernel(lam_ref, subln_ref, qt_ref, kp_ref, vp_ref, kn_ref, vn_ref, o_ref, *, lam_init):
    lam = _diff_lambda(lam_ref, lam_init)
    q = _token_major(qt_ref)
    for h in range(DIFF_HEADS):
        sl = slice(h * LANES, (h + 1) * LANES)
        q1 = q[:, 2 * h * LANES:(2 * h + 1) * LANES]
        q2 = q[:, (2 * h + 1) * LANES:(2 * h + 2) * LANES]
        kp = kp_ref[0, :, h, :].astype(BF16)
        vp = vp_ref[0, :, h, :].astype(BF16)
        kn = kn_ref[0, :, sl]
        vn = vn_ref[0, :, sl]
        o1 = _two_part_softmax_pv(_nt_dot(q1, kp), _nt_dot(q1, kn), vp, vn)
        o2 = _two_part_softmax_pv(_nt_dot(q2, kp), _nt_dot(q2, kn), vp, vn)
        o_ref[0, :, sl] = _diff_finish(o1, o2, lam, subln_ref[...], lam_init).astype(BF16)


def _diff_step(lam_rows, subln, dqt, k_past, v_past, dkb, dvb, *, lam_init):
    b, _, n = dqt.shape
    full = lambda a: pl.BlockSpec(a.shape, lambda bi: (0,) * a.ndim)
    blk = lambda a: pl.BlockSpec((1,) + a.shape[1:], lambda bi: (bi,) + (0,) * (a.ndim - 1))
    args = (dqt, k_past, v_past, dkb, dvb)
    return pl.pallas_call(
        functools.partial(_diff_step_kernel, lam_init=lam_init),
        grid=(b,),
        in_specs=[full(lam_rows), full(subln)] + [blk(a) for a in args],
        out_specs=pl.BlockSpec((1, n, DIFF_HEADS * DIFF_V), lambda bi: (bi, 0, 0)),
        out_shape=jax.ShapeDtypeStruct((b, n, DIFF_HEADS * DIFF_V), BF16),
        compiler_params=_cparams(("parallel",)),
        name="diff_step",
    )(lam_rows, subln, *args)


def _first_index_of_max(vals, lane):
    m = jnp.max(vals, axis=-1, keepdims=True)
    idx = jnp.min(jnp.where(vals == m, lane, float(LANES)), axis=-1, keepdims=True)
    return m, idx


_SLAB_MEMBER = EXP_PER_GROUP
_SLAB_RANK = EXP_PER_GROUP + 1


def _route(hb, wr_ref, br_ref):
    logit = jnp.dot(hb, wr_ref[...], preferred_element_type=F32) + br_ref[...]
    lane_i = lax.broadcasted_iota(jnp.int32, logit.shape, 1)
    lane = lane_i.astype(F32)
    group_of_lane = (lane_i >> (EXP_PER_GROUP.bit_length() - 1)).astype(F32)
    neg = -jnp.inf
    g_logit = jnp.where((lane_i >= N_EXPERTS) & (lane_i < N_EXPERTS + N_GROUPS), logit, neg)
    g_max, g_lane = _first_index_of_max(g_logit, lane)
    g_w = 1.0 / jnp.sum(jnp.exp(g_logit - g_max), axis=-1, keepdims=True)
    g_idx = g_lane - float(N_EXPERTS)
    e_logit = jnp.where(group_of_lane == g_idx, logit, neg)
    m1, i1 = _first_index_of_max(e_logit, lane)
    m2, i2 = _first_index_of_max(jnp.where(lane == i1, neg, e_logit), lane)
    r = jnp.exp(m2 - m1)
    p1 = 1.0 / (1.0 + r)
    p2 = r / (1.0 + r)
    gate = g_w * (jnp.where(lane == i1, p1, 0.0) + jnp.where(lane == i2, p2, 0.0))
    return gate, g_idx, jnp.where(lane == g_idx, 1.0, 0.0)


def _post_attn_kernel(x_ref, mla_ref, diff_ref, wo_a_ref, wo_b_ref, g_ref, b_ref, wr_ref, br_ref, ltri_ref, utri_ref,
                      h_ref, hb_ref, slab_ref, rank_ref, cnt_ref, *, alpha):
    a = (jnp.dot(mla_ref[...], wo_a_ref[...], preferred_element_type=F32)
         + jnp.dot(diff_ref[...], wo_b_ref[...], preferred_element_type=F32))
    h = _layer_norm(alpha * x_ref[...] + a, g_ref[...], b_ref[...])
    h_ref[...] = h
    hb = h.astype(BF16)
    hb_ref[...] = hb
    gate, g_idx, onehot = _route(hb, wr_ref, br_ref)
    lane_i = lax.broadcasted_iota(jnp.int32, gate.shape, 1)

    rank_c = jnp.dot(ltri_ref[...], onehot.astype(BF16), preferred_element_type=F32)
    onehot_t = onehot.T[:BF16_ROWS]
    rank_r = jnp.dot(onehot_t.astype(BF16), utri_ref[...], preferred_element_type=F32)
    rank_ref[0] = jnp.where(onehot_t > 0.0, rank_r, -1.0)
    cnt_ref[0] = jnp.broadcast_to(jnp.sum(onehot, axis=0, keepdims=True), cnt_ref.shape[1:])

    for g in range(N_GROUPS):
        member = g_idx == float(g)
        rank_g = jnp.sum(jnp.where(lane_i == g, rank_c, 0.0), axis=-1, keepdims=True)
        rolled = gate if g == 0 else pltpu.roll(gate, LANES - EXP_PER_GROUP * g, 1)
        extra = jnp.where(lane_i == _SLAB_MEMBER, jnp.where(member, 1.0, 0.0),
                          jnp.where(lane_i == _SLAB_RANK, jnp.where(member, rank_g, -1.0), 0.0))
        slab_ref[:, g * LANES:(g + 1) * LANES] = jnp.where(lane_i < EXP_PER_GROUP, rolled, extra)


def _post_attn(x, mla_o, diff_o, w, *, tm, alpha):
    t = x.shape[0]
    nt = t // tm
    row = lambda n: pl.BlockSpec((tm, n), lambda i: (i, 0))
    full = lambda a: pl.BlockSpec(a.shape, lambda i: (0,) * a.ndim)
    per_tile = lambda r, c: pl.BlockSpec((1, r, c), lambda i: (i, 0, 0))
    idx = jnp.arange(tm)
    ltri = (idx[None, :] < idx[:, None]).astype(BF16)
    weights = (w['w_out_a'], w['w_out_b'], w['ln1_g'], w['ln1_b'], w['w_route'], w['b_route'], ltri, ltri.T)
    return pl.pallas_call(
        functools.partial(_post_attn_kernel, alpha=alpha),
        grid=(nt,),
        in_specs=[row(D_MODEL), row(512), row(512)] + [full(a) for a in weights],
        out_specs=[row(D_MODEL), row(D_MODEL), row(N_GROUPS * LANES), per_tile(BF16_ROWS, tm), per_tile(8, LANES)],
        out_shape=[jax.ShapeDtypeStruct((t, D_MODEL), F32), jax.ShapeDtypeStruct((t, D_MODEL), BF16),
                   jax.ShapeDtypeStruct((t, N_GROUPS * LANES), F32),
                   jax.ShapeDtypeStruct((nt, BF16_ROWS, tm), F32), jax.ShapeDtypeStruct((nt, 8, LANES), F32)],
        compiler_params=_cparams(("parallel",)),
        name="post_attn",
    )(x, mla_o, diff_o, *weights)


MOE_TILE = 1024
MOE_SLACK_ROWS = 32
MOE_EXTRA_ROWS = 128


def _pack_gate_terms(slab):
    lane = lax.broadcasted_iota(jnp.int32, slab.shape, 1)
    gates = jnp.where(lane < EXP_PER_GROUP, slab, 0.0)
    hi = gates.astype(BF16).astype(F32)
    r1 = gates - hi
    mid = r1.astype(BF16).astype(F32)
    lo = r1 - mid
    return (hi + pltpu.roll(mid, EXP_PER_GROUP, 1) + pltpu.roll(lo, 2 * EXP_PER_GROUP, 1)).astype(BF16)


def _unpack_gate_terms(packed):
    return (packed + pltpu.roll(packed, LANES - EXP_PER_GROUP, 1)
            + pltpu.roll(packed, LANES - 2 * EXP_PER_GROUP, 1))


def _moe_kernel(cnt_ref, hb_ref, slab_ref, rank_ref, wg_ref, wu_ref, wd_ref, yin_ref, g_ref, b_ref, y_ref,
                *, g, alpha):
    n_tok = cnt_ref[pl.program_id(0), g]
    slab = slab_ref[...]
    rank_col = slab[:, _SLAB_RANK:_SLAB_RANK + 1]
    rank_row = rank_ref[0, g:g + 1, :]
    gate_terms = _pack_gate_terms(slab)

    y_ref[...] = alpha * yin_ref[...] if g == 0 else yin_ref[...]

    def one_pass(base, rows):
        sub = lax.broadcasted_iota(jnp.int32, (rows, 1), 0).astype(F32)
        lan2 = lax.broadcasted_iota(jnp.int32, (1, 2 * rows), 1)
        row_of_lane = jnp.where(lan2 < rows, lan2, lan2 - rows).astype(F32)
        sel = jnp.where(rank_row - base == sub, 1.0, 0.0).astype(BF16)
        sel_t2 = jnp.where(rank_col - base == row_of_lane, 1.0, 0.0).astype(BF16)
        x = jnp.dot(sel, hb_ref[...], preferred_element_type=F32).astype(BF16)
        gate_c = _unpack_gate_terms(jnp.dot(sel, gate_terms, preferred_element_type=F32))
        ys = jnp.zeros((rows, D_MODEL), F32)
        for e in range(EXP_PER_GROUP):
            hid = (jax.nn.silu(jnp.dot(x, wg_ref[e].astype(BF16), preferred_element_type=F32))
                   * jnp.dot(x, wu_ref[e].astype(BF16), preferred_element_type=F32))
            ys = ys + jnp.dot((hid * gate_c[:, e:e + 1]).astype(BF16), wd_ref[e].astype(BF16),
                              preferred_element_type=F32)
        ys_hi = ys.astype(BF16)
        ys_lo = (ys - ys_hi.astype(F32)).astype(BF16)
        y_ref[...] += jnp.dot(sel_t2, jnp.concatenate([ys_hi, ys_lo], axis=0), preferred_element_type=F32)

    first_rows = slab.shape[0] // N_GROUPS + MOE_SLACK_ROWS
    one_pass(0.0, first_rows)

    def extra_pass(c, carry):
        one_pass((first_rows + c * MOE_EXTRA_ROWS).astype(F32), MOE_EXTRA_ROWS)
        return carry

    lax.fori_loop(0, (jnp.maximum(n_tok - first_rows, 0) + MOE_EXTRA_ROWS - 1) // MOE_EXTRA_ROWS, extra_pass, 0)

    if g == N_GROUPS - 1:
        y_ref[...] = _layer_norm(y_ref[...], g_ref[...], b_ref[...])


def _moe_group(g, y, hb, slab, rank_r, cnt, w, *, tm, alpha):
    t = y.shape[0]
    row = lambda n: pl.BlockSpec((tm, n), lambda i, c: (i, 0))
    full = lambda a: pl.BlockSpec(a.shape, lambda i, c: (0,) * a.ndim)
    grp = lambda a: pl.BlockSpec((EXP_PER_GROUP,) + a.shape[1:], lambda i, c: (g, 0, 0),
                                 pipeline_mode=pl.Buffered(1))
    grid_spec = pltpu.PrefetchScalarGridSpec(
        num_scalar_prefetch=1,
        grid=(t // tm,),
        in_specs=[row(D_MODEL), pl.BlockSpec((tm, LANES), lambda i, c: (i, g)),
                  pl.BlockSpec((1, BF16_ROWS, tm), lambda i, c: (i, 0, 0)),
                  grp(w['w_gate']), grp(w['w_up']), grp(w['w_down']), row(D_MODEL),
                  full(w['ln2_g']), full(w['ln2_b'])],
        out_specs=row(D_MODEL))
    return pl.pallas_call(
        functools.partial(_moe_kernel, g=g, alpha=alpha),
        grid_spec=grid_spec,
        out_shape=jax.ShapeDtypeStruct((t, D_MODEL), F32),
        compiler_params=_cparams(("arbitrary",)),
        name=f"moe_group{g}",
    )(cnt, hb, slab, rank_r, w['w_gate'], w['w_up'], w['w_down'], y, w['ln2_g'], w['ln2_b'])


def _moe(h, hb, slab, rank_r, cnt, w, *, tm, alpha):
    y = h
    for g in range(N_GROUPS):
        y = _moe_group(g, y, hb, slab, rank_r, cnt, w, tm=tm, alpha=alpha)
    return y


def _rope_tables(pos):
    pos = pos.astype(F32)[:, None]
    lane = jnp.arange(LANES)
    inv_m = ROPE_THETA ** (-jnp.arange(0, MLA_ROPE, 2, dtype=F32) / MLA_ROPE)
    ang_m = pos * inv_m
    inv_d = ROPE_THETA ** (-jnp.arange(0, DIFF_ROT, 2, dtype=F32) / DIFF_ROT)
    ang_d = pos * inv_d
    narrow = lax.optimization_barrier((jnp.cos(ang_m), jnp.sin(ang_m), jnp.cos(ang_d), jnp.sin(ang_d)))
    tabs = tuple(a[:, lane % a.shape[1]] for a in narrow)
    tabs_t = tuple(a.T for a in narrow)
    return tabs, tabs_t


def _prep_weights(l, w_in, mla_q_norm, mla_w_uq, mla_kv_norm, mla_w_ukv, diff_subln, w_out, ln1_g, ln1_b,
                  ln2_g, ln2_b, w_route_group, b_route_group, w_route_expert, b_route_expert,
                  w_exp_gate, w_exp_up, w_exp_down, lam_rows):
    wi = w_in[l]
    s0, s1, s2, s3, s4 = 256, 512, 544, 1056, 1568
    w_in_r = jnp.concatenate([wi[:, :s0], wi[:, s0:s1], wi[:, s3:s4], wi[:, s4:],
                              wi[:, s1:s2], jnp.zeros((D_MODEL, LANES - MLA_ROPE), F32)], axis=1)
    w_uq = jnp.pad(mla_w_uq[l].reshape(MLA_Q_RANK, MLA_HEADS, MLA_NOPE + MLA_ROPE),
                   ((0, 0), (0, 0), (0, HEAD_PAD - MLA_NOPE - MLA_ROPE))).reshape(MLA_Q_RANK, -1)
    ukv = mla_w_ukv[l].reshape(MLA_KV_RANK, MLA_HEADS, MLA_NOPE + MLA_V)
    w_uk = jnp.pad(ukv[..., :MLA_NOPE], ((0, 0), (0, 0), (0, HEAD_PAD - MLA_NOPE))).reshape(MLA_KV_RANK, -1)
    w_uv = ukv[..., MLA_NOPE:].reshape(MLA_KV_RANK, -1)
    rows = jnp.arange(LANES)[:, None]
    cols = jnp.arange(MLA_HEADS * HEAD_PAD)[None, :]
    e_place = ((rows < MLA_ROPE) & (cols % HEAD_PAD == rows + MLA_NOPE)).astype(BF16)
    w_route = jnp.concatenate([w_route_expert[l], w_route_group[l],
                               jnp.zeros((D_MODEL, LANES - N_EXPERTS - N_GROUPS), F32)], axis=1)
    b_route = jnp.concatenate([b_route_expert[l], b_route_group[l],
                               jnp.zeros((LANES - N_EXPERTS - N_GROUPS,), F32)])[None, :]
    return dict(
        w_in=w_in_r.astype(BF16), w_dqt=wi[:, s2:s3].T.astype(BF16),
        q_norm=mla_q_norm[l][None, :], w_uqt=w_uq.T.astype(BF16),
        kv_norm=mla_kv_norm[l][None, :], w_ukv=jnp.concatenate([w_uk, w_uv], axis=1).astype(BF16),
        w_uvt=w_uv.T.astype(BF16),
        e_place=e_place, subln=diff_subln[l][None, :], lam_rows=lam_rows,
        w_out_a=w_out[l][:MLA_HEADS * MLA_V].astype(BF16), w_out_b=w_out[l][MLA_HEADS * MLA_V:].astype(BF16),
        ln1_g=ln1_g[l][None, :], ln1_b=ln1_b[l][None, :], ln2_g=ln2_g[l][None, :], ln2_b=ln2_b[l][None, :],
        w_route=w_route.astype(BF16), b_route=b_route,
        w_gate=w_exp_gate[l], w_up=w_exp_up[l], w_down=w_exp_down[l])


def _block_output(x, mla_o, diff_o, w, alpha):
    t = x.shape[0]
    tm = min(MOE_TILE, t)
    h, hb, slab, rank_r, cnt = _post_attn(x, mla_o.reshape(t, -1), diff_o.reshape(t, -1), w, tm=tm, alpha=alpha)
    cnt = cnt[:, 0, :N_GROUPS].astype(jnp.int32)
    return _moe(h, hb, slab, rank_r, cnt, w, tm=tm, alpha=alpha)


def kernel(x_prompt, x_sample, cache_mla_ckv, cache_mla_kpe, cache_diff_k, cache_diff_v, w_in, mla_q_norm, mla_w_uq, mla_kv_norm, mla_w_ukv, diff_lambda_q1, diff_lambda_k1, diff_lambda_q2, diff_lambda_k2, diff_subln, w_out, ln1_g, ln1_b, ln2_g, ln2_b, w_route_group, b_route_group, w_route_expert, b_route_expert, w_exp_gate, w_exp_up, w_exp_down):
    depth = w_in.shape[0]
    bp, sp, _ = x_prompt.shape
    bs, ss, _ = x_sample.shape
    past = cache_mla_ckv.shape[2]
    alpha = (2.0 * depth) ** 0.25
    tabs_p, tabs_pt = _rope_tables(jnp.arange(sp))
    tabs_s, tabs_st = _rope_tables(past + jnp.arange(ss))
    hp = x_prompt.reshape(bp * sp, D_MODEL)
    hs = x_sample.reshape(bs * ss, D_MODEL)
    outs = [[] for _ in range(8)]
    for l in range(depth):
        lam_init = 0.8 - 0.6 * math.exp(-0.3 * l)
        lam_rows = jnp.stack([diff_lambda_q1[l], diff_lambda_k1[l], diff_lambda_q2[l], diff_lambda_k2[l]])
        w = _prep_weights(l, w_in, mla_q_norm, mla_w_uq, mla_kv_norm, mla_w_ukv, diff_subln, w_out,
                          ln1_g, ln1_b, ln2_g, ln2_b, w_route_group, b_route_group, w_route_expert,
                          b_route_expert, w_exp_gate, w_exp_up, w_exp_down, lam_rows)

        ckv, kpe, dk, dv, qt, km, _, vt, dqt, dkb, _, dvt = _proj(hp, tabs_p, tabs_pt, w, tm=PROJ_TILE)
        b3 = lambda a: a.reshape(bp, sp, -1)
        mla_o = _mla_flash(qt, b3(km), vt, t=ATTN_TILE, heads=4, nq=ATTN_QTILES)
        diff_o = _diff_flash(w['lam_rows'], w['subln'], dqt, b3(dkb), dvt, t=ATTN_TILE, heads=2,
                             nq=ATTN_QTILES, lam_init=lam_init)
        outs[0].append(ckv.reshape(bp, sp, MLA_KV_RANK))
        outs[1].append(kpe.reshape(bp, sp, MLA_ROPE))
        outs[2].append(dk.reshape(bp, sp, DIFF_HEADS, 2 * DIFF_D))
        outs[3].append(dv.reshape(bp, sp, DIFF_HEADS, DIFF_V))
        hp = _block_output(hp, mla_o, diff_o, w, alpha)

        ckv_s, kpe_s, dk_s, dv_s, qt_s, km_s, vm_s, _, dqt_s, dkb_s, dvb_s, _ = _proj(hs, tabs_s, tabs_st, w, tm=ss)
        s3 = lambda a: a.reshape(bs, ss, -1)
        mla_o_s = _mla_step(qt_s, cache_mla_ckv[l], cache_mla_kpe[l], s3(km_s), s3(vm_s), w)
        diff_o_s = _diff_step(w['lam_rows'], w['subln'], dqt_s, cache_diff_k[l], cache_diff_v[l],
                              s3(dkb_s), s3(dvb_s), lam_init=lam_init)
        outs[4].append(ckv_s.reshape(bs, ss, MLA_KV_RANK))
        outs[5].append(kpe_s.reshape(bs, ss, MLA_ROPE))
        outs[6].append(dk_s.reshape(bs, ss, DIFF_HEADS, 2 * DIFF_D))
        outs[7].append(dv_s.reshape(bs, ss, DIFF_HEADS, DIFF_V))
        hs = _block_output(hs, mla_o_s, diff_o_s, w, alpha)

    return (hp.reshape(bp, sp, D_MODEL), hs.reshape(bs, ss, D_MODEL)) + tuple(jnp.stack(o) for o in outs)
```

```python
import functools
import math

import jax
import jax.numpy as jnp
from jax import lax
from jax.experimental import pallas as pl
from jax.experimental.pallas import tpu as pltpu

F32 = jnp.float32
BF16 = jnp.bfloat16

D_MODEL = 1024
CHUNK = 64
ROPE_THETA = 500000.0
MLA_HEADS = 8
MLA_NOPE = 64
MLA_ROPE = 32
MLA_V = 64
MLA_Q_RANK = 256
MLA_KV_RANK = 256
MLA_SCALE = (MLA_NOPE + MLA_ROPE) ** -0.5
DIFF_HEADS = 4
DIFF_D = 64
DIFF_V = 2 * DIFF_D
DIFF_ROT = DIFF_D // 4
DIFF_SCALE = DIFF_D ** -0.5
N_GROUPS = 4
EXP_PER_GROUP = 8
N_EXPERTS = N_GROUPS * EXP_PER_GROUP
D_EXPERT = 256
EPS_LN = 1e-5
EPS_RMS = 1e-6
LOG2E = math.log2(math.e)

LANES = 128
HEAD_PAD = LANES
ATTN_TILE = 256
PROJ_TILE = ATTN_TILE
ATTN_QTILES = 4
BF16_ROWS = 16
MLA_VA = MLA_V + BF16_ROWS
DIFF_VA = DIFF_V + BF16_ROWS
VMEM_LIMIT = 56 * 1024 * 1024

_CQ, _CKV, _DK, _DV, _KR, _IN_COLS_R = 0, 256, 512, 1024, 1536, 1664


def _cparams(sem):
    return pltpu.CompilerParams(dimension_semantics=sem, vmem_limit_bytes=VMEM_LIMIT)


def _rms(x, g):
    return x * lax.rsqrt(jnp.mean(x * x, axis=-1, keepdims=True) + EPS_RMS) * g


def _layer_norm(x, g, b):
    mu = jnp.mean(x, axis=-1, keepdims=True)
    xc = x - mu
    var = jnp.mean(xc * xc, axis=-1, keepdims=True)
    return xc * lax.rsqrt(var + EPS_LN) * g + b


def _nt_dot(a, b):
    return lax.dot_general(a, b, (((1,), (1,)), ((), ())), preferred_element_type=F32)


def _rope_coeffs(cos_t, sin_t, lo, half, period):
    lane = lax.broadcasted_iota(jnp.int32, cos_t.shape, 1) & (period - 1)
    is1 = (lane >= lo) & (lane < lo + half)
    is2 = (lane >= lo + half) & (lane < lo + 2 * half)
    c = jnp.where(is1 | is2, cos_t, 1.0)
    a = jnp.where(is1, -sin_t, 0.0)
    b = jnp.where(is2, sin_t, 0.0)
    return c, a, b


def _rope_apply(x, coeffs, half):
    c, a, b = coeffs
    return x * c + pltpu.roll(x, LANES - half, 1) * a + pltpu.roll(x, half, 1) * b


def _rope_rows(x, cos_t, sin_t):
    half = cos_t.shape[0]
    x1, x2 = x[:half], x[half:]
    return x1 * cos_t - x2 * sin_t, x1 * sin_t + x2 * cos_t


def _store_tiles(ref, r0, r1, val):
    tile = ref.shape[-1]
    for c in range(val.shape[-1] // tile):
        ref[c, r0:r1, :] = val[:, c * tile:(c + 1) * tile]


def _proj_kernel(x_ref, w_in_ref, w_dqt_ref, qn_ref, w_uqt_ref, kvn_ref, w_ukv_ref, w_uvt_ref, e_ref,
                 cm_ref, sm_ref, cd_ref, sd_ref, cmt_ref, smt_ref, cdt_ref, sdt_ref,
                 ckv_ref, kpe_ref, dk_ref, dv_ref, qt_ref, km_ref, vm_ref, vt_ref,
                 dqt_ref, dkb_ref, dvb_ref, dvt_ref):
    xb = x_ref[...].astype(BF16)
    tm = xb.shape[0]
    z = jnp.dot(xb, w_in_ref[...], preferred_element_type=F32)
    half_m = MLA_ROPE // 2
    half_d = DIFF_ROT // 2
    coef_k = _rope_coeffs(cm_ref[...], sm_ref[...], 0, half_m, LANES)
    coef_d = _rope_coeffs(cd_ref[...], sd_ref[...], 0, half_d, DIFF_D)

    cq = _rms(z[:, _CQ:_CQ + MLA_Q_RANK], qn_ref[...])
    qt = _nt_dot(w_uqt_ref[...], cq.astype(BF16)) * (MLA_SCALE * LOG2E)
    cos_mt, sin_mt = cmt_ref[...], smt_ref[...]
    for h in range(MLA_HEADS):
        r0 = h * HEAD_PAD
        _store_tiles(qt_ref, r0, r0 + MLA_NOPE, qt[r0:r0 + MLA_NOPE].astype(BF16))
        o1, o2 = _rope_rows(qt[r0 + MLA_NOPE:r0 + MLA_NOPE + MLA_ROPE], cos_mt, sin_mt)
        _store_tiles(qt_ref, r0 + MLA_NOPE, r0 + MLA_NOPE + half_m, o1.astype(BF16))
        _store_tiles(qt_ref, r0 + MLA_NOPE + half_m, r0 + MLA_NOPE + MLA_ROPE, o2.astype(BF16))
        _store_tiles(qt_ref, r0 + MLA_NOPE + MLA_ROPE, r0 + HEAD_PAD,
                     jnp.zeros((HEAD_PAD - MLA_NOPE - MLA_ROPE, tm), BF16))

    ckv = _rms(z[:, _CKV:_CKV + MLA_KV_RANK], kvn_ref[...])
    ckv_ref[...] = ckv
    ckv_b = ckv.astype(BF16)
    kslab = _rope_apply(z[:, _KR:_KR + LANES], coef_k, half_m)
    kpe_ref[...] = kslab[:, :MLA_ROPE]
    kv = jnp.dot(ckv_b, w_ukv_ref[...], preferred_element_type=F32)
    k_full = kv[:, :MLA_HEADS * HEAD_PAD] + jnp.dot(kslab.astype(BF16), e_ref[...], preferred_element_type=F32)
    km_ref[...] = k_full.astype(BF16)
    vm_ref[...] = kv[:, MLA_HEADS * HEAD_PAD:].astype(BF16)
    ones_rows = jnp.where(lax.broadcasted_iota(jnp.int32, (BF16_ROWS, tm), 0) == 0, 1.0, 0.0).astype(BF16)
    vt = _nt_dot(w_uvt_ref[...], ckv_b)
    for h in range(MLA_HEADS):
        _store_tiles(vt_ref, h * MLA_VA, h * MLA_VA + MLA_V, vt[h * MLA_V:(h + 1) * MLA_V].astype(BF16))
        _store_tiles(vt_ref, h * MLA_VA + MLA_V, (h + 1) * MLA_VA, ones_rows)

    dqt = _nt_dot(w_dqt_ref[...], xb) * (DIFF_SCALE * LOG2E)
    cos_dt, sin_dt = cdt_ref[...], sdt_ref[...]
    zeros = jnp.zeros((DIFF_D, tm), BF16)
    for n in range(2 * DIFF_HEADS):
        r0 = n * DIFF_D
        o1, o2 = _rope_rows(dqt[r0:r0 + DIFF_ROT], cos_dt, sin_dt)
        qn = jnp.concatenate([o1, o2, dqt[r0 + DIFF_ROT:r0 + DIFF_D]], axis=0).astype(BF16)
        lo, hi = (qn, zeros) if n % 2 == 0 else (zeros, qn)
        _store_tiles(dqt_ref, n * LANES, n * LANES + DIFF_D, lo)
        _store_tiles(dqt_ref, n * LANES + DIFF_D, (n + 1) * LANES, hi)

    dv = z[:, _DV:_DV + DIFF_HEADS * DIFF_V]
    dvb_ref[...] = dv.astype(BF16)
    dvt = dv.T
    for h in range(DIFF_HEADS):
        sl = slice(h * LANES, (h + 1) * LANES)
        dk = _rope_apply(z[:, _DK + h * LANES:_DK + (h + 1) * LANES], coef_d, half_d)
        dk_ref[:, h, :] = dk
        dkb_ref[:, sl] = dk.astype(BF16)
        dv_ref[:, h, :] = dv[:, sl]
        _store_tiles(dvt_ref, h * DIFF_VA, h * DIFF_VA + DIFF_V, dvt[sl].astype(BF16))
        _store_tiles(dvt_ref, h * DIFF_VA + DIFF_V, (h + 1) * DIFF_VA, ones_rows)


def _proj(x, tabs, tabs_t, w, *, tm):
    t = x.shape[0]
    n_pos_blocks = tabs[0].shape[0] // tm
    row = lambda n: pl.BlockSpec((tm, n), lambda i: (i, 0))
    full = lambda a: pl.BlockSpec(a.shape, lambda i: (0,) * a.ndim)
    tab = pl.BlockSpec((tm, LANES), lambda i: (i % n_pos_blocks, 0))
    tab_t = lambda a: pl.BlockSpec((a.shape[0], tm), lambda i: (0, i % n_pos_blocks))
    tile = min(tm, ATTN_TILE)
    tr = lambda n: pl.BlockSpec((tm // tile, n, tile), lambda i: (i, 0, 0))
    hd = lambda n: pl.BlockSpec((tm, DIFF_HEADS, n), lambda i: (i, 0, 0))
    weights = (w['w_in'], w['w_dqt'], w['q_norm'], w['w_uqt'], w['kv_norm'], w['w_ukv'], w['w_uvt'], w['e_place'])
    outs = ((row, MLA_KV_RANK, F32), (row, MLA_ROPE, F32), (hd, 2 * DIFF_D, F32), (hd, DIFF_V, F32),
            (tr, 1024, BF16), (row, 1024, BF16), (row, 512, BF16), (tr, MLA_HEADS * MLA_VA, BF16),
            (tr, 1024, BF16), (row, 512, BF16), (row, 512, BF16), (tr, DIFF_HEADS * DIFF_VA, BF16))
    shape = lambda kind, n: {tr: (t // tile, n, tile), hd: (t, DIFF_HEADS, n), row: (t, n)}[kind]
    return pl.pallas_call(
        _proj_kernel,
        grid=(t // tm,),
        in_specs=[row(D_MODEL)] + [full(a) for a in weights] + [tab] * 4 + [tab_t(a) for a in tabs_t],
        out_specs=[kind(n) for kind, n, _ in outs],
        out_shape=[jax.ShapeDtypeStruct(shape(kind, n), d) for kind, n, d in outs],
        compiler_params=_cparams(("parallel",)),
        name="proj",
    )(x, *weights, *tabs, *tabs_t)


def _chunk_mask_t(t):
    shift = CHUNK.bit_length() - 1
    kc = lax.broadcasted_iota(jnp.int32, (t, t), 0) >> shift
    qc = lax.broadcasted_iota(jnp.int32, (t, t), 1) >> shift
    return kc <= qc


_RESIDENT = pl.Buffered(1)


def _flash_scratch(t, dv, n):
    return [pltpu.VMEM((n, t, t), F32), pltpu.VMEM((n, t, t), F32),
            pltpu.VMEM((n, 1, t), F32), pltpu.VMEM((n, dv, t), F32)]


def _flash_loop(i, score, value, n, nq, s_a, s_b, m_ref, acc_ref, ahead):
    t = s_a.shape[-1]
    mask = _chunk_mask_t(t)
    m_ref[...] = jnp.full(m_ref.shape, -jnp.inf, F32)
    acc_ref[...] = jnp.zeros(acc_ref.shape, F32)
    chains_from = lambda q: tuple(range(q * n, nq * n))
    everything = chains_from(0)

    def step(j, s_cur, s_nxt, cur, masked, nxt):
        pending = list(nxt)

        def issue_scores(count):
            for k in pending[:count]:
                s_nxt[k] = score(j + 1, k // n, k % n)
            del pending[:count]

        issue_scores(ahead)
        for k in cur:
            issue_scores(1)
            s = s_cur[k]
            if k in masked:
                s = jnp.where(mask, s, -jnp.inf)
            m = m_ref[k]
            m_new = jnp.maximum(m, jnp.max(s, axis=0, keepdims=True))
            alpha = jnp.exp2(m - m_new)
            p = jnp.exp2(s - m_new)
            m_ref[k] = m_new
            acc_ref[k] = alpha * acc_ref[k] + jnp.dot(value(j, k % n), p.astype(BF16), preferred_element_type=F32)
        issue_scores(len(pending))

    for k in everything:
        s_a[k] = score(0, k // n, k % n)

    def pair(jj, carry):
        step(2 * jj, s_a, s_b, everything, (), everything)
        step(2 * jj + 1, s_b, s_a, everything, (), everything)
        return carry

    lax.fori_loop(0, (nq // 2) * i, pair, 0)
    bufs = (s_a, s_b)
    for e in range(nq):
        step(nq * i + e, bufs[e % 2], bufs[(e + 1) % 2], chains_from(e), chains_from(e)[:n], chains_from(e + 1))


def _normalised(acc_ref, k, dv):
    return acc_ref[k, :dv, :] / acc_ref[k, dv:dv + 1, :]


def _mla_flash_kernel(qt_ref, k_ref, vt_ref, o_ref, s_a, s_b, m_ref, acc_ref, *, t, heads, nq):
    def score(j, q, h):
        rows = pl.ds(pl.multiple_of(j * t, t), t)
        return jnp.dot(k_ref[0, rows, h * HEAD_PAD:(h + 1) * HEAD_PAD],
                       qt_ref[q, h * HEAD_PAD:(h + 1) * HEAD_PAD, :], preferred_element_type=F32)

    value = lambda j, h: vt_ref[j, h * MLA_VA:(h + 1) * MLA_VA, :]
    _flash_loop(pl.program_id(2), score, value, heads, nq, s_a, s_b, m_ref, acc_ref, ahead=2)
    for q in range(nq):
        for h in range(0, heads, 2):
            k = q * heads + h
            pair = jnp.concatenate([_normalised(acc_ref, k, MLA_V), _normalised(acc_ref, k + 1, MLA_V)], axis=0)
            o_ref[0, q * t:(q + 1) * t, h * MLA_V:(h + 2) * MLA_V] = pair.T.astype(BF16)


def _mla_flash(qt, km, vt, *, t, heads, nq):
    b, s, _ = km.shape
    nk = s // t
    steps = nk // nq
    return pl.pallas_call(
        functools.partial(_mla_flash_kernel, t=t, heads=heads, nq=nq),
        grid=(b, MLA_HEADS // heads, steps),
        in_specs=[pl.BlockSpec((nq, heads * HEAD_PAD, t), lambda bi, hi, i: (bi * steps + i, hi, 0)),
                  pl.BlockSpec((1, s, heads * HEAD_PAD), lambda bi, hi, i: (bi, 0, hi), pipeline_mode=_RESIDENT),
                  pl.BlockSpec((nk, heads * MLA_VA, t), lambda bi, hi, i: (bi, hi, 0), pipeline_mode=_RESIDENT)],
        out_specs=pl.BlockSpec((1, nq * t, heads * MLA_V), lambda bi, hi, i: (bi, i, hi)),
        out_shape=jax.ShapeDtypeStruct((b, s, MLA_HEADS * MLA_V), BF16),
        scratch_shapes=_flash_scratch(t, MLA_VA, nq * heads),
        compiler_params=_cparams(("parallel", "parallel", "arbitrary")),
        name="mla_flash",
    )(qt, km, vt)


def _diff_lambda(lam_ref, lam_init):
    lq1, lk1, lq2, lk2 = (lam_ref[r:r + 1, :] for r in range(4))
    return (jnp.exp(jnp.sum(lq1 * lk1, axis=-1, keepdims=True))
            - jnp.exp(jnp.sum(lq2 * lk2, axis=-1, keepdims=True)) + lam_init)


def _diff_finish(o1, o2, lam, subln, lam_init):
    o = o1 - lam * o2
    return _rms(o, subln) * (1.0 - lam_init)


def _diff_flash_kernel(lam_ref, subln_ref, qt_ref, k_ref, vt_ref, o_ref, s_a, s_b, m_ref, acc_ref,
                       *, t, heads, nq, lam_init):
    def score(j, q, n):
        rows = pl.ds(pl.multiple_of(j * t, t), t)
        return jnp.dot(k_ref[0, rows, (n // 2) * LANES:(n // 2 + 1) * LANES],
                       qt_ref[q, n * LANES:(n + 1) * LANES, :], preferred_element_type=F32)

    value = lambda j, n: vt_ref[j, (n // 2) * DIFF_VA:(n // 2 + 1) * DIFF_VA, :]
    _flash_loop(pl.program_id(2), score, value, 2 * heads, nq, s_a, s_b, m_ref, acc_ref, ahead=1)
    lam = _diff_lambda(lam_ref, lam_init)
    for q in range(nq):
        for h in range(heads):
            k = q * 2 * heads + 2 * h
            o1 = _normalised(acc_ref, k, DIFF_V).T
            o2 = _normalised(acc_ref, k + 1, DIFF_V).T
            o = _diff_finish(o1, o2, lam, subln_ref[...], lam_init)
            o_ref[0, q * t:(q + 1) * t, h * DIFF_V:(h + 1) * DIFF_V] = o.astype(BF16)


def _diff_flash(lam_rows, subln, dqt, dkb, dvt, *, t, heads, nq, lam_init):
    b, s, _ = dkb.shape
    nk = s // t
    steps = nk // nq
    full = lambda a: pl.BlockSpec(a.shape, lambda bi, hi, i: (0,) * a.ndim)
    return pl.pallas_call(
        functools.partial(_diff_flash_kernel, t=t, heads=heads, nq=nq, lam_init=lam_init),
        grid=(b, DIFF_HEADS // heads, steps),
        in_specs=[full(lam_rows), full(subln),
                  pl.BlockSpec((nq, heads * 2 * LANES, t), lambda bi, hi, i: (bi * steps + i, hi, 0)),
                  pl.BlockSpec((1, s, heads * LANES), lambda bi, hi, i: (bi, 0, hi), pipeline_mode=_RESIDENT),
                  pl.BlockSpec((nk, heads * DIFF_VA, t), lambda bi, hi, i: (bi, hi, 0), pipeline_mode=_RESIDENT)],
        out_specs=pl.BlockSpec((1, nq * t, heads * DIFF_V), lambda bi, hi, i: (bi, i, hi)),
        out_shape=jax.ShapeDtypeStruct((b, s, DIFF_HEADS * DIFF_V), BF16),
        scratch_shapes=_flash_scratch(t, DIFF_VA, nq * 2 * heads),
        compiler_params=_cparams(("parallel", "parallel", "arbitrary")),
        name="diff_flash",
    )(lam_rows, subln, dqt, dkb, dvt)


def _two_part_softmax_pv(s_past, s_new, v_past, v_new):
    m = jnp.maximum(jnp.max(s_past, axis=-1, keepdims=True), jnp.max(s_new, axis=-1, keepdims=True))
    p_past = jnp.exp2(s_past - m)
    p_new = jnp.exp2(s_new - m)
    l = jnp.sum(p_past, axis=-1, keepdims=True) + jnp.sum(p_new, axis=-1, keepdims=True)
    acc = (jnp.dot(p_past.astype(BF16), v_past, preferred_element_type=F32)
           + jnp.dot(p_new.astype(BF16), v_new, preferred_element_type=F32))
    return acc / l


def _token_major(qt_ref):
    return qt_ref[0].astype(F32).T.astype(BF16)


def _mla_step_kernel(qt_ref, ckv_ref, kpe_ref, w_ukv_ref, e_ref, kn_ref, vn_ref, o_ref):
    q_all = _token_major(qt_ref)
    kv = jnp.dot(ckv_ref[0].astype(BF16), w_ukv_ref[...], preferred_element_type=F32)
    k_past = (kv[:, :MLA_HEADS * HEAD_PAD]
              + jnp.dot(kpe_ref[0].astype(BF16), e_ref[:MLA_ROPE, :], preferred_element_type=F32)).astype(BF16)
    v_past = kv[:, MLA_HEADS * HEAD_PAD:].astype(BF16)
    for h in range(MLA_HEADS):
        ks = slice(h * HEAD_PAD, (h + 1) * HEAD_PAD)
        vs = slice(h * MLA_V, (h + 1) * MLA_V)
        q = q_all[:, ks]
        o = _two_part_softmax_pv(_nt_dot(q, k_past[:, ks]), _nt_dot(q, kn_ref[0, :, ks]),
                                 v_past[:, vs], vn_ref[0, :, vs])
        o_ref[0, :, vs] = o.astype(BF16)


def _mla_step(qt, ckv_past, kpe_past, km_new, vm_new, w):
    b, _, n = qt.shape
    blk = lambda a: pl.BlockSpec((1,) + a.shape[1:], lambda bi: (bi, 0, 0))
    full = lambda a: pl.BlockSpec(a.shape, lambda bi: (0,) * a.ndim)
    return pl.pallas_call(
        _mla_step_kernel,
        grid=(b,),
        in_specs=[blk(qt), blk(ckv_past), blk(kpe_past), full(w['w_ukv']), full(w['e_place']),
                  blk(km_new), blk(vm_new)],
        out_specs=pl.BlockSpec((1, n, MLA_HEADS * MLA_V), lambda bi: (bi, 0, 0)),
        out_shape=jax.ShapeDtypeStruct((b, n, MLA_HEADS * MLA_V), BF16),
        compiler_params=_cparams(("parallel",)),
        name="mla_step",
    )(qt, ckv_past, kpe_past, w['w_ukv'], w['e_place'], km_new, vm_new)


def _diff_step_kernel(lam_ref, subln_ref, qt_ref, kp_ref, vp_ref, kn_ref, vn_ref, o_ref, *, lam_init):
    lam = _diff_lambda(lam_ref, lam_init)
    q = _token_major(qt_ref)
    for h in range(DIFF_HEADS):
        sl = slice(h * LANES, (h + 1) * LANES)
        q1 = q[:, 2 * h * LANES:(2 * h + 1) * LANES]
        q2 = q[:, (2 * h + 1) * LANES:(2 * h + 2) * LANES]
        kp = kp_ref[0, :, h, :].astype(BF16)
        vp = vp_ref[0, :, h, :].astype(BF16)
        kn = kn_ref[0, :, sl]
        vn = vn_ref[0, :, sl]
        o1 = _two_part_softmax_pv(_nt_dot(q1, kp), _nt_dot(q1, kn), vp, vn)
        o2 = _two_part_softmax_pv(_nt_dot(q2, kp), _nt_dot(q2, kn), vp, vn)
        o_ref[0, :, sl] = _diff_finish(o1, o2, lam, subln_ref[...], lam_init).astype(BF16)


def _diff_step(lam_rows, subln, dqt, k_past, v_past, dkb, dvb, *, lam_init):
    b, _, n = dqt.shape
    full = lambda a: pl.BlockSpec(a.shape, lambda bi: (0,) * a.ndim)
    blk = lambda a: pl.BlockSpec((1,) + a.shape[1:], lambda bi: (bi,) + (0,) * (a.ndim - 1))
    args = (dqt, k_past, v_past, dkb, dvb)
    return pl.pallas_call(
        functools.partial(_diff_step_kernel, lam_init=lam_init),
        grid=(b,),
        in_specs=[full(lam_rows), full(subln)] + [blk(a) for a in args],
        out_specs=pl.BlockSpec((1, n, DIFF_HEADS * DIFF_V), lambda bi: (bi, 0, 0)),
        out_shape=jax.ShapeDtypeStruct((b, n, DIFF_HEADS * DIFF_V), BF16),
        compiler_params=_cparams(("parallel",)),
        name="diff_step",
    )(lam_rows, subln, *args)


def _first_index_of_max(vals, lane):
    m = jnp.max(vals, axis=-1, keepdims=True)
    idx = jnp.min(jnp.where(vals == m, lane, float(LANES)), axis=-1, keepdims=True)
    return m, idx


_SLAB_MEMBER = EXP_PER_GROUP
_SLAB_RANK = EXP_PER_GROUP + 1


def _route(hb, wr_ref, br_ref):
    logit = jnp.dot(hb, wr_ref[...], preferred_element_type=F32) + br_ref[...]
    lane_i = lax.broadcasted_iota(jnp.int32, logit.shape, 1)
    lane = lane_i.astype(F32)
    group_of_lane = (lane_i >> (EXP_PER_GROUP.bit_length() - 1)).astype(F32)
    neg = -jnp.inf
    g_logit = jnp.where((lane_i >= N_EXPERTS) & (lane_i < N_EXPERTS + N_GROUPS), logit, neg)
    g_max, g_lane = _first_index_of_max(g_logit, lane)
    g_w = 1.0 / jnp.sum(jnp.exp(g_logit - g_max), axis=-1, keepdims=True)
    g_idx = g_lane - float(N_EXPERTS)
    e_logit = jnp.where(group_of_lane == g_idx, logit, neg)
    m1, i1 = _first_index_of_max(e_logit, lane)
    m2, i2 = _first_index_of_max(jnp.where(lane == i1, neg, e_logit), lane)
    r = jnp.exp(m2 - m1)
    p1 = 1.0 / (1.0 + r)
    p2 = r / (1.0 + r)
    gate = g_w * (jnp.where(lane == i1, p1, 0.0) + jnp.where(lane == i2, p2, 0.0))
    return gate, g_idx, jnp.where(lane == g_idx, 1.0, 0.0)


def _post_attn_kernel(x_ref, mla_ref, diff_ref, wo_a_ref, wo_b_ref, g_ref, b_ref, wr_ref, br_ref, ltri_ref, utri_ref,
                      h_ref, hb_ref, slab_ref, rank_ref, cnt_ref, *, alpha):
    a = (jnp.dot(mla_ref[...], wo_a_ref[...], preferred_element_type=F32)
         + jnp.dot(diff_ref[...], wo_b_ref[...], preferred_element_type=F32))
    h = _layer_norm(alpha * x_ref[...] + a, g_ref[...], b_ref[...])
    h_ref[...] = h
    hb = h.astype(BF16)
    hb_ref[...] = hb
    gate, g_idx, onehot = _route(hb, wr_ref, br_ref)
    lane_i = lax.broadcasted_iota(jnp.int32, gate.shape, 1)

    rank_c = jnp.dot(ltri_ref[...], onehot.astype(BF16), preferred_element_type=F32)
    onehot_t = onehot.T[:BF16_ROWS]
    rank_r = jnp.dot(onehot_t.astype(BF16), utri_ref[...], preferred_element_type=F32)
    rank_ref[0] = jnp.where(onehot_t > 0.0, rank_r, -1.0)
    cnt_ref[0] = jnp.broadcast_to(jnp.sum(onehot, axis=0, keepdims=True), cnt_ref.shape[1:])

    for g in range(N_GROUPS):
        member = g_idx == float(g)
        rank_g = jnp.sum(jnp.where(lane_i == g, rank_c, 0.0), axis=-1, keepdims=True)
        rolled = gate if g == 0 else pltpu.roll(gate, LANES - EXP_PER_GROUP * g, 1)
        extra = jnp.where(lane_i == _SLAB_MEMBER, jnp.where(member, 1.0, 0.0),
                          jnp.where(lane_i == _SLAB_RANK, jnp.where(member, rank_g, -1.0), 0.0))
        slab_ref[:, g * LANES:(g + 1) * LANES] = jnp.where(lane_i < EXP_PER_GROUP, rolled, extra)


def _post_attn(x, mla_o, diff_o, w, *, tm, alpha):
    t = x.shape[0]
    nt = t // tm
    row = lambda n: pl.BlockSpec((tm, n), lambda i: (i, 0))
    full = lambda a: pl.BlockSpec(a.shape, lambda i: (0,) * a.ndim)
    per_tile = lambda r, c: pl.BlockSpec((1, r, c), lambda i: (i, 0, 0))
    idx = jnp.arange(tm)
    ltri = (idx[None, :] < idx[:, None]).astype(BF16)
    utri = (idx[:, None] < idx[None, :]).astype(BF16)
    weights = (w['w_out_a'], w['w_out_b'], w['ln1_g'], w['ln1_b'], w['w_route'], w['b_route'], ltri, utri)
    return pl.pallas_call(
        functools.partial(_post_attn_kernel, alpha=alpha),
        grid=(nt,),
        in_specs=[row(D_MODEL), row(512), row(512)] + [full(a) for a in weights],
        out_specs=[row(D_MODEL), row(D_MODEL), row(N_GROUPS * LANES), per_tile(BF16_ROWS, tm), per_tile(8, LANES)],
        out_shape=[jax.ShapeDtypeStruct((t, D_MODEL), F32), jax.ShapeDtypeStruct((t, D_MODEL), BF16),
                   jax.ShapeDtypeStruct((t, N_GROUPS * LANES), F32),
                   jax.ShapeDtypeStruct((nt, BF16_ROWS, tm), F32), jax.ShapeDtypeStruct((nt, 8, LANES), F32)],
        compiler_params=_cparams(("parallel",)),
        name="post_attn",
    )(x, mla_o, diff_o, *weights)


MOE_TILE = 1024
MOE_SLACK_ROWS = 32
MOE_EXTRA_ROWS = 128


def _pack_gate_terms(slab):
    lane = lax.broadcasted_iota(jnp.int32, slab.shape, 1)
    gates = jnp.where(lane < EXP_PER_GROUP, slab, 0.0)
    hi = gates.astype(BF16).astype(F32)
    r1 = gates - hi
    mid = r1.astype(BF16).astype(F32)
    lo = r1 - mid
    return (hi + pltpu.roll(mid, EXP_PER_GROUP, 1) + pltpu.roll(lo, 2 * EXP_PER_GROUP, 1)).astype(BF16)


def _unpack_gate_terms(packed):
    return (packed + pltpu.roll(packed, LANES - EXP_PER_GROUP, 1)
            + pltpu.roll(packed, LANES - 2 * EXP_PER_GROUP, 1))


def _moe_kernel(cnt_ref, hb_ref, slab_ref, rank_ref, wg_ref, wu_ref, wd_ref, yin_ref, g_ref, b_ref, y_ref,
                *, g, alpha):
    n_tok = cnt_ref[pl.program_id(0), g]
    slab = slab_ref[...]
    rank_col = slab[:, _SLAB_RANK:_SLAB_RANK + 1]
    rank_row = rank_ref[0, g:g + 1, :]
    gate_terms = _pack_gate_terms(slab)

    y_ref[...] = alpha * yin_ref[...] if g == 0 else yin_ref[...]

    def one_pass(base, rows):
        sub = lax.broadcasted_iota(jnp.int32, (rows, 1), 0).astype(F32)
        lan2 = lax.broadcasted_iota(jnp.int32, (1, 2 * rows), 1)
        row_of_lane = jnp.where(lan2 < rows, lan2, lan2 - rows).astype(F32)
        sel = jnp.where(rank_row - base == sub, 1.0, 0.0).astype(BF16)
        sel_t2 = jnp.where(rank_col - base == row_of_lane, 1.0, 0.0).astype(BF16)
        x = jnp.dot(sel, hb_ref[...], preferred_element_type=F32).astype(BF16)
        gate_c = _unpack_gate_terms(jnp.dot(sel, gate_terms, preferred_element_type=F32))
        ys = jnp.zeros((rows, D_MODEL), F32)
        for e in range(EXP_PER_GROUP):
            hid = (jax.nn.silu(jnp.dot(x, wg_ref[e].astype(BF16), preferred_element_type=F32))
                   * jnp.dot(x, wu_ref[e].astype(BF16), preferred_element_type=F32))
            ys = ys + jnp.dot((hid * gate_c[:, e:e + 1]).astype(BF16), wd_ref[e].astype(BF16),
                              preferred_element_type=F32)
        ys_hi = ys.astype(BF16)
        ys_lo = (ys - ys_hi.astype(F32)).astype(BF16)
        y_ref[...] += jnp.dot(sel_t2, jnp.concatenate([ys_hi, ys_lo], axis=0), preferred_element_type=F32)

    first_rows = slab.shape[0] // N_GROUPS + MOE_SLACK_ROWS
    one_pass(0.0, first_rows)

    def extra_pass(c, carry):
        one_pass((first_rows + c * MOE_EXTRA_ROWS).astype(F32), MOE_EXTRA_ROWS)
        return carry

    lax.fori_loop(0, (jnp.maximum(n_tok - first_rows, 0) + MOE_EXTRA_ROWS - 1) // MOE_EXTRA_ROWS, extra_pass, 0)

    if g == N_GROUPS - 1:
        y_ref[...] = _layer_norm(y_ref[...], g_ref[...], b_ref[...])


def _moe_group(g, y, hb, slab, rank_r, cnt, w, *, tm, alpha):
    t = y.shape[0]
    row = lambda n: pl.BlockSpec((tm, n), lambda i, c: (i, 0))
    full = lambda a: pl.BlockSpec(a.shape, lambda i, c: (0,) * a.ndim)
    grp = lambda a: pl.BlockSpec((EXP_PER_GROUP,) + a.shape[1:], lambda i, c: (g, 0, 0),
                                 pipeline_mode=pl.Buffered(1))
    grid_spec = pltpu.PrefetchScalarGridSpec(
        num_scalar_prefetch=1,
        grid=(t // tm,),
        in_specs=[row(D_MODEL), pl.BlockSpec((tm, LANES), lambda i, c: (i, g)),
                  pl.BlockSpec((1, BF16_ROWS, tm), lambda i, c: (i, 0, 0)),
                  grp(w['w_gate']), grp(w['w_up']), grp(w['w_down']), row(D_MODEL),
                  full(w['ln2_g']), full(w['ln2_b'])],
        out_specs=row(D_MODEL))
    return pl.pallas_call(
        functools.partial(_moe_kernel, g=g, alpha=alpha),
        grid_spec=grid_spec,
        out_shape=jax.ShapeDtypeStruct((t, D_MODEL), F32),
        compiler_params=_cparams(("arbitrary",)),
        name=f"moe_group{g}",
    )(cnt, hb, slab, rank_r, w['w_gate'], w['w_up'], w['w_down'], y, w['ln2_g'], w['ln2_b'])


def _moe(h, hb, slab, rank_r, cnt, w, *, tm, alpha):
    y = h
    for g in range(N_GROUPS):
        y = _moe_group(g, y, hb, slab, rank_r, cnt, w, tm=tm, alpha=alpha)
    return y


def _rope_tables(pos):
    pos = pos.astype(F32)[:, None]
    lane = jnp.arange(LANES)
    inv_m = ROPE_THETA ** (-jnp.arange(0, MLA_ROPE, 2, dtype=F32) / MLA_ROPE)
    ang_m = pos * inv_m
    inv_d = ROPE_THETA ** (-jnp.arange(0, DIFF_ROT, 2, dtype=F32) / DIFF_ROT)
    ang_d = pos * inv_d
    narrow = lax.optimization_barrier((jnp.cos(ang_m), jnp.sin(ang_m), jnp.cos(ang_d), jnp.sin(ang_d)))
    tabs = tuple(a[:, lane % a.shape[1]] for a in narrow)
    tabs_t = tuple(a.T for a in narrow)
    return tabs, tabs_t


def _prep_weights(l, w_in, mla_q_norm, mla_w_uq, mla_kv_norm, mla_w_ukv, diff_subln, w_out, ln1_g, ln1_b,
                  ln2_g, ln2_b, w_route_group, b_route_group, w_route_expert, b_route_expert,
                  w_exp_gate, w_exp_up, w_exp_down, lam_rows):
    wi = w_in[l]
    s0, s1, s2, s3, s4 = 256, 512, 544, 1056, 1568
    w_in_r = jnp.concatenate([wi[:, :s0], wi[:, s0:s1], wi[:, s3:s4], wi[:, s4:],
                              wi[:, s1:s2], jnp.zeros((D_MODEL, LANES - MLA_ROPE), F32)], axis=1)
    w_uq = jnp.pad(mla_w_uq[l].reshape(MLA_Q_RANK, MLA_HEADS, MLA_NOPE + MLA_ROPE),
                   ((0, 0), (0, 0), (0, HEAD_PAD - MLA_NOPE - MLA_ROPE))).reshape(MLA_Q_RANK, -1)
    ukv = mla_w_ukv[l].reshape(MLA_KV_RANK, MLA_HEADS, MLA_NOPE + MLA_V)
    w_uk = jnp.pad(ukv[..., :MLA_NOPE], ((0, 0), (0, 0), (0, HEAD_PAD - MLA_NOPE))).reshape(MLA_KV_RANK, -1)
    w_uv = ukv[..., MLA_NOPE:].reshape(MLA_KV_RANK, -1)
    rows = jnp.arange(LANES)[:, None]
    cols = jnp.arange(MLA_HEADS * HEAD_PAD)[None, :]
    e_place = ((rows < MLA_ROPE) & (cols % HEAD_PAD == rows + MLA_NOPE)).astype(BF16)
    w_route = jnp.concatenate([w_route_expert[l], w_route_group[l],
                               jnp.zeros((D_MODEL, LANES - N_EXPERTS - N_GROUPS), F32)], axis=1)
    b_route = jnp.concatenate([b_route_expert[l], b_route_group[l],
                               jnp.zeros((LANES - N_EXPERTS - N_GROUPS,), F32)])[None, :]
    return dict(
        w_in=w_in_r.astype(BF16), w_dqt=wi[:, s2:s3].T.astype(BF16),
        q_norm=mla_q_norm[l][None, :], w_uqt=w_uq.T.astype(BF16),
        kv_norm=mla_kv_norm[l][None, :], w_ukv=jnp.concatenate([w_uk, w_uv], axis=1).astype(BF16),
        w_uvt=w_uv.T.astype(BF16),
        e_place=e_place, subln=diff_subln[l][None, :], lam_rows=lam_rows,
        w_out_a=w_out[l][:MLA_HEADS * MLA_V].astype(BF16), w_out_b=w_out[l][MLA_HEADS * MLA_V:].astype(BF16),
        ln1_g=ln1_g[l][None, :], ln1_b=ln1_b[l][None, :], ln2_g=ln2_g[l][None, :], ln2_b=ln2_b[l][None, :],
        w_route=w_route.astype(BF16), b_route=b_route,
        w_gate=w_exp_gate[l], w_up=w_exp_up[l], w_down=w_exp_down[l])


def _block_output(x, mla_o, diff_o, w, alpha):
    t = x.shape[0]
    tm = min(MOE_TILE, t)
    h, hb, slab, rank_r, cnt = _post_attn(x, mla_o.reshape(t, -1), diff_o.reshape(t, -1), w, tm=tm, alpha=alpha)
    cnt = cnt[:, 0, :N_GROUPS].astype(jnp.int32)
    return _moe(h, hb, slab, rank_r, cnt, w, tm=tm, alpha=alpha)


def kernel(x_prompt, x_sample, cache_mla_ckv, cache_mla_kpe, cache_diff_k, cache_diff_v, w_in, mla_q_norm, mla_w_uq, mla_kv_norm, mla_w_ukv, diff_lambda_q1, diff_lambda_k1, diff_lambda_q2, diff_lambda_k2, diff_subln, w_out, ln1_g, ln1_b, ln2_g, ln2_b, w_route_group, b_route_group, w_route_expert, b_route_expert, w_exp_gate, w_exp_up, w_exp_down):
    depth = w_in.shape[0]
    bp, sp, _ = x_prompt.shape
    bs, ss, _ = x_sample.shape
    past = cache_mla_ckv.shape[2]
    alpha = (2.0 * depth) ** 0.25
    tabs_p, tabs_pt = _rope_tables(jnp.arange(sp))
    tabs_s, tabs_st = _rope_tables(past + jnp.arange(ss))
    hp = x_prompt.reshape(bp * sp, D_MODEL)
    hs = x_sample.reshape(bs * ss, D_MODEL)
    outs = [[] for _ in range(8)]
    for l in range(depth):
        lam_init = 0.8 - 0.6 * math.exp(-0.3 * l)
        lam_rows = jnp.stack([diff_lambda_q1[l], diff_lambda_k1[l], diff_lambda_q2[l], diff_lambda_k2[l]])
        w = _prep_weights(l, w_in, mla_q_norm, mla_w_uq, mla_kv_norm, mla_w_ukv, diff_subln, w_out,
                          ln1_g, ln1_b, ln2_g, ln2_b, w_route_group, b_route_group, w_route_expert,
                          b_route_expert, w_exp_gate, w_exp_up, w_exp_down, lam_rows)

        ckv, kpe, dk, dv, qt, km, _, vt, dqt, dkb, _, dvt = _proj(hp, tabs_p, tabs_pt, w, tm=PROJ_TILE)
        b3 = lambda a: a.reshape(bp, sp, -1)
        mla_o = _mla_flash(qt, b3(km), vt, t=ATTN_TILE, heads=4, nq=2 * ATTN_QTILES)
        diff_o = _diff_flash(w['lam_rows'], w['subln'], dqt, b3(dkb), dvt, t=ATTN_TILE, heads=2,
                             nq=2 * ATTN_QTILES, lam_init=lam_init)
        outs[0].append(ckv.reshape(bp, sp, MLA_KV_RANK))
        outs[1].append(kpe.reshape(bp, sp, MLA_ROPE))
        outs[2].append(dk.reshape(bp, sp, DIFF_HEADS, 2 * DIFF_D))
        outs[3].append(dv.reshape(bp, sp, DIFF_HEADS, DIFF_V))
        hp = _block_output(hp, mla_o, diff_o, w, alpha)

        ckv_s, kpe_s, dk_s, dv_s, qt_s, km_s, vm_s, _, dqt_s, dkb_s, dvb_s, _ = _proj(hs, tabs_s, tabs_st, w, tm=ss)
        s3 = lambda a: a.reshape(bs, ss, -1)
        mla_o_s = _mla_step(qt_s, cache_mla_ckv[l], cache_mla_kpe[l], s3(km_s), s3(vm_s), w)
        diff_o_s = _diff_step(w['lam_rows'], w['subln'], dqt_s, cache_diff_k[l], cache_diff_v[l],
                              s3(dkb_s), s3(dvb_s), lam_init=lam_init)
        outs[4].append(ckv_s.reshape(bs, ss, MLA_KV_RANK))
        outs[5].append(kpe_s.reshape(bs, ss, MLA_ROPE))
        outs[6].append(dk_s.reshape(bs, ss, DIFF_HEADS, 2 * DIFF_D))
        outs[7].append(dv_s.reshape(bs, ss, DIFF_HEADS, DIFF_V))
        hs = _block_output(hs, mla_o_s, diff_o_s, w, alpha)

    return (hp.reshape(bp, sp, D_MODEL), hs.reshape(bs, ss, D_MODEL)) + tuple(jnp.stack(o) for o in outs)
```

```python
import functools
import math

import jax
import jax.numpy as jnp
from jax import lax
from jax.experimental import pallas as pl
from jax.experimental.pallas import tpu as pltpu

F32 = jnp.float32
BF16 = jnp.bfloat16

D_MODEL = 1024
CHUNK = 64
ROPE_THETA = 500000.0
MLA_HEADS = 8
MLA_NOPE = 64
MLA_ROPE = 32
MLA_V = 64
MLA_Q_RANK = 256
MLA_KV_RANK = 256
MLA_SCALE = (MLA_NOPE + MLA_ROPE) ** -0.5
DIFF_HEADS = 4
DIFF_D = 64
DIFF_V = 2 * DIFF_D
DIFF_ROT = DIFF_D // 4
DIFF_SCALE = DIFF_D ** -0.5
N_GROUPS = 4
EXP_PER_GROUP = 8
N_EXPERTS = N_GROUPS * EXP_PER_GROUP
EPS_LN = 1e-5
EPS_RMS = 1e-6
LOG2E = math.log2(math.e)

LANES = 128
HEAD_PAD = LANES
ATTN_TILE = 256
PROJ_TILE = ATTN_TILE
ATTN_QTILES = 8
MLA_HEADS_PER_STEP = 4
DIFF_HEADS_PER_STEP = 2
BF16_ROWS = 16
MLA_VA = MLA_V + BF16_ROWS
DIFF_VA = DIFF_V + BF16_ROWS
VMEM_LIMIT = 56 * 1024 * 1024

MLA_QK_WIDTH = MLA_HEADS * HEAD_PAD
MLA_O_WIDTH = MLA_HEADS * MLA_V
DIFF_QK_WIDTH = DIFF_HEADS * 2 * DIFF_D
DIFF_O_WIDTH = DIFF_HEADS * DIFF_V
_CQ = 0
_CKV = _CQ + MLA_Q_RANK
_DK = _CKV + MLA_KV_RANK
_DV = _DK + DIFF_QK_WIDTH
_KR = _DV + DIFF_O_WIDTH


def _cparams(sem):
    return pltpu.CompilerParams(dimension_semantics=sem, vmem_limit_bytes=VMEM_LIMIT)


def _rms(x, g):
    return x * lax.rsqrt(jnp.mean(x * x, axis=-1, keepdims=True) + EPS_RMS) * g


def _layer_norm(x, g, b):
    mu = jnp.mean(x, axis=-1, keepdims=True)
    xc = x - mu
    var = jnp.mean(xc * xc, axis=-1, keepdims=True)
    return xc * lax.rsqrt(var + EPS_LN) * g + b


def _nt_dot(a, b):
    return lax.dot_general(a, b, (((1,), (1,)), ((), ())), preferred_element_type=F32)


def _rope_coeffs(cos_t, sin_t, lo, half, period):
    lane = lax.broadcasted_iota(jnp.int32, cos_t.shape, 1) & (period - 1)
    is1 = (lane >= lo) & (lane < lo + half)
    is2 = (lane >= lo + half) & (lane < lo + 2 * half)
    c = jnp.where(is1 | is2, cos_t, 1.0)
    a = jnp.where(is1, -sin_t, 0.0)
    b = jnp.where(is2, sin_t, 0.0)
    return c, a, b


def _rope_apply(x, coeffs, half):
    c, a, b = coeffs
    return x * c + pltpu.roll(x, LANES - half, 1) * a + pltpu.roll(x, half, 1) * b


def _rope_rows(x, cos_t, sin_t):
    half = cos_t.shape[0]
    x1, x2 = x[:half], x[half:]
    return x1 * cos_t - x2 * sin_t, x1 * sin_t + x2 * cos_t


def _store_tiles(ref, r0, r1, val):
    tile = ref.shape[-1]
    for c in range(val.shape[-1] // tile):
        ref[c, r0:r1, :] = val[:, c * tile:(c + 1) * tile]


def _proj_kernel(x_ref, w_in_ref, w_dqt_ref, qn_ref, w_uqt_ref, kvn_ref, w_ukv_ref, w_uvt_ref, e_ref,
                 cm_ref, sm_ref, cd_ref, sd_ref, cmt_ref, smt_ref, cdt_ref, sdt_ref,
                 ckv_ref, kpe_ref, dk_ref, dv_ref, qt_ref, km_ref, vm_ref, vt_ref,
                 dqt_ref, dkb_ref, dvb_ref, dvt_ref):
    xb = x_ref[...].astype(BF16)
    tm = xb.shape[0]
    z = jnp.dot(xb, w_in_ref[...], preferred_element_type=F32)
    half_m = MLA_ROPE // 2
    half_d = DIFF_ROT // 2
    coef_k = _rope_coeffs(cm_ref[...], sm_ref[...], 0, half_m, LANES)
    coef_d = _rope_coeffs(cd_ref[...], sd_ref[...], 0, half_d, DIFF_D)

    cq = _rms(z[:, _CQ:_CQ + MLA_Q_RANK], qn_ref[...])
    qt = _nt_dot(w_uqt_ref[...], cq.astype(BF16)) * (MLA_SCALE * LOG2E)
    cos_mt, sin_mt = cmt_ref[...], smt_ref[...]
    for h in range(MLA_HEADS):
        r0 = h * HEAD_PAD
        _store_tiles(qt_ref, r0, r0 + MLA_NOPE, qt[r0:r0 + MLA_NOPE].astype(BF16))
        o1, o2 = _rope_rows(qt[r0 + MLA_NOPE:r0 + MLA_NOPE + MLA_ROPE], cos_mt, sin_mt)
        _store_tiles(qt_ref, r0 + MLA_NOPE, r0 + MLA_NOPE + half_m, o1.astype(BF16))
        _store_tiles(qt_ref, r0 + MLA_NOPE + half_m, r0 + MLA_NOPE + MLA_ROPE, o2.astype(BF16))
        _store_tiles(qt_ref, r0 + MLA_NOPE + MLA_ROPE, r0 + HEAD_PAD,
                     jnp.zeros((HEAD_PAD - MLA_NOPE - MLA_ROPE, tm), BF16))

    ckv = _rms(z[:, _CKV:_CKV + MLA_KV_RANK], kvn_ref[...])
    ckv_ref[...] = ckv
    ckv_b = ckv.astype(BF16)
    kslab = _rope_apply(z[:, _KR:_KR + LANES], coef_k, half_m)
    kpe_ref[...] = kslab[:, :MLA_ROPE]
    kv = jnp.dot(ckv_b, w_ukv_ref[...], preferred_element_type=F32)
    k_full = kv[:, :MLA_HEADS * HEAD_PAD] + jnp.dot(kslab.astype(BF16), e_ref[...], preferred_element_type=F32)
    km_ref[...] = k_full.astype(BF16)
    vm_ref[...] = kv[:, MLA_HEADS * HEAD_PAD:].astype(BF16)
    ones_rows = jnp.where(lax.broadcasted_iota(jnp.int32, (BF16_ROWS, tm), 0) == 0, 1.0, 0.0).astype(BF16)
    vt = _nt_dot(w_uvt_ref[...], ckv_b)
    for h in range(MLA_HEADS):
        _store_tiles(vt_ref, h * MLA_VA, h * MLA_VA + MLA_V, vt[h * MLA_V:(h + 1) * MLA_V].astype(BF16))
        _store_tiles(vt_ref, h * MLA_VA + MLA_V, (h + 1) * MLA_VA, ones_rows)

    dqt = _nt_dot(w_dqt_ref[...], xb) * (DIFF_SCALE * LOG2E)
    cos_dt, sin_dt = cdt_ref[...], sdt_ref[...]
    zeros = jnp.zeros((DIFF_D, tm), BF16)
    for n in range(2 * DIFF_HEADS):
        r0 = n * DIFF_D
        o1, o2 = _rope_rows(dqt[r0:r0 + DIFF_ROT], cos_dt, sin_dt)
        qn = jnp.concatenate([o1, o2, dqt[r0 + DIFF_ROT:r0 + DIFF_D]], axis=0).astype(BF16)
        lo, hi = (qn, zeros) if n % 2 == 0 else (zeros, qn)
        _store_tiles(dqt_ref, n * LANES, n * LANES + DIFF_D, lo)
        _store_tiles(dqt_ref, n * LANES + DIFF_D, (n + 1) * LANES, hi)

    dv = z[:, _DV:_DV + DIFF_HEADS * DIFF_V]
    dvb_ref[...] = dv.astype(BF16)
    dvt = dv.T
    for h in range(DIFF_HEADS):
        sl = slice(h * LANES, (h + 1) * LANES)
        dk = _rope_apply(z[:, _DK + h * LANES:_DK + (h + 1) * LANES], coef_d, half_d)
        dk_ref[:, h, :] = dk
        dkb_ref[:, sl] = dk.astype(BF16)
        dv_ref[:, h, :] = dv[:, sl]
        _store_tiles(dvt_ref, h * DIFF_VA, h * DIFF_VA + DIFF_V, dvt[sl].astype(BF16))
        _store_tiles(dvt_ref, h * DIFF_VA + DIFF_V, (h + 1) * DIFF_VA, ones_rows)


def _proj(x, tabs, tabs_t, w, *, tm):
    t = x.shape[0]
    n_pos_blocks = tabs[0].shape[0] // tm
    row = lambda n: pl.BlockSpec((tm, n), lambda i: (i, 0))
    full = lambda a: pl.BlockSpec(a.shape, lambda i: (0,) * a.ndim)
    tab = pl.BlockSpec((tm, LANES), lambda i: (i % n_pos_blocks, 0))
    tab_t = lambda a: pl.BlockSpec((a.shape[0], tm), lambda i: (0, i % n_pos_blocks))
    tile = min(tm, ATTN_TILE)
    tr = lambda n: pl.BlockSpec((tm // tile, n, tile), lambda i: (i, 0, 0))
    hd = lambda n: pl.BlockSpec((tm, DIFF_HEADS, n), lambda i: (i, 0, 0))
    weights = (w['w_in'], w['w_dqt'], w['q_norm'], w['w_uqt'], w['kv_norm'], w['w_ukv'], w['w_uvt'], w['e_place'])
    outs = ((row, MLA_KV_RANK, F32), (row, MLA_ROPE, F32), (hd, 2 * DIFF_D, F32), (hd, DIFF_V, F32),
            (tr, MLA_QK_WIDTH, BF16), (row, MLA_QK_WIDTH, BF16), (row, MLA_O_WIDTH, BF16),
            (tr, MLA_HEADS * MLA_VA, BF16),
            (tr, 2 * DIFF_QK_WIDTH, BF16), (row, DIFF_QK_WIDTH, BF16), (row, DIFF_O_WIDTH, BF16),
            (tr, DIFF_HEADS * DIFF_VA, BF16))
    shape = lambda kind, n: {tr: (t // tile, n, tile), hd: (t, DIFF_HEADS, n), row: (t, n)}[kind]
    return pl.pallas_call(
        _proj_kernel,
        grid=(t // tm,),
        in_specs=[row(D_MODEL)] + [full(a) for a in weights] + [tab] * 4 + [tab_t(a) for a in tabs_t],
        out_specs=[kind(n) for kind, n, _ in outs],
        out_shape=[jax.ShapeDtypeStruct(shape(kind, n), d) for kind, n, d in outs],
        compiler_params=_cparams(("parallel",)),
        name="proj",
    )(x, *weights, *tabs, *tabs_t)


def _chunk_mask_t(t):
    shift = CHUNK.bit_length() - 1
    kc = lax.broadcasted_iota(jnp.int32, (t, t), 0) >> shift
    qc = lax.broadcasted_iota(jnp.int32, (t, t), 1) >> shift
    return kc <= qc


_RESIDENT = pl.Buffered(1)


def _flash_scratch(t, dv, n):
    return [pltpu.VMEM((n, t, t), F32), pltpu.VMEM((n, t, t), F32),
            pltpu.VMEM((n, 1, t), F32), pltpu.VMEM((n, dv, t), F32)]


def _flash_loop(i, score, value, n, nq, s_a, s_b, m_ref, acc_ref, ahead):
    t = s_a.shape[-1]
    mask = _chunk_mask_t(t)
    m_ref[...] = jnp.full(m_ref.shape, -jnp.inf, F32)
    acc_ref[...] = jnp.zeros(acc_ref.shape, F32)
    chains_from = lambda q: tuple(range(q * n, nq * n))
    everything = chains_from(0)

    def step(j, s_cur, s_nxt, cur, masked, nxt):
        pending = list(nxt)

        def issue_scores(count):
            for k in pending[:count]:
                s_nxt[k] = score(j + 1, k // n, k % n)
            del pending[:count]

        issue_scores(ahead)
        for k in cur:
            issue_scores(1)
            s = s_cur[k]
            if k in masked:
                s = jnp.where(mask, s, -jnp.inf)
            m = m_ref[k]
            m_new = jnp.maximum(m, jnp.max(s, axis=0, keepdims=True))
            alpha = jnp.exp2(m - m_new)
            p = jnp.exp2(s - m_new)
            m_ref[k] = m_new
            acc_ref[k] = alpha * acc_ref[k] + jnp.dot(value(j, k % n), p.astype(BF16), preferred_element_type=F32)
        issue_scores(len(pending))

    for k in everything:
        s_a[k] = score(0, k // n, k % n)

    def pair(jj, carry):
        step(2 * jj, s_a, s_b, everything, (), everything)
        step(2 * jj + 1, s_b, s_a, everything, (), everything)
        return carry

    lax.fori_loop(0, (nq // 2) * i, pair, 0)
    bufs = (s_a, s_b)
    for e in range(nq):
        step(nq * i + e, bufs[e % 2], bufs[(e + 1) % 2], chains_from(e), chains_from(e)[:n], chains_from(e + 1))


def _normalised(acc_ref, k, dv):
    return acc_ref[k, :dv, :] / acc_ref[k, dv:dv + 1, :]


def _mla_flash_kernel(qt_ref, k_ref, vt_ref, o_ref, s_a, s_b, m_ref, acc_ref, *, t, heads, nq):
    def score(j, q, h):
        rows = pl.ds(pl.multiple_of(j * t, t), t)
        return jnp.dot(k_ref[0, rows, h * HEAD_PAD:(h + 1) * HEAD_PAD],
                       qt_ref[q, h * HEAD_PAD:(h + 1) * HEAD_PAD, :], preferred_element_type=F32)

    value = lambda j, h: vt_ref[j, h * MLA_VA:(h + 1) * MLA_VA, :]
    _flash_loop(pl.program_id(2), score, value, heads, nq, s_a, s_b, m_ref, acc_ref, ahead=2)
    for q in range(nq):
        for h in range(0, heads, 2):
            k = q * heads + h
            pair = jnp.concatenate([_normalised(acc_ref, k, MLA_V), _normalised(acc_ref, k + 1, MLA_V)], axis=0)
            o_ref[0, q * t:(q + 1) * t, h * MLA_V:(h + 2) * MLA_V] = pair.T.astype(BF16)


def _mla_flash(qt, km, vt, *, t, heads, nq):
    b, s, _ = km.shape
    nk = s // t
    steps = nk // nq
    return pl.pallas_call(
        functools.partial(_mla_flash_kernel, t=t, heads=heads, nq=nq),
        grid=(b, MLA_HEADS // heads, steps),
        in_specs=[pl.BlockSpec((nq, heads * HEAD_PAD, t), lambda bi, hi, i: (bi * steps + i, hi, 0)),
                  pl.BlockSpec((1, s, heads * HEAD_PAD), lambda bi, hi, i: (bi, 0, hi), pipeline_mode=_RESIDENT),
                  pl.BlockSpec((nk, heads * MLA_VA, t), lambda bi, hi, i: (bi, hi, 0), pipeline_mode=_RESIDENT)],
        out_specs=pl.BlockSpec((1, nq * t, heads * MLA_V), lambda bi, hi, i: (bi, i, hi)),
        out_shape=jax.ShapeDtypeStruct((b, s, MLA_HEADS * MLA_V), BF16),
        scratch_shapes=_flash_scratch(t, MLA_VA, nq * heads),
        compiler_params=_cparams(("parallel", "parallel", "arbitrary")),
        name="mla_flash",
    )(qt, km, vt)


def _diff_lambda(lam_ref, lam_init):
    lq1, lk1, lq2, lk2 = (lam_ref[r:r + 1, :] for r in range(4))
    return (jnp.exp(jnp.sum(lq1 * lk1, axis=-1, keepdims=True))
            - jnp.exp(jnp.sum(lq2 * lk2, axis=-1, keepdims=True)) + lam_init)


def _diff_finish(o1, o2, lam, subln, lam_init):
    o = o1 - lam * o2
    return _rms(o, subln) * (1.0 - lam_init)


def _diff_flash_kernel(lam_ref, subln_ref, qt_ref, k_ref, vt_ref, o_ref, s_a, s_b, m_ref, acc_ref,
                       *, t, heads, nq, lam_init):
    def score(j, q, n):
        rows = pl.ds(pl.multiple_of(j * t, t), t)
        return jnp.dot(k_ref[0, rows, (n // 2) * LANES:(n // 2 + 1) * LANES],
                       qt_ref[q, n * LANES:(n + 1) * LANES, :], preferred_element_type=F32)

    value = lambda j, n: vt_ref[j, (n // 2) * DIFF_VA:(n // 2 + 1) * DIFF_VA, :]
    _flash_loop(pl.program_id(2), score, value, 2 * heads, nq, s_a, s_b, m_ref, acc_ref, ahead=1)
    lam = _diff_lambda(lam_ref, lam_init)
    for q in range(nq):
        for h in range(heads):
            k = q * 2 * heads + 2 * h
            o1 = _normalised(acc_ref, k, DIFF_V).T
            o2 = _normalised(acc_ref, k + 1, DIFF_V).T
            o = _diff_finish(o1, o2, lam, subln_ref[...], lam_init)
            o_ref[0, q * t:(q + 1) * t, h * DIFF_V:(h + 1) * DIFF_V] = o.astype(BF16)


def _diff_flash(lam_rows, subln, dqt, dkb, dvt, *, t, heads, nq, lam_init):
    b, s, _ = dkb.shape
    nk = s // t
    steps = nk // nq
    full = lambda a: pl.BlockSpec(a.shape, lambda bi, hi, i: (0,) * a.ndim)
    return pl.pallas_call(
        functools.partial(_diff_flash_kernel, t=t, heads=heads, nq=nq, lam_init=lam_init),
        grid=(b, DIFF_HEADS // heads, steps),
        in_specs=[full(lam_rows), full(subln),
                  pl.BlockSpec((nq, heads * 2 * LANES, t), lambda bi, hi, i: (bi * steps + i, hi, 0)),
                  pl.BlockSpec((1, s, heads * LANES), lambda bi, hi, i: (bi, 0, hi), pipeline_mode=_RESIDENT),
                  pl.BlockSpec((nk, heads * DIFF_VA, t), lambda bi, hi, i: (bi, hi, 0), pipeline_mode=_RESIDENT)],
        out_specs=pl.BlockSpec((1, nq * t, heads * DIFF_V), lambda bi, hi, i: (bi, i, hi)),
        out_shape=jax.ShapeDtypeStruct((b, s, DIFF_HEADS * DIFF_V), BF16),
        scratch_shapes=_flash_scratch(t, DIFF_VA, nq * 2 * heads),
        compiler_params=_cparams(("parallel", "parallel", "arbitrary")),
        name="diff_flash",
    )(lam_rows, subln, dqt, dkb, dvt)


def _two_part_softmax_pv(s_past, s_new, v_past, v_new):
    m = jnp.maximum(jnp.max(s_past, axis=-1, keepdims=True), jnp.max(s_new, axis=-1, keepdims=True))
    p_past = jnp.exp2(s_past - m)
    p_new = jnp.exp2(s_new - m)
    l = jnp.sum(p_past, axis=-1, keepdims=True) + jnp.sum(p_new, axis=-1, keepdims=True)
    acc = (jnp.dot(p_past.astype(BF16), v_past, preferred_element_type=F32)
           + jnp.dot(p_new.astype(BF16), v_new, preferred_element_type=F32))
    return acc / l


def _token_major(qt_ref):
    return qt_ref[0].astype(F32).T.astype(BF16)


def _mla_step_kernel(qt_ref, ckv_ref, kpe_ref, w_ukv_ref, e_ref, kn_ref, vn_ref, o_ref):
    q_all = _token_major(qt_ref)
    kv = jnp.dot(ckv_ref[0].astype(BF16), w_ukv_ref[...], preferred_element_type=F32)
    k_past = (kv[:, :MLA_HEADS * HEAD_PAD]
              + jnp.dot(kpe_ref[0].astype(BF16), e_ref[:MLA_ROPE, :], preferred_element_type=F32)).astype(BF16)
    v_past = kv[:, MLA_HEADS * HEAD_PAD:].astype(BF16)
    for h in range(MLA_HEADS):
        ks = slice(h * HEAD_PAD, (h + 1) * HEAD_PAD)
        vs = slice(h * MLA_V, (h + 1) * MLA_V)
        q = q_all[:, ks]
        o = _two_part_softmax_pv(_nt_dot(q, k_past[:, ks]), _nt_dot(q, kn_ref[0, :, ks]),
                                 v_past[:, vs], vn_ref[0, :, vs])
        o_ref[0, :, vs] = o.astype(BF16)


def _mla_step(qt, ckv_past, kpe_past, km_new, vm_new, w):
    b, _, n = qt.shape
    blk = lambda a: pl.BlockSpec((1,) + a.shape[1:], lambda bi: (bi, 0, 0))
    full = lambda a: pl.BlockSpec(a.shape, lambda bi: (0,) * a.ndim)
    return pl.pallas_call(
        _mla_step_kernel,
        grid=(b,),
        in_specs=[blk(qt), blk(ckv_past), blk(kpe_past), full(w['w_ukv']), full(w['e_place']),
                  blk(km_new), blk(vm_new)],
        out_specs=pl.BlockSpec((1, n, MLA_HEADS * MLA_V), lambda bi: (bi, 0, 0)),
        out_shape=jax.ShapeDtypeStruct((b, n, MLA_HEADS * MLA_V), BF16),
        compiler_params=_cparams(("parallel",)),
        name="mla_step",
    )(qt, ckv_past, kpe_past, w['w_ukv'], w['e_place'], km_new, vm_new)


def _diff_step_kernel(lam_ref, subln_ref, qt_ref, kp_ref, vp_ref, kn_ref, vn_ref, o_ref, *, lam_init):
    lam = _diff_lambda(lam_ref, lam_init)
    q = _token_major(qt_ref)
    for h in range(DIFF_HEADS):
        sl = slice(h * LANES, (h + 1) * LANES)
        q1 = q[:, 2 * h * LANES:(2 * h + 1) * LANES]
        q2 = q[:, (2 * h + 1) * LANES:(2 * h + 2) * LANES]
        kp = kp_ref[0, :, h, :].astype(BF16)
        vp = vp_ref[0, :, h, :].astype(BF16)
        kn = kn_ref[0, :, sl]
        vn = vn_ref[0, :, sl]
        o1 = _two_part_softmax_pv(_nt_dot(q1, kp), _nt_dot(q1, kn), vp, vn)
        o2 = _two_part_softmax_pv(_nt_dot(q2, kp), _nt_dot(q2, kn), vp, vn)
        o_ref[0, :, sl] = _diff_finish(o1, o2, lam, subln_ref[...], lam_init).astype(BF16)


def _diff_step(lam_rows, subln, dqt, k_past, v_past, dkb, dvb, *, lam_init):
    b, _, n = dqt.shape
    full = lambda a: pl.BlockSpec(a.shape, lambda bi: (0,) * a.ndim)
    blk = lambda a: pl.BlockSpec((1,) + a.shape[1:], lambda bi: (bi,) + (0,) * (a.ndim - 1))
    args = (dqt, k_past, v_past, dkb, dvb)
    return pl.pallas_call(
        functools.partial(_diff_step_kernel, lam_init=lam_init),
        grid=(b,),
        in_specs=[full(lam_rows), full(subln)] + [blk(a) for a in args],
        out_specs=pl.BlockSpec((1, n, DIFF_HEADS * DIFF_V), lambda bi: (bi, 0, 0)),
        out_shape=jax.ShapeDtypeStruct((b, n, DIFF_HEADS * DIFF_V), BF16),
        compiler_params=_cparams(("parallel",)),
        name="diff_step",
    )(lam_rows, subln, *args)


def _first_index_of_max(vals, lane):
    m = jnp.max(vals, axis=-1, keepdims=True)
    idx = jnp.min(jnp.where(vals == m, lane, float(LANES)), axis=-1, keepdims=True)
    return m, idx


_SLAB_MEMBER = EXP_PER_GROUP
_SLAB_RANK = EXP_PER_GROUP + 1


def _route(hb, wr_ref, br_ref):
    logit = jnp.dot(hb, wr_ref[...], preferred_element_type=F32) + br_ref[...]
    lane_i = lax.broadcasted_iota(jnp.int32, logit.shape, 1)
    lane = lane_i.astype(F32)
    group_of_lane = (lane_i >> (EXP_PER_GROUP.bit_length() - 1)).astype(F32)
    neg = -jnp.inf
    g_logit = jnp.where((lane_i >= N_EXPERTS) & (lane_i < N_EXPERTS + N_GROUPS), logit, neg)
    g_max, g_lane = _first_index_of_max(g_logit, lane)
    g_w = 1.0 / jnp.sum(jnp.exp(g_logit - g_max), axis=-1, keepdims=True)
    g_idx = g_lane - float(N_EXPERTS)
    e_logit = jnp.where(group_of_lane == g_idx, logit, neg)
    m1, i1 = _first_index_of_max(e_logit, lane)
    m2, i2 = _first_index_of_max(jnp.where(lane == i1, neg, e_logit), lane)
    r = jnp.exp(m2 - m1)
    p1 = 1.0 / (1.0 + r)
    p2 = r / (1.0 + r)
    gate = g_w * (jnp.where(lane == i1, p1, 0.0) + jnp.where(lane == i2, p2, 0.0))
    return gate, g_idx, jnp.where(lane == g_idx, 1.0, 0.0)


def _post_attn_kernel(x_ref, mla_ref, diff_ref, wo_a_ref, wo_b_ref, g_ref, b_ref, wr_ref, br_ref, ltri_ref, utri_ref,
                      h_ref, hb_ref, slab_ref, rank_ref, cnt_ref, *, alpha):
    a = (jnp.dot(mla_ref[...], wo_a_ref[...], preferred_element_type=F32)
         + jnp.dot(diff_ref[...], wo_b_ref[...], preferred_element_type=F32))
    h = _layer_norm(alpha * x_ref[...] + a, g_ref[...], b_ref[...])
    h_ref[...] = h
    hb = h.astype(BF16)
    hb_ref[...] = hb
    gate, g_idx, onehot = _route(hb, wr_ref, br_ref)
    lane_i = lax.broadcasted_iota(jnp.int32, gate.shape, 1)

    rank_c = jnp.dot(ltri_ref[...], onehot.astype(BF16), preferred_element_type=F32)
    onehot_t = onehot.T[:BF16_ROWS]
    rank_r = jnp.dot(onehot_t.astype(BF16), utri_ref[...], preferred_element_type=F32)
    rank_ref[0] = jnp.where(onehot_t > 0.0, rank_r, -1.0)
    cnt_ref[0] = jnp.broadcast_to(jnp.sum(onehot, axis=0, keepdims=True), cnt_ref.shape[1:])

    for g in range(N_GROUPS):
        member = g_idx == float(g)
        rank_g = jnp.sum(jnp.where(lane_i == g, rank_c, 0.0), axis=-1, keepdims=True)
        rolled = gate if g == 0 else pltpu.roll(gate, LANES - EXP_PER_GROUP * g, 1)
        extra = jnp.where(lane_i == _SLAB_MEMBER, jnp.where(member, 1.0, 0.0),
                          jnp.where(lane_i == _SLAB_RANK, jnp.where(member, rank_g, -1.0), 0.0))
        slab_ref[:, g * LANES:(g + 1) * LANES] = jnp.where(lane_i < EXP_PER_GROUP, rolled, extra)


def _post_attn(x, mla_o, diff_o, w, *, tm, alpha):
    t = x.shape[0]
    nt = t // tm
    row = lambda n: pl.BlockSpec((tm, n), lambda i: (i, 0))
    full = lambda a: pl.BlockSpec(a.shape, lambda i: (0,) * a.ndim)
    per_tile = lambda r, c: pl.BlockSpec((1, r, c), lambda i: (i, 0, 0))
    idx = jnp.arange(tm)
    ltri = (idx[None, :] < idx[:, None]).astype(BF16)
    utri = (idx[:, None] < idx[None, :]).astype(BF16)
    weights = (w['w_out_a'], w['w_out_b'], w['ln1_g'], w['ln1_b'], w['w_route'], w['b_route'], ltri, utri)
    return pl.pallas_call(
        functools.partial(_post_attn_kernel, alpha=alpha),
        grid=(nt,),
        in_specs=[row(D_MODEL), row(MLA_O_WIDTH), row(DIFF_O_WIDTH)] + [full(a) for a in weights],
        out_specs=[row(D_MODEL), row(D_MODEL), row(N_GROUPS * LANES), per_tile(BF16_ROWS, tm), per_tile(8, LANES)],
        out_shape=[jax.ShapeDtypeStruct((t, D_MODEL), F32), jax.ShapeDtypeStruct((t, D_MODEL), BF16),
                   jax.ShapeDtypeStruct((t, N_GROUPS * LANES), F32),
                   jax.ShapeDtypeStruct((nt, BF16_ROWS, tm), F32), jax.ShapeDtypeStruct((nt, 8, LANES), F32)],
        compiler_params=_cparams(("parallel",)),
        name="post_attn",
    )(x, mla_o, diff_o, *weights)


MOE_TILE = 1024
MOE_SLACK_ROWS = 32
MOE_EXTRA_ROWS = 128


def _pack_gate_terms(slab):
    lane = lax.broadcasted_iota(jnp.int32, slab.shape, 1)
    gates = jnp.where(lane < EXP_PER_GROUP, slab, 0.0)
    hi = gates.astype(BF16).astype(F32)
    r1 = gates - hi
    mid = r1.astype(BF16).astype(F32)
    lo = r1 - mid
    return (hi + pltpu.roll(mid, EXP_PER_GROUP, 1) + pltpu.roll(lo, 2 * EXP_PER_GROUP, 1)).astype(BF16)


def _unpack_gate_terms(packed):
    return (packed + pltpu.roll(packed, LANES - EXP_PER_GROUP, 1)
            + pltpu.roll(packed, LANES - 2 * EXP_PER_GROUP, 1))


def _moe_kernel(cnt_ref, hb_ref, slab_ref, rank_ref, wg_ref, wu_ref, wd_ref, yin_ref, g_ref, b_ref, y_ref,
                *, g, alpha):
    n_tok = cnt_ref[pl.program_id(0), g]
    slab = slab_ref[...]
    rank_col = slab[:, _SLAB_RANK:_SLAB_RANK + 1]
    rank_row = rank_ref[0, g:g + 1, :]
    gate_terms = _pack_gate_terms(slab)

    y_ref[...] = alpha * yin_ref[...] if g == 0 else yin_ref[...]

    def one_pass(base, rows):
        sub = lax.broadcasted_iota(jnp.int32, (rows, 1), 0).astype(F32)
        lan2 = lax.broadcasted_iota(jnp.int32, (1, 2 * rows), 1)
        row_of_lane = jnp.where(lan2 < rows, lan2, lan2 - rows).astype(F32)
        sel = jnp.where(rank_row - base == sub, 1.0, 0.0).astype(BF16)
        sel_t2 = jnp.where(rank_col - base == row_of_lane, 1.0, 0.0).astype(BF16)
        x = jnp.dot(sel, hb_ref[...], preferred_element_type=F32).astype(BF16)
        gate_c = _unpack_gate_terms(jnp.dot(sel, gate_terms, preferred_element_type=F32))
        ys = jnp.zeros((rows, D_MODEL), F32)
        for e in range(EXP_PER_GROUP):
            hid = (jax.nn.silu(jnp.dot(x, wg_ref[e].astype(BF16), preferred_element_type=F32))
                   * jnp.dot(x, wu_ref[e].astype(BF16), preferred_element_type=F32))
            ys = ys + jnp.dot((hid * gate_c[:, e:e + 1]).astype(BF16), wd_ref[e].astype(BF16),
                              preferred_element_type=F32)
        ys_hi = ys.astype(BF16)
        ys_lo = (ys - ys_hi.astype(F32)).astype(BF16)
        y_ref[...] += jnp.dot(sel_t2, jnp.concatenate([ys_hi, ys_lo], axis=0), preferred_element_type=F32)

    first_rows = slab.shape[0] // N_GROUPS + MOE_SLACK_ROWS
    one_pass(0.0, first_rows)

    def extra_pass(c, carry):
        one_pass((first_rows + c * MOE_EXTRA_ROWS).astype(F32), MOE_EXTRA_ROWS)
        return carry

    lax.fori_loop(0, (jnp.maximum(n_tok - first_rows, 0) + MOE_EXTRA_ROWS - 1) // MOE_EXTRA_ROWS, extra_pass, 0)

    if g == N_GROUPS - 1:
        y_ref[...] = _layer_norm(y_ref[...], g_ref[...], b_ref[...])


def _moe_group(g, y, hb, slab, rank_r, cnt, w, *, tm, alpha):
    t = y.shape[0]
    row = lambda n: pl.BlockSpec((tm, n), lambda i, c: (i, 0))
    full = lambda a: pl.BlockSpec(a.shape, lambda i, c: (0,) * a.ndim)
    grp = lambda a: pl.BlockSpec((EXP_PER_GROUP,) + a.shape[1:], lambda i, c: (g, 0, 0),
                                 pipeline_mode=pl.Buffered(1))
    grid_spec = pltpu.PrefetchScalarGridSpec(
        num_scalar_prefetch=1,
        grid=(t // tm,),
        in_specs=[row(D_MODEL), pl.BlockSpec((tm, LANES), lambda i, c: (i, g)),
                  pl.BlockSpec((1, BF16_ROWS, tm), lambda i, c: (i, 0, 0)),
                  grp(w['w_gate']), grp(w['w_up']), grp(w['w_down']), row(D_MODEL),
                  full(w['ln2_g']), full(w['ln2_b'])],
        out_specs=row(D_MODEL))
    return pl.pallas_call(
        functools.partial(_moe_kernel, g=g, alpha=alpha),
        grid_spec=grid_spec,
        out_shape=jax.ShapeDtypeStruct((t, D_MODEL), F32),
        compiler_params=_cparams(("arbitrary",)),
        name=f"moe_group{g}",
    )(cnt, hb, slab, rank_r, w['w_gate'], w['w_up'], w['w_down'], y, w['ln2_g'], w['ln2_b'])


def _moe(h, hb, slab, rank_r, cnt, w, *, tm, alpha):
    y = h
    for g in range(N_GROUPS):
        y = _moe_group(g, y, hb, slab, rank_r, cnt, w, tm=tm, alpha=alpha)
    return y


def _rope_tables(pos):
    pos = pos.astype(F32)[:, None]
    lane = jnp.arange(LANES)
    inv_m = ROPE_THETA ** (-jnp.arange(0, MLA_ROPE, 2, dtype=F32) / MLA_ROPE)
    ang_m = pos * inv_m
    inv_d = ROPE_THETA ** (-jnp.arange(0, DIFF_ROT, 2, dtype=F32) / DIFF_ROT)
    ang_d = pos * inv_d
    narrow = lax.optimization_barrier((jnp.cos(ang_m), jnp.sin(ang_m), jnp.cos(ang_d), jnp.sin(ang_d)))
    tabs = tuple(a[:, lane % a.shape[1]] for a in narrow)
    tabs_t = tuple(a.T for a in narrow)
    return tabs, tabs_t


def _prep_weights(l, w_in, mla_q_norm, mla_w_uq, mla_kv_norm, mla_w_ukv, diff_subln, w_out, ln1_g, ln1_b,
                  ln2_g, ln2_b, w_route_group, b_route_group, w_route_expert, b_route_expert,
                  w_exp_gate, w_exp_up, w_exp_down, lam_rows):
    wi = w_in[l]
    s1 = MLA_Q_RANK + MLA_KV_RANK
    s2 = s1 + MLA_ROPE
    s3 = s2 + DIFF_QK_WIDTH
    s4 = s3 + DIFF_QK_WIDTH
    w_in_r = jnp.concatenate([wi[:, :s1], wi[:, s3:s4], wi[:, s4:],
                              wi[:, s1:s2], jnp.zeros((D_MODEL, LANES - MLA_ROPE), F32)], axis=1)
    w_uq = jnp.pad(mla_w_uq[l].reshape(MLA_Q_RANK, MLA_HEADS, MLA_NOPE + MLA_ROPE),
                   ((0, 0), (0, 0), (0, HEAD_PAD - MLA_NOPE - MLA_ROPE))).reshape(MLA_Q_RANK, -1)
    ukv = mla_w_ukv[l].reshape(MLA_KV_RANK, MLA_HEADS, MLA_NOPE + MLA_V)
    w_uk = jnp.pad(ukv[..., :MLA_NOPE], ((0, 0), (0, 0), (0, HEAD_PAD - MLA_NOPE))).reshape(MLA_KV_RANK, -1)
    w_uv = ukv[..., MLA_NOPE:].reshape(MLA_KV_RANK, -1)
    rows = jnp.arange(LANES)[:, None]
    cols = jnp.arange(MLA_HEADS * HEAD_PAD)[None, :]
    e_place = ((rows < MLA_ROPE) & (cols % HEAD_PAD == rows + MLA_NOPE)).astype(BF16)
    w_route = jnp.concatenate([w_route_expert[l], w_route_group[l],
                               jnp.zeros((D_MODEL, LANES - N_EXPERTS - N_GROUPS), F32)], axis=1)
    b_route = jnp.concatenate([b_route_expert[l], b_route_group[l],
                               jnp.zeros((LANES - N_EXPERTS - N_GROUPS,), F32)])[None, :]
    return dict(
        w_in=w_in_r.astype(BF16), w_dqt=wi[:, s2:s3].T.astype(BF16),
        q_norm=mla_q_norm[l][None, :], w_uqt=w_uq.T.astype(BF16),
        kv_norm=mla_kv_norm[l][None, :], w_ukv=jnp.concatenate([w_uk, w_uv], axis=1).astype(BF16),
        w_uvt=w_uv.T.astype(BF16),
        e_place=e_place, subln=diff_subln[l][None, :], lam_rows=lam_rows,
        w_out_a=w_out[l][:MLA_HEADS * MLA_V].astype(BF16), w_out_b=w_out[l][MLA_HEADS * MLA_V:].astype(BF16),
        ln1_g=ln1_g[l][None, :], ln1_b=ln1_b[l][None, :], ln2_g=ln2_g[l][None, :], ln2_b=ln2_b[l][None, :],
        w_route=w_route.astype(BF16), b_route=b_route,
        w_gate=w_exp_gate[l], w_up=w_exp_up[l], w_down=w_exp_down[l])


def _block_output(x, mla_o, diff_o, w, alpha):
    t = x.shape[0]
    tm = min(MOE_TILE, t)
    h, hb, slab, rank_r, cnt = _post_attn(x, mla_o.reshape(t, -1), diff_o.reshape(t, -1), w, tm=tm, alpha=alpha)
    cnt = cnt[:, 0, :N_GROUPS].astype(jnp.int32)
    return _moe(h, hb, slab, rank_r, cnt, w, tm=tm, alpha=alpha)


def kernel(x_prompt, x_sample, cache_mla_ckv, cache_mla_kpe, cache_diff_k, cache_diff_v, w_in, mla_q_norm, mla_w_uq, mla_kv_norm, mla_w_ukv, diff_lambda_q1, diff_lambda_k1, diff_lambda_q2, diff_lambda_k2, diff_subln, w_out, ln1_g, ln1_b, ln2_g, ln2_b, w_route_group, b_route_group, w_route_expert, b_route_expert, w_exp_gate, w_exp_up, w_exp_down):
    depth = w_in.shape[0]
    bp, sp, _ = x_prompt.shape
    bs, ss, _ = x_sample.shape
    past = cache_mla_ckv.shape[2]
    alpha = (2.0 * depth) ** 0.25
    tabs_p, tabs_pt = _rope_tables(jnp.arange(sp))
    tabs_s, tabs_st = _rope_tables(past + jnp.arange(ss))
    hp = x_prompt.reshape(bp * sp, D_MODEL)
    hs = x_sample.reshape(bs * ss, D_MODEL)
    outs = [[] for _ in range(8)]
    for l in range(depth):
        lam_init = 0.8 - 0.6 * math.exp(-0.3 * l)
        lam_rows = jnp.stack([diff_lambda_q1[l], diff_lambda_k1[l], diff_lambda_q2[l], diff_lambda_k2[l]])
        w = _prep_weights(l, w_in, mla_q_norm, mla_w_uq, mla_kv_norm, mla_w_ukv, diff_subln, w_out,
                          ln1_g, ln1_b, ln2_g, ln2_b, w_route_group, b_route_group, w_route_expert,
                          b_route_expert, w_exp_gate, w_exp_up, w_exp_down, lam_rows)

        ckv, kpe, dk, dv, qt, km, _, vt, dqt, dkb, _, dvt = _proj(hp, tabs_p, tabs_pt, w, tm=PROJ_TILE)
        b3 = lambda a: a.reshape(bp, sp, -1)
        mla_o = _mla_flash(qt, b3(km), vt, t=ATTN_TILE, heads=MLA_HEADS_PER_STEP, nq=ATTN_QTILES)
        diff_o = _diff_flash(w['lam_rows'], w['subln'], dqt, b3(dkb), dvt, t=ATTN_TILE, heads=DIFF_HEADS_PER_STEP,
                             nq=ATTN_QTILES, lam_init=lam_init)
        outs[0].append(ckv.reshape(bp, sp, MLA_KV_RANK))
        outs[1].append(kpe.reshape(bp, sp, MLA_ROPE))
        outs[2].append(dk.reshape(bp, sp, DIFF_HEADS, 2 * DIFF_D))
        outs[3].append(dv.reshape(bp, sp, DIFF_HEADS, DIFF_V))
        hp = _block_output(hp, mla_o, diff_o, w, alpha)

        ckv_s, kpe_s, dk_s, dv_s, qt_s, km_s, vm_s, _, dqt_s, dkb_s, dvb_s, _ = _proj(hs, tabs_s, tabs_st, w, tm=ss)
        s3 = lambda a: a.reshape(bs, ss, -1)
        mla_o_s = _mla_step(qt_s, cache_mla_ckv[l], cache_mla_kpe[l], s3(km_s), s3(vm_s), w)
        diff_o_s = _diff_step(w['lam_rows'], w['subln'], dqt_s, cache_diff_k[l], cache_diff_v[l],
                              s3(dkb_s), s3(dvb_s), lam_init=lam_init)
        outs[4].append(ckv_s.reshape(bs, ss, MLA_KV_RANK))
        outs[5].append(kpe_s.reshape(bs, ss, MLA_ROPE))
        outs[6].append(dk_s.reshape(bs, ss, DIFF_HEADS, 2 * DIFF_D))
        outs[7].append(dv_s.reshape(bs, ss, DIFF_HEADS, DIFF_V))
        hs = _block_output(hs, mla_o_s, diff_o_s, w, alpha)

    return (hp.reshape(bp, sp, D_MODEL), hs.reshape(bs, ss, D_MODEL)) + tuple(jnp.stack(o) for o in outs)
```

```python
import functools
import math

import jax
import jax.numpy as jnp
from jax import lax
from jax.experimental import pallas as pl
from jax.experimental.pallas import tpu as pltpu

F32 = jnp.float32
BF16 = jnp.bfloat16

D_MODEL = 1024
CHUNK = 64
ROPE_THETA = 500000.0
MLA_HEADS = 8
MLA_NOPE = 64
MLA_ROPE = 32
MLA_V = 64
MLA_Q_RANK = 256
MLA_KV_RANK = 256
MLA_SCALE = (MLA_NOPE + MLA_ROPE) ** -0.5
DIFF_HEADS = 4
DIFF_D = 64
DIFF_V = 2 * DIFF_D
DIFF_ROT = DIFF_D // 4
DIFF_SCALE = DIFF_D ** -0.5
N_GROUPS = 4
EXP_PER_GROUP = 8
N_EXPERTS = N_GROUPS * EXP_PER_GROUP
EPS_LN = 1e-5
EPS_RMS = 1e-6
LOG2E = math.log2(math.e)

LANES = 128
HEAD_PAD = LANES
ATTN_TILE = 256
PROJ_TILE = ATTN_TILE
ATTN_QTILES = 8
MLA_HEADS_PER_STEP = 4
DIFF_HEADS_PER_STEP = 2
BF16_ROWS = 16
MLA_VA = MLA_V + BF16_ROWS
DIFF_VA = DIFF_V + BF16_ROWS
VMEM_LIMIT = 56 * 1024 * 1024

MLA_QK_WIDTH = MLA_HEADS * HEAD_PAD
MLA_O_WIDTH = MLA_HEADS * MLA_V
DIFF_QK_WIDTH = DIFF_HEADS * 2 * DIFF_D
DIFF_O_WIDTH = DIFF_HEADS * DIFF_V
_CQ = 0
_CKV = _CQ + MLA_Q_RANK
_DK = _CKV + MLA_KV_RANK
_DV = _DK + DIFF_QK_WIDTH
_KR = _DV + DIFF_O_WIDTH


def _cparams(sem):
    return pltpu.CompilerParams(dimension_semantics=sem, vmem_limit_bytes=VMEM_LIMIT)


def _rms(x, g):
    return x * lax.rsqrt(jnp.mean(x * x, axis=-1, keepdims=True) + EPS_RMS) * g


def _layer_norm(x, g, b):
    mu = jnp.mean(x, axis=-1, keepdims=True)
    xc = x - mu
    var = jnp.mean(xc * xc, axis=-1, keepdims=True)
    return xc * lax.rsqrt(var + EPS_LN) * g + b


def _nt_dot(a, b):
    return lax.dot_general(a, b, (((1,), (1,)), ((), ())), preferred_element_type=F32)


def _rope_coeffs(cos_t, sin_t, lo, half, period):
    lane = lax.broadcasted_iota(jnp.int32, cos_t.shape, 1) & (period - 1)
    is1 = (lane >= lo) & (lane < lo + half)
    is2 = (lane >= lo + half) & (lane < lo + 2 * half)
    c = jnp.where(is1 | is2, cos_t, 1.0)
    a = jnp.where(is1, -sin_t, 0.0)
    b = jnp.where(is2, sin_t, 0.0)
    return c, a, b


def _rope_apply(x, coeffs, half):
    c, a, b = coeffs
    return x * c + pltpu.roll(x, LANES - half, 1) * a + pltpu.roll(x, half, 1) * b


def _rope_rows(x, cos_t, sin_t):
    half = cos_t.shape[0]
    x1, x2 = x[:half], x[half:]
    return x1 * cos_t - x2 * sin_t, x1 * sin_t + x2 * cos_t


def _store_tiles(ref, r0, r1, val):
    tile = ref.shape[-1]
    for c in range(val.shape[-1] // tile):
        ref[c, r0:r1, :] = val[:, c * tile:(c + 1) * tile]


def _proj_kernel(x_ref, w_in_ref, w_dqt_ref, qn_ref, w_uqt_ref, kvn_ref, w_ukv_ref, w_uvt_ref,
                 cm_ref, sm_ref, cd_ref, sd_ref, cmt_ref, smt_ref, cdt_ref, sdt_ref,
                 ckv_ref, kpe_ref, dk_ref, dv_ref, qt_ref, km_ref, vm_ref, vt_ref,
                 dqt_ref, dkb_ref, dvb_ref, dvt_ref):
    xb = x_ref[...].astype(BF16)
    tm = xb.shape[0]
    z = jnp.dot(xb, w_in_ref[...], preferred_element_type=F32)
    half_m = MLA_ROPE // 2
    half_d = DIFF_ROT // 2
    coef_k = _rope_coeffs(cm_ref[...], sm_ref[...], MLA_NOPE, half_m, LANES)
    coef_d = _rope_coeffs(cd_ref[...], sd_ref[...], 0, half_d, DIFF_D)

    cq = _rms(z[:, _CQ:_CQ + MLA_Q_RANK], qn_ref[...])
    qt = _nt_dot(w_uqt_ref[...], cq.astype(BF16)) * (MLA_SCALE * LOG2E)
    cos_mt, sin_mt = cmt_ref[...], smt_ref[...]
    for h in range(MLA_HEADS):
        r0 = h * HEAD_PAD
        _store_tiles(qt_ref, r0, r0 + MLA_NOPE, qt[r0:r0 + MLA_NOPE].astype(BF16))
        o1, o2 = _rope_rows(qt[r0 + MLA_NOPE:r0 + MLA_NOPE + MLA_ROPE], cos_mt, sin_mt)
        _store_tiles(qt_ref, r0 + MLA_NOPE, r0 + MLA_NOPE + half_m, o1.astype(BF16))
        _store_tiles(qt_ref, r0 + MLA_NOPE + half_m, r0 + MLA_NOPE + MLA_ROPE, o2.astype(BF16))
        _store_tiles(qt_ref, r0 + MLA_NOPE + MLA_ROPE, r0 + HEAD_PAD,
                     jnp.zeros((HEAD_PAD - MLA_NOPE - MLA_ROPE, tm), BF16))

    ckv = _rms(z[:, _CKV:_CKV + MLA_KV_RANK], kvn_ref[...])
    ckv_ref[...] = ckv
    ckv_b = ckv.astype(BF16)
    kslab = _rope_apply(z[:, _KR:_KR + LANES], coef_k, half_m)
    kpe_ref[...] = kslab[:, MLA_NOPE:MLA_NOPE + MLA_ROPE]
    kv = jnp.dot(ckv_b, w_ukv_ref[...], preferred_element_type=F32)
    for h in range(MLA_HEADS):
        sl = slice(h * HEAD_PAD, (h + 1) * HEAD_PAD)
        km_ref[:, sl] = (kv[:, sl] + kslab).astype(BF16)
    vm_ref[...] = kv[:, MLA_HEADS * HEAD_PAD:].astype(BF16)
    ones_rows = jnp.where(lax.broadcasted_iota(jnp.int32, (BF16_ROWS, tm), 0) == 0, 1.0, 0.0).astype(BF16)
    vt = _nt_dot(w_uvt_ref[...], ckv_b)
    for h in range(MLA_HEADS):
        _store_tiles(vt_ref, h * MLA_VA, h * MLA_VA + MLA_V, vt[h * MLA_V:(h + 1) * MLA_V].astype(BF16))
        _store_tiles(vt_ref, h * MLA_VA + MLA_V, (h + 1) * MLA_VA, ones_rows)

    dqt = _nt_dot(w_dqt_ref[...], xb) * (DIFF_SCALE * LOG2E)
    cos_dt, sin_dt = cdt_ref[...], sdt_ref[...]
    zeros = jnp.zeros((DIFF_D, tm), BF16)
    for n in range(2 * DIFF_HEADS):
        r0 = n * DIFF_D
        o1, o2 = _rope_rows(dqt[r0:r0 + DIFF_ROT], cos_dt, sin_dt)
        qn = jnp.concatenate([o1, o2, dqt[r0 + DIFF_ROT:r0 + DIFF_D]], axis=0).astype(BF16)
        lo, hi = (qn, zeros) if n % 2 == 0 else (zeros, qn)
        _store_tiles(dqt_ref, n * LANES, n * LANES + DIFF_D, lo)
        _store_tiles(dqt_ref, n * LANES + DIFF_D, (n + 1) * LANES, hi)

    dv = z[:, _DV:_DV + DIFF_HEADS * DIFF_V]
    dvb_ref[...] = dv.astype(BF16)
    dvt = dv.T
    for h in range(DIFF_HEADS):
        sl = slice(h * LANES, (h + 1) * LANES)
        dk = _rope_apply(z[:, _DK + h * LANES:_DK + (h + 1) * LANES], coef_d, half_d)
        dk_ref[:, h, :] = dk
        dkb_ref[:, sl] = dk.astype(BF16)
        dv_ref[:, h, :] = dv[:, sl]
        _store_tiles(dvt_ref, h * DIFF_VA, h * DIFF_VA + DIFF_V, dvt[sl].astype(BF16))
        _store_tiles(dvt_ref, h * DIFF_VA + DIFF_V, (h + 1) * DIFF_VA, ones_rows)


def _proj(x, tabs, tabs_t, w, *, tm):
    t = x.shape[0]
    n_pos_blocks = tabs[0].shape[0] // tm
    row = lambda n: pl.BlockSpec((tm, n), lambda i: (i, 0))
    full = lambda a: pl.BlockSpec(a.shape, lambda i: (0,) * a.ndim)
    tab = pl.BlockSpec((tm, LANES), lambda i: (i % n_pos_blocks, 0))
    tab_t = lambda a: pl.BlockSpec((a.shape[0], tm), lambda i: (0, i % n_pos_blocks))
    tile = min(tm, ATTN_TILE)
    tr = lambda n: pl.BlockSpec((tm // tile, n, tile), lambda i: (i, 0, 0))
    hd = lambda n: pl.BlockSpec((tm, DIFF_HEADS, n), lambda i: (i, 0, 0))
    weights = (w['w_in'], w['w_dqt'], w['q_norm'], w['w_uqt'], w['kv_norm'], w['w_ukv'], w['w_uvt'])
    outs = ((row, MLA_KV_RANK, F32), (row, MLA_ROPE, F32), (hd, 2 * DIFF_D, F32), (hd, DIFF_V, F32),
            (tr, MLA_QK_WIDTH, BF16), (row, MLA_QK_WIDTH, BF16), (row, MLA_O_WIDTH, BF16),
            (tr, MLA_HEADS * MLA_VA, BF16),
            (tr, 2 * DIFF_QK_WIDTH, BF16), (row, DIFF_QK_WIDTH, BF16), (row, DIFF_O_WIDTH, BF16),
            (tr, DIFF_HEADS * DIFF_VA, BF16))
    shape = lambda kind, n: {tr: (t // tile, n, tile), hd: (t, DIFF_HEADS, n), row: (t, n)}[kind]
    return pl.pallas_call(
        _proj_kernel,
        grid=(t // tm,),
        in_specs=[row(D_MODEL)] + [full(a) for a in weights] + [tab] * 4 + [tab_t(a) for a in tabs_t],
        out_specs=[kind(n) for kind, n, _ in outs],
        out_shape=[jax.ShapeDtypeStruct(shape(kind, n), d) for kind, n, d in outs],
        compiler_params=_cparams(("parallel",)),
        name="proj",
    )(x, *weights, *tabs, *tabs_t)


def _chunk_mask_t(t):
    shift = CHUNK.bit_length() - 1
    kc = lax.broadcasted_iota(jnp.int32, (t, t), 0) >> shift
    qc = lax.broadcasted_iota(jnp.int32, (t, t), 1) >> shift
    return kc <= qc


_RESIDENT = pl.Buffered(1)


def _flash_scratch(t, dv, n):
    return [pltpu.VMEM((n, t, t), F32), pltpu.VMEM((n, t, t), F32),
            pltpu.VMEM((n, 1, t), F32), pltpu.VMEM((n, dv, t), F32)]


def _flash_loop(i, score, value, n, nq, s_a, s_b, m_ref, acc_ref, ahead):
    t = s_a.shape[-1]
    mask = _chunk_mask_t(t)
    m_ref[...] = jnp.full(m_ref.shape, -jnp.inf, F32)
    acc_ref[...] = jnp.zeros(acc_ref.shape, F32)
    chains_from = lambda q: tuple(range(q * n, nq * n))
    everything = chains_from(0)

    def step(j, s_cur, s_nxt, cur, masked, nxt):
        pending = list(nxt)

        def issue_scores(count):
            for k in pending[:count]:
                s_nxt[k] = score(j + 1, k // n, k % n)
            del pending[:count]

        issue_scores(ahead)
        for k in cur:
            issue_scores(1)
            s = s_cur[k]
            if k in masked:
                s = jnp.where(mask, s, -jnp.inf)
            m = m_ref[k]
            m_new = jnp.maximum(m, jnp.max(s, axis=0, keepdims=True))
            alpha = jnp.exp2(m - m_new)
            p = jnp.exp2(s - m_new)
            m_ref[k] = m_new
            acc_ref[k] = alpha * acc_ref[k] + jnp.dot(value(j, k % n), p.astype(BF16), preferred_element_type=F32)
        issue_scores(len(pending))

    for k in everything:
        s_a[k] = score(0, k // n, k % n)

    def pair(jj, carry):
        step(2 * jj, s_a, s_b, everything, (), everything)
        step(2 * jj + 1, s_b, s_a, everything, (), everything)
        return carry

    lax.fori_loop(0, (nq // 2) * i, pair, 0)
    bufs = (s_a, s_b)
    for e in range(nq):
        step(nq * i + e, bufs[e % 2], bufs[(e + 1) % 2], chains_from(e), chains_from(e)[:n], chains_from(e + 1))


def _normalised(acc_ref, k, dv):
    return acc_ref[k, :dv, :] / acc_ref[k, dv:dv + 1, :]


def _mla_flash_kernel(qt_ref, k_ref, vt_ref, o_ref, s_a, s_b, m_ref, acc_ref, *, t, heads, nq):
    def score(j, q, h):
        rows = pl.ds(pl.multiple_of(j * t, t), t)
        return jnp.dot(k_ref[0, rows, h * HEAD_PAD:(h + 1) * HEAD_PAD],
                       qt_ref[q, h * HEAD_PAD:(h + 1) * HEAD_PAD, :], preferred_element_type=F32)

    value = lambda j, h: vt_ref[j, h * MLA_VA:(h + 1) * MLA_VA, :]
    _flash_loop(pl.program_id(2), score, value, heads, nq, s_a, s_b, m_ref, acc_ref, ahead=2)
    for q in range(nq):
        for h in range(0, heads, 2):
            k = q * heads + h
            pair = jnp.concatenate([_normalised(acc_ref, k, MLA_V), _normalised(acc_ref, k + 1, MLA_V)], axis=0)
            o_ref[0, q * t:(q + 1) * t, h * MLA_V:(h + 2) * MLA_V] = pair.T.astype(BF16)


def _mla_flash(qt, km, vt, *, t, heads, nq):
    b, s, _ = km.shape
    nk = s // t
    steps = nk // nq
    return pl.pallas_call(
        functools.partial(_mla_flash_kernel, t=t, heads=heads, nq=nq),
        grid=(b, MLA_HEADS // heads, steps),
        in_specs=[pl.BlockSpec((nq, heads * HEAD_PAD, t), lambda bi, hi, i: (bi * steps + i, hi, 0)),
                  pl.BlockSpec((1, s, heads * HEAD_PAD), lambda bi, hi, i: (bi, 0, hi), pipeline_mode=_RESIDENT),
                  pl.BlockSpec((nk, heads * MLA_VA, t), lambda bi, hi, i: (bi, hi, 0), pipeline_mode=_RESIDENT)],
        out_specs=pl.BlockSpec((1, nq * t, heads * MLA_V), lambda bi, hi, i: (bi, i, hi)),
        out_shape=jax.ShapeDtypeStruct((b, s, MLA_HEADS * MLA_V), BF16),
        scratch_shapes=_flash_scratch(t, MLA_VA, nq * heads),
        compiler_params=_cparams(("parallel", "parallel", "arbitrary")),
        name="mla_flash",
    )(qt, km, vt)


def _diff_lambda(lam_ref, lam_init):
    lq1, lk1, lq2, lk2 = (lam_ref[r:r + 1, :] for r in range(4))
    return (jnp.exp(jnp.sum(lq1 * lk1, axis=-1, keepdims=True))
            - jnp.exp(jnp.sum(lq2 * lk2, axis=-1, keepdims=True)) + lam_init)


def _diff_finish(o1, o2, lam, subln, lam_init):
    o = o1 - lam * o2
    return _rms(o, subln) * (1.0 - lam_init)


def _diff_flash_kernel(lam_ref, subln_ref, qt_ref, k_ref, vt_ref, o_ref, s_a, s_b, m_ref, acc_ref,
                       *, t, heads, nq, lam_init):
    def score(j, q, n):
        rows = pl.ds(pl.multiple_of(j * t, t), t)
        return jnp.dot(k_ref[0, rows, (n // 2) * LANES:(n // 2 + 1) * LANES],
                       qt_ref[q, n * LANES:(n + 1) * LANES, :], preferred_element_type=F32)

    value = lambda j, n: vt_ref[j, (n // 2) * DIFF_VA:(n // 2 + 1) * DIFF_VA, :]
    _flash_loop(pl.program_id(2), score, value, 2 * heads, nq, s_a, s_b, m_ref, acc_ref, ahead=1)
    lam = _diff_lambda(lam_ref, lam_init)
    for q in range(nq):
        for h in range(heads):
            k = q * 2 * heads + 2 * h
            o1 = _normalised(acc_ref, k, DIFF_V).T
            o2 = _normalised(acc_ref, k + 1, DIFF_V).T
            o = _diff_finish(o1, o2, lam, subln_ref[...], lam_init)
            o_ref[0, q * t:(q + 1) * t, h * DIFF_V:(h + 1) * DIFF_V] = o.astype(BF16)


def _diff_flash(lam_rows, subln, dqt, dkb, dvt, *, t, heads, nq, lam_init):
    b, s, _ = dkb.shape
    nk = s // t
    steps = nk // nq
    full = lambda a: pl.BlockSpec(a.shape, lambda bi, hi, i: (0,) * a.ndim)
    return pl.pallas_call(
        functools.partial(_diff_flash_kernel, t=t, heads=heads, nq=nq, lam_init=lam_init),
        grid=(b, DIFF_HEADS // heads, steps),
        in_specs=[full(lam_rows), full(subln),
                  pl.BlockSpec((nq, heads * 2 * LANES, t), lambda bi, hi, i: (bi * steps + i, hi, 0)),
                  pl.BlockSpec((1, s, heads * LANES), lambda bi, hi, i: (bi, 0, hi), pipeline_mode=_RESIDENT),
                  pl.BlockSpec((nk, heads * DIFF_VA, t), lambda bi, hi, i: (bi, hi, 0), pipeline_mode=_RESIDENT)],
        out_specs=pl.BlockSpec((1, nq * t, heads * DIFF_V), lambda bi, hi, i: (bi, i, hi)),
        out_shape=jax.ShapeDtypeStruct((b, s, DIFF_HEADS * DIFF_V), BF16),
        scratch_shapes=_flash_scratch(t, DIFF_VA, nq * 2 * heads),
        compiler_params=_cparams(("parallel", "parallel", "arbitrary")),
        name="diff_flash",
    )(lam_rows, subln, dqt, dkb, dvt)


def _two_part_softmax_pv(s_past, s_new, v_past, v_new):
    m = jnp.maximum(jnp.max(s_past, axis=-1, keepdims=True), jnp.max(s_new, axis=-1, keepdims=True))
    p_past = jnp.exp2(s_past - m)
    p_new = jnp.exp2(s_new - m)
    l = jnp.sum(p_past, axis=-1, keepdims=True) + jnp.sum(p_new, axis=-1, keepdims=True)
    acc = (jnp.dot(p_past.astype(BF16), v_past, preferred_element_type=F32)
           + jnp.dot(p_new.astype(BF16), v_new, preferred_element_type=F32))
    return acc / l


def _token_major(qt_ref):
    return qt_ref[0].astype(F32).T.astype(BF16)


def _mla_step_kernel(qt_ref, ckv_ref, kpe_ref, w_ukv_ref, e_ref, kn_ref, vn_ref, o_ref):
    q_all = _token_major(qt_ref)
    kv = jnp.dot(ckv_ref[0].astype(BF16), w_ukv_ref[...], preferred_element_type=F32)
    k_past = (kv[:, :MLA_HEADS * HEAD_PAD]
              + jnp.dot(kpe_ref[0].astype(BF16), e_ref[:MLA_ROPE, :], preferred_element_type=F32)).astype(BF16)
    v_past = kv[:, MLA_HEADS * HEAD_PAD:].astype(BF16)
    for h in range(MLA_HEADS):
        ks = slice(h * HEAD_PAD, (h + 1) * HEAD_PAD)
        vs = slice(h * MLA_V, (h + 1) * MLA_V)
        q = q_all[:, ks]
        o = _two_part_softmax_pv(_nt_dot(q, k_past[:, ks]), _nt_dot(q, kn_ref[0, :, ks]),
                                 v_past[:, vs], vn_ref[0, :, vs])
        o_ref[0, :, vs] = o.astype(BF16)


def _mla_step(qt, ckv_past, kpe_past, km_new, vm_new, w):
    b, _, n = qt.shape
    blk = lambda a: pl.BlockSpec((1,) + a.shape[1:], lambda bi: (bi, 0, 0))
    full = lambda a: pl.BlockSpec(a.shape, lambda bi: (0,) * a.ndim)
    return pl.pallas_call(
        _mla_step_kernel,
        grid=(b,),
        in_specs=[blk(qt), blk(ckv_past), blk(kpe_past), full(w['w_ukv']), full(w['e_place']),
                  blk(km_new), blk(vm_new)],
        out_specs=pl.BlockSpec((1, n, MLA_HEADS * MLA_V), lambda bi: (bi, 0, 0)),
        out_shape=jax.ShapeDtypeStruct((b, n, MLA_HEADS * MLA_V), BF16),
        compiler_params=_cparams(("parallel",)),
        name="mla_step",
    )(qt, ckv_past, kpe_past, w['w_ukv'], w['e_place'], km_new, vm_new)


def _diff_step_kernel(lam_ref, subln_ref, qt_ref, kp_ref, vp_ref, kn_ref, vn_ref, o_ref, *, lam_init):
    lam = _diff_lambda(lam_ref, lam_init)
    q = _token_major(qt_ref)
    for h in range(DIFF_HEADS):
        sl = slice(h * LANES, (h + 1) * LANES)
        q1 = q[:, 2 * h * LANES:(2 * h + 1) * LANES]
        q2 = q[:, (2 * h + 1) * LANES:(2 * h + 2) * LANES]
        kp = kp_ref[0, :, h, :].astype(BF16)
        vp = vp_ref[0, :, h, :].astype(BF16)
        kn = kn_ref[0, :, sl]
        vn = vn_ref[0, :, sl]
        o1 = _two_part_softmax_pv(_nt_dot(q1, kp), _nt_dot(q1, kn), vp, vn)
        o2 = _two_part_softmax_pv(_nt_dot(q2, kp), _nt_dot(q2, kn), vp, vn)
        o_ref[0, :, sl] = _diff_finish(o1, o2, lam, subln_ref[...], lam_init).astype(BF16)


def _diff_step(lam_rows, subln, dqt, k_past, v_past, dkb, dvb, *, lam_init):
    b, _, n = dqt.shape
    full = lambda a: pl.BlockSpec(a.shape, lambda bi: (0,) * a.ndim)
    blk = lambda a: pl.BlockSpec((1,) + a.shape[1:], lambda bi: (bi,) + (0,) * (a.ndim - 1))
    args = (dqt, k_past, v_past, dkb, dvb)
    return pl.pallas_call(
        functools.partial(_diff_step_kernel, lam_init=lam_init),
        grid=(b,),
        in_specs=[full(lam_rows), full(subln)] + [blk(a) for a in args],
        out_specs=pl.BlockSpec((1, n, DIFF_HEADS * DIFF_V), lambda bi: (bi, 0, 0)),
        out_shape=jax.ShapeDtypeStruct((b, n, DIFF_HEADS * DIFF_V), BF16),
        compiler_params=_cparams(("parallel",)),
        name="diff_step",
    )(lam_rows, subln, *args)


def _first_index_of_max(vals, lane):
    m = jnp.max(vals, axis=-1, keepdims=True)
    idx = jnp.min(jnp.where(vals == m, lane, float(LANES)), axis=-1, keepdims=True)
    return m, idx


_SLAB_MEMBER = EXP_PER_GROUP
_SLAB_RANK = EXP_PER_GROUP + 1


def _route(hb, wr_ref, br_ref):
    logit = jnp.dot(hb, wr_ref[...], preferred_element_type=F32) + br_ref[...]
    lane_i = lax.broadcasted_iota(jnp.int32, logit.shape, 1)
    lane = lane_i.astype(F32)
    group_of_lane = (lane_i >> (EXP_PER_GROUP.bit_length() - 1)).astype(F32)
    neg = -jnp.inf
    g_logit = jnp.where((lane_i >= N_EXPERTS) & (lane_i < N_EXPERTS + N_GROUPS), logit, neg)
    g_max, g_lane = _first_index_of_max(g_logit, lane)
    g_w = 1.0 / jnp.sum(jnp.exp(g_logit - g_max), axis=-1, keepdims=True)
    g_idx = g_lane - float(N_EXPERTS)
    e_logit = jnp.where(group_of_lane == g_idx, logit, neg)
    m1, i1 = _first_index_of_max(e_logit, lane)
    m2, i2 = _first_index_of_max(jnp.where(lane == i1, neg, e_logit), lane)
    r = jnp.exp(m2 - m1)
    p1 = 1.0 / (1.0 + r)
    p2 = r / (1.0 + r)
    gate = g_w * (jnp.where(lane == i1, p1, 0.0) + jnp.where(lane == i2, p2, 0.0))
    return gate, g_idx, jnp.where(lane == g_idx, 1.0, 0.0)


def _post_attn_kernel(x_ref, mla_ref, diff_ref, wo_a_ref, wo_b_ref, g_ref, b_ref, wr_ref, br_ref, ltri_ref, utri_ref,
                      h_ref, hb_ref, slab_ref, rank_ref, cnt_ref, *, alpha):
    a = (jnp.dot(mla_ref[...], wo_a_ref[...], preferred_element_type=F32)
         + jnp.dot(diff_ref[...], wo_b_ref[...], preferred_element_type=F32))
    h = _layer_norm(alpha * x_ref[...] + a, g_ref[...], b_ref[...])
    h_ref[...] = h
    hb = h.astype(BF16)
    hb_ref[...] = hb
    gate, g_idx, onehot = _route(hb, wr_ref, br_ref)
    lane_i = lax.broadcasted_iota(jnp.int32, gate.shape, 1)

    rank_c = jnp.dot(ltri_ref[...], onehot.astype(BF16), preferred_element_type=F32)
    onehot_t = onehot.T[:BF16_ROWS]
    rank_r = jnp.dot(onehot_t.astype(BF16), utri_ref[...], preferred_element_type=F32)
    rank_ref[0] = jnp.where(onehot_t > 0.0, rank_r, -1.0)
    cnt_ref[0] = jnp.broadcast_to(jnp.sum(onehot, axis=0, keepdims=True), cnt_ref.shape[1:])

    for g in range(N_GROUPS):
        member = g_idx == float(g)
        rank_g = jnp.sum(jnp.where(lane_i == g, rank_c, 0.0), axis=-1, keepdims=True)
        rolled = gate if g == 0 else pltpu.roll(gate, LANES - EXP_PER_GROUP * g, 1)
        extra = jnp.where(lane_i == _SLAB_MEMBER, jnp.where(member, 1.0, 0.0),
                          jnp.where(lane_i == _SLAB_RANK, jnp.where(member, rank_g, -1.0), 0.0))
        slab_ref[:, g * LANES:(g + 1) * LANES] = jnp.where(lane_i < EXP_PER_GROUP, rolled, extra)


def _post_attn(x, mla_o, diff_o, w, *, tm, alpha):
    t = x.shape[0]
    nt = t // tm
    row = lambda n: pl.BlockSpec((tm, n), lambda i: (i, 0))
    full = lambda a: pl.BlockSpec(a.shape, lambda i: (0,) * a.ndim)
    per_tile = lambda r, c: pl.BlockSpec((1, r, c), lambda i: (i, 0, 0))
    idx = jnp.arange(tm)
    ltri = (idx[None, :] < idx[:, None]).astype(BF16)
    utri = (idx[:, None] < idx[None, :]).astype(BF16)
    weights = (w['w_out_a'], w['w_out_b'], w['ln1_g'], w['ln1_b'], w['w_route'], w['b_route'], ltri, utri)
    return pl.pallas_call(
        functools.partial(_post_attn_kernel, alpha=alpha),
        grid=(nt,),
        in_specs=[row(D_MODEL), row(MLA_O_WIDTH), row(DIFF_O_WIDTH)] + [full(a) for a in weights],
        out_specs=[row(D_MODEL), row(D_MODEL), row(N_GROUPS * LANES), per_tile(BF16_ROWS, tm), per_tile(8, LANES)],
        out_shape=[jax.ShapeDtypeStruct((t, D_MODEL), F32), jax.ShapeDtypeStruct((t, D_MODEL), BF16),
                   jax.ShapeDtypeStruct((t, N_GROUPS * LANES), F32),
                   jax.ShapeDtypeStruct((nt, BF16_ROWS, tm), F32), jax.ShapeDtypeStruct((nt, 8, LANES), F32)],
        compiler_params=_cparams(("parallel",)),
        name="post_attn",
    )(x, mla_o, diff_o, *weights)


MOE_TILE = 1024
MOE_SLACK_ROWS = 32
MOE_EXTRA_ROWS = 128


def _pack_gate_terms(slab):
    lane = lax.broadcasted_iota(jnp.int32, slab.shape, 1)
    gates = jnp.where(lane < EXP_PER_GROUP, slab, 0.0)
    hi = gates.astype(BF16).astype(F32)
    r1 = gates - hi
    mid = r1.astype(BF16).astype(F32)
    lo = r1 - mid
    return (hi + pltpu.roll(mid, EXP_PER_GROUP, 1) + pltpu.roll(lo, 2 * EXP_PER_GROUP, 1)).astype(BF16)


def _unpack_gate_terms(packed):
    return (packed + pltpu.roll(packed, LANES - EXP_PER_GROUP, 1)
            + pltpu.roll(packed, LANES - 2 * EXP_PER_GROUP, 1))


def _moe_kernel(cnt_ref, hb_ref, slab_ref, rank_ref, wg_ref, wu_ref, wd_ref, yin_ref, g_ref, b_ref, y_ref,
                *, g, alpha):
    n_tok = cnt_ref[pl.program_id(0), g]
    slab = slab_ref[...]
    rank_col = slab[:, _SLAB_RANK:_SLAB_RANK + 1]
    rank_row = rank_ref[0, g:g + 1, :]
    gate_terms = _pack_gate_terms(slab)

    y_ref[...] = alpha * yin_ref[...] if g == 0 else yin_ref[...]

    def one_pass(base, rows):
        sub = lax.broadcasted_iota(jnp.int32, (rows, 1), 0).astype(F32)
        lan2 = lax.broadcasted_iota(jnp.int32, (1, 2 * rows), 1)
        row_of_lane = jnp.where(lan2 < rows, lan2, lan2 - rows).astype(F32)
        sel = jnp.where(rank_row - base == sub, 1.0, 0.0).astype(BF16)
        sel_t2 = jnp.where(rank_col - base == row_of_lane, 1.0, 0.0).astype(BF16)
        x = jnp.dot(sel, hb_ref[...], preferred_element_type=F32).astype(BF16)
        gate_c = _unpack_gate_terms(jnp.dot(sel, gate_terms, preferred_element_type=F32))
        ys = jnp.zeros((rows, D_MODEL), F32)
        for e in range(EXP_PER_GROUP):
            hid = (jax.nn.silu(jnp.dot(x, wg_ref[e].astype(BF16), preferred_element_type=F32))
                   * jnp.dot(x, wu_ref[e].astype(BF16), preferred_element_type=F32))
            ys = ys + jnp.dot((hid * gate_c[:, e:e + 1]).astype(BF16), wd_ref[e].astype(BF16),
                              preferred_element_type=F32)
        ys_hi = ys.astype(BF16)
        ys_lo = (ys - ys_hi.astype(F32)).astype(BF16)
        y_ref[...] += jnp.dot(sel_t2, jnp.concatenate([ys_hi, ys_lo], axis=0), preferred_element_type=F32)

    first_rows = slab.shape[0] // N_GROUPS + MOE_SLACK_ROWS
    one_pass(0.0, first_rows)

    def extra_pass(c, carry):
        one_pass((first_rows + c * MOE_EXTRA_ROWS).astype(F32), MOE_EXTRA_ROWS)
        return carry

    lax.fori_loop(0, (jnp.maximum(n_tok - first_rows, 0) + MOE_EXTRA_ROWS - 1) // MOE_EXTRA_ROWS, extra_pass, 0)

    if g == N_GROUPS - 1:
        y_ref[...] = _layer_norm(y_ref[...], g_ref[...], b_ref[...])


def _moe_group(g, y, hb, slab, rank_r, cnt, w, *, tm, alpha):
    t = y.shape[0]
    row = lambda n: pl.BlockSpec((tm, n), lambda i, c: (i, 0))
    full = lambda a: pl.BlockSpec(a.shape, lambda i, c: (0,) * a.ndim)
    grp = lambda a: pl.BlockSpec((EXP_PER_GROUP,) + a.shape[1:], lambda i, c: (g, 0, 0),
                                 pipeline_mode=pl.Buffered(1))
    grid_spec = pltpu.PrefetchScalarGridSpec(
        num_scalar_prefetch=1,
        grid=(t // tm,),
        in_specs=[row(D_MODEL), pl.BlockSpec((tm, LANES), lambda i, c: (i, g)),
                  pl.BlockSpec((1, BF16_ROWS, tm), lambda i, c: (i, 0, 0)),
                  grp(w['w_gate']), grp(w['w_up']), grp(w['w_down']), row(D_MODEL),
                  full(w['ln2_g']), full(w['ln2_b'])],
        out_specs=row(D_MODEL))
    return pl.pallas_call(
        functools.partial(_moe_kernel, g=g, alpha=alpha),
        grid_spec=grid_spec,
        out_shape=jax.ShapeDtypeStruct((t, D_MODEL), F32),
        compiler_params=_cparams(("arbitrary",)),
        name=f"moe_group{g}",
    )(cnt, hb, slab, rank_r, w['w_gate'], w['w_up'], w['w_down'], y, w['ln2_g'], w['ln2_b'])


def _moe(h, hb, slab, rank_r, cnt, w, *, tm, alpha):
    y = h
    for g in range(N_GROUPS):
        y = _moe_group(g, y, hb, slab, rank_r, cnt, w, tm=tm, alpha=alpha)
    return y


def _rope_tables(pos):
    pos = pos.astype(F32)[:, None]
    lane = jnp.arange(LANES)
    inv_m = ROPE_THETA ** (-jnp.arange(0, MLA_ROPE, 2, dtype=F32) / MLA_ROPE)
    ang_m = pos * inv_m
    inv_d = ROPE_THETA ** (-jnp.arange(0, DIFF_ROT, 2, dtype=F32) / DIFF_ROT)
    ang_d = pos * inv_d
    narrow = lax.optimization_barrier((jnp.cos(ang_m), jnp.sin(ang_m), jnp.cos(ang_d), jnp.sin(ang_d)))
    tabs = tuple(a[:, lane % a.shape[1]] for a in narrow)
    tabs_t = tuple(a.T for a in narrow)
    return tabs, tabs_t


def _prep_weights(l, w_in, mla_q_norm, mla_w_uq, mla_kv_norm, mla_w_ukv, diff_subln, w_out, ln1_g, ln1_b,
                  ln2_g, ln2_b, w_route_group, b_route_group, w_route_expert, b_route_expert,
                  w_exp_gate, w_exp_up, w_exp_down, lam_rows):
    wi = w_in[l]
    s1 = MLA_Q_RANK + MLA_KV_RANK
    s2 = s1 + MLA_ROPE
    s3 = s2 + DIFF_QK_WIDTH
    s4 = s3 + DIFF_QK_WIDTH
    w_in_r = jnp.concatenate([wi[:, :s1], wi[:, s3:s4], wi[:, s4:], jnp.zeros((D_MODEL, MLA_NOPE), F32),
                              wi[:, s1:s2], jnp.zeros((D_MODEL, HEAD_PAD - MLA_NOPE - MLA_ROPE), F32)], axis=1)
    w_uq = jnp.pad(mla_w_uq[l].reshape(MLA_Q_RANK, MLA_HEADS, MLA_NOPE + MLA_ROPE),
                   ((0, 0), (0, 0), (0, HEAD_PAD - MLA_NOPE - MLA_ROPE))).reshape(MLA_Q_RANK, -1)
    ukv = mla_w_ukv[l].reshape(MLA_KV_RANK, MLA_HEADS, MLA_NOPE + MLA_V)
    w_uk = jnp.pad(ukv[..., :MLA_NOPE], ((0, 0), (0, 0), (0, HEAD_PAD - MLA_NOPE))).reshape(MLA_KV_RANK, -1)
    w_uv = ukv[..., MLA_NOPE:].reshape(MLA_KV_RANK, -1)
    rows = jnp.arange(LANES)[:, None]
    cols = jnp.arange(MLA_HEADS * HEAD_PAD)[None, :]
    e_place = ((rows < MLA_ROPE) & (cols % HEAD_PAD == rows + MLA_NOPE)).astype(BF16)
    w_route = jnp.concatenate([w_route_expert[l], w_route_group[l],
                               jnp.zeros((D_MODEL, LANES - N_EXPERTS - N_GROUPS), F32)], axis=1)
    b_route = jnp.concatenate([b_route_expert[l], b_route_group[l],
                               jnp.zeros((LANES - N_EXPERTS - N_GROUPS,), F32)])[None, :]
    return dict(
        w_in=w_in_r.astype(BF16), w_dqt=wi[:, s2:s3].T.astype(BF16),
        q_norm=mla_q_norm[l][None, :], w_uqt=w_uq.T.astype(BF16),
        kv_norm=mla_kv_norm[l][None, :], w_ukv=jnp.concatenate([w_uk, w_uv], axis=1).astype(BF16),
        w_uvt=w_uv.T.astype(BF16),
        e_place=e_place, subln=diff_subln[l][None, :], lam_rows=lam_rows,
        w_out_a=w_out[l][:MLA_HEADS * MLA_V].astype(BF16), w_out_b=w_out[l][MLA_HEADS * MLA_V:].astype(BF16),
        ln1_g=ln1_g[l][None, :], ln1_b=ln1_b[l][None, :], ln2_g=ln2_g[l][None, :], ln2_b=ln2_b[l][None, :],
        w_route=w_route.astype(BF16), b_route=b_route,
        w_gate=w_exp_gate[l], w_up=w_exp_up[l], w_down=w_exp_down[l])


def _block_output(x, mla_o, diff_o, w, alpha):
    t = x.shape[0]
    tm = min(MOE_TILE, t)
    h, hb, slab, rank_r, cnt = _post_attn(x, mla_o.reshape(t, -1), diff_o.reshape(t, -1), w, tm=tm, alpha=alpha)
    cnt = cnt[:, 0, :N_GROUPS].astype(jnp.int32)
    return _moe(h, hb, slab, rank_r, cnt, w, tm=tm, alpha=alpha)


def kernel(x_prompt, x_sample, cache_mla_ckv, cache_mla_kpe, cache_diff_k, cache_diff_v, w_in, mla_q_norm, mla_w_uq, mla_kv_norm, mla_w_ukv, diff_lambda_q1, diff_lambda_k1, diff_lambda_q2, diff_lambda_k2, diff_subln, w_out, ln1_g, ln1_b, ln2_g, ln2_b, w_route_group, b_route_group, w_route_expert, b_route_expert, w_exp_gate, w_exp_up, w_exp_down):
    depth = w_in.shape[0]
    bp, sp, _ = x_prompt.shape
    bs, ss, _ = x_sample.shape
    past = cache_mla_ckv.shape[2]
    alpha = (2.0 * depth) ** 0.25
    tabs_p, tabs_pt = _rope_tables(jnp.arange(sp))
    tabs_s, tabs_st = _rope_tables(past + jnp.arange(ss))
    hp = x_prompt.reshape(bp * sp, D_MODEL)
    hs = x_sample.reshape(bs * ss, D_MODEL)
    outs = [[] for _ in range(8)]
    for l in range(depth):
        lam_init = 0.8 - 0.6 * math.exp(-0.3 * l)
        lam_rows = jnp.stack([diff_lambda_q1[l], diff_lambda_k1[l], diff_lambda_q2[l], diff_lambda_k2[l]])
        w = _prep_weights(l, w_in, mla_q_norm, mla_w_uq, mla_kv_norm, mla_w_ukv, diff_subln, w_out,
                          ln1_g, ln1_b, ln2_g, ln2_b, w_route_group, b_route_group, w_route_expert,
                          b_route_expert, w_exp_gate, w_exp_up, w_exp_down, lam_rows)

        ckv, kpe, dk, dv, qt, km, _, vt, dqt, dkb, _, dvt = _proj(hp, tabs_p, tabs_pt, w, tm=PROJ_TILE)
        b3 = lambda a: a.reshape(bp, sp, -1)
        mla_o = _mla_flash(qt, b3(km), vt, t=ATTN_TILE, heads=MLA_HEADS_PER_STEP, nq=ATTN_QTILES)
        diff_o = _diff_flash(w['lam_rows'], w['subln'], dqt, b3(dkb), dvt, t=ATTN_TILE, heads=DIFF_HEADS_PER_STEP,
                             nq=ATTN_QTILES, lam_init=lam_init)
        outs[0].append(ckv.reshape(bp, sp, MLA_KV_RANK))
        outs[1].append(kpe.reshape(bp, sp, MLA_ROPE))
        outs[2].append(dk.reshape(bp, sp, DIFF_HEADS, 2 * DIFF_D))
        outs[3].append(dv.reshape(bp, sp, DIFF_HEADS, DIFF_V))
        hp = _block_output(hp, mla_o, diff_o, w, alpha)

        ckv_s, kpe_s, dk_s, dv_s, qt_s, km_s, vm_s, _, dqt_s, dkb_s, dvb_s, _ = _proj(hs, tabs_s, tabs_st, w, tm=ss)
        s3 = lambda a: a.reshape(bs, ss, -1)
        mla_o_s = _mla_step(qt_s, cache_mla_ckv[l], cache_mla_kpe[l], s3(km_s), s3(vm_s), w)
        diff_o_s = _diff_step(w['lam_rows'], w['subln'], dqt_s, cache_diff_k[l], cache_diff_v[l],
                              s3(dkb_s), s3(dvb_s), lam_init=lam_init)
        outs[4].append(ckv_s.reshape(bs, ss, MLA_KV_RANK))
        outs[5].append(kpe_s.reshape(bs, ss, MLA_ROPE))
        outs[6].append(dk_s.reshape(bs, ss, DIFF_HEADS, 2 * DIFF_D))
        outs[7].append(dv_s.reshape(bs, ss, DIFF_HEADS, DIFF_V))
        hs = _block_output(hs, mla_o_s, diff_o_s, w, alpha)

    return (hp.reshape(bp, sp, D_MODEL), hs.reshape(bs, ss, D_MODEL)) + tuple(jnp.stack(o) for o in outs)
```
